```python
import math
import jax, jax.numpy as jnp
from jax import lax
import numpy as np

D_MODEL = 1024
BATCH = 16
SEQ = 2048
DEPTH = 1
DEC_BATCH = 32
DEC_SEQ = 32
PAST_LEN = 2048

CHUNK = 64
Q_BLOCK = 128
N_HEADS_A = D_MODEL // 256
HEAD_DIM_A = 64
V_DIM_A = 2 * HEAD_DIM_A
QK_WIDTH = N_HEADS_A * 2 * HEAD_DIM_A
WIDTH_A = N_HEADS_A * V_DIM_A
WIDTH_B = D_MODEL - WIDTH_A
A_COLS = 2 * QK_WIDTH + WIDTH_A
IN_COLS = A_COLS + 2 * WIDTH_B
CONV_WIDTH = 31
CONV_PAD = CONV_WIDTH - 1
D_FF = 4 * D_MODEL
N_BUCKETS = 32
MAX_DISTANCE = 128
EPS = 1e-6
NEG_INF = -1e30

kernel_name = 'hymba_diffattn_conformer_stream_step'


def _rms_norm(x, g):
    xf = x.astype(jnp.float32)
    y = xf * lax.rsqrt(jnp.mean(xf * xf, axis=-1, keepdims=True) + EPS)
    return (y * g.astype(jnp.float32)).astype(x.dtype)


def _layer_norm(x, g, b):
    xf = x.astype(jnp.float32)
    mu = jnp.mean(xf, axis=-1, keepdims=True)
    var = jnp.mean(jnp.square(xf - mu), axis=-1, keepdims=True)
    y = (xf - mu) * lax.rsqrt(var + EPS)
    return (y * g.astype(jnp.float32) + b.astype(jnp.float32)).astype(x.dtype)


def _rel_bucket(rel):
    half = N_BUCKETS // 2
    max_exact = half // 2
    n = -rel
    ret = jnp.where(n < 0, half, 0)
    n = jnp.abs(n)
    nf = jnp.maximum(n, 1).astype(jnp.float32)
    large = max_exact + (jnp.log(nf / max_exact) / math.log(MAX_DISTANCE / max_exact)
                         * (half - max_exact)).astype(jnp.int32)
    large = jnp.minimum(large, half - 1)
    return ret + jnp.where(n < max_exact, n, large)


def _diff_attention(q, k, v, past_len, rel_bias, lam, subln_g, lam_init):
    B, T = q.shape[0], q.shape[1]
    n_keys = k.shape[1]
    scale = HEAD_DIM_A ** -0.5
    qh = q.reshape(B, T, N_HEADS_A, 2, HEAD_DIM_A) * scale
    kh = k.reshape(B, n_keys, N_HEADS_A, 2, HEAD_DIM_A)
    outs = []
    for start in range(0, T, Q_BLOCK):
        stop = min(start + Q_BLOCK, T)
        k_end = past_len + stop
        q_pos = past_len + jnp.arange(start, stop)
        k_pos = jnp.arange(k_end)
        bias = rel_bias[_rel_bucket(k_pos[None, :] - q_pos[:, None])]
        bias = jnp.transpose(bias, (2, 0, 1)).astype(jnp.float32)
        mask = (k_pos[None, :] // CHUNK) <= (q_pos[:, None] // CHUNK)
        s = jnp.einsum('bqhmd,bkhmd->bhmqk', qh[:, start:stop], kh[:, :k_end]).astype(jnp.float32)
        s = jnp.where(mask, s + bias[None, :, None], NEG_INF)
        p = jax.nn.softmax(s, axis=-1)
        attn = p[:, :, 0] - lam * p[:, :, 1]
        outs.append(jnp.einsum('bhqk,bkhd->bqhd', attn.astype(v.dtype), v[:, :k_end]))
    o = jnp.concatenate(outs, axis=1)
    o = _rms_norm(o, subln_g) * (1.0 - lam_init)
    return o.reshape(B, T, WIDTH_A)


def _causal_depthwise_conv(u_full, w_dw, b_dw):
    y = lax.conv_general_dilated(u_full, w_dw[:, None, :].astype(u_full.dtype), window_strides=(1,),
                                 padding='VALID', dimension_numbers=('NWC', 'WIO', 'NWC'),
                                 feature_group_count=WIDTH_B)
    return y + b_dw


def _layer(x, past_k, past_v, conv_prefix, rel_bias, lam_init, ln1_g, w_in, lq1, lk1, lq2, lk2,
           subln_g, w_dw, b_dw, cln_g, cln_b, w_out, ln2_g, w_up, w_down):
    B, T, _ = x.shape
    h = _rms_norm(x, ln1_g)
    proj = h @ w_in
    q = proj[..., :QK_WIDTH].reshape(B, T, N_HEADS_A, 2 * HEAD_DIM_A)
    k = proj[..., QK_WIDTH:2 * QK_WIDTH].reshape(B, T, N_HEADS_A, 2 * HEAD_DIM_A)
    v = proj[..., 2 * QK_WIDTH:A_COLS].reshape(B, T, N_HEADS_A, V_DIM_A)
    u = proj[..., A_COLS:A_COLS + WIDTH_B] * jax.nn.sigmoid(proj[..., A_COLS + WIDTH_B:])
    if past_k is None:
        past_len, k_all, v_all = 0, k, v
    else:
        past_len = past_k.shape[1]
        k_all = jnp.concatenate([past_k, k], axis=1)
        v_all = jnp.concatenate([past_v, v], axis=1)
    f32 = jnp.float32
    lam = (jnp.exp(jnp.sum(lq1.astype(f32) * lk1.astype(f32)))
           - jnp.exp(jnp.sum(lq2.astype(f32) * lk2.astype(f32))) + lam_init)
    a = _diff_attention(q, k_all, v_all, past_len, rel_bias, lam, subln_g, lam_init)
    u_full = jnp.concatenate([conv_prefix.astype(u.dtype), u], axis=1)
    c = _layer_norm(_causal_depthwise_conv(u_full, w_dw, b_dw), cln_g, cln_b)
    c = c * jax.nn.sigmoid(c)
    x = x + jnp.concatenate([a, c], axis=-1) @ w_out
    h2 = _rms_norm(x, ln2_g)
    x = x + jnp.square(jax.nn.relu(h2 @ w_up)) @ w_down
    return x, k, v, u_full[:, -CONV_PAD:]


def setup_inputs(seed: int = 0) -> dict:
    key = jax.random.key(seed)
    ks = jax.random.split(key, 24)
    n = jax.random.normal
    f = jnp.float32
    H = N_HEADS_A
    return {
        'x_prompt': n(ks[0], (BATCH, SEQ, D_MODEL), f),
        'x_sample': n(ks[1], (DEC_BATCH, DEC_SEQ, D_MODEL), f),
        'cache_k': n(ks[2], (DEPTH, DEC_BATCH, PAST_LEN, H, 2 * HEAD_DIM_A), f),
        'cache_v': n(ks[3], (DEPTH, DEC_BATCH, PAST_LEN, H, V_DIM_A), f),
        'state_conv': 0.5 * n(ks[4], (DEPTH, DEC_BATCH, CONV_PAD, WIDTH_B), f),
        'rel_bias': 0.5 * n(ks[5], (N_BUCKETS, H), f),
        'ln1_g': 1.0 + 0.05 * n(ks[6], (DEPTH, D_MODEL), f),
        'w_in': n(ks[7], (DEPTH, D_MODEL, IN_COLS), f) * D_MODEL ** -0.5,
        'lambda_q1': 0.1 * n(ks[8], (DEPTH, HEAD_DIM_A), f),
        'lambda_k1': 0.1 * n(ks[9], (DEPTH, HEAD_DIM_A), f),
        'lambda_q2': 0.1 * n(ks[10], (DEPTH, HEAD_DIM_A), f),
        'lambda_k2': 0.1 * n(ks[11], (DEPTH, HEAD_DIM_A), f),
        'subln_g': 1.0 + 0.05 * n(ks[12], (DEPTH, V_DIM_A), f),
        'w_dw': n(ks[13], (DEPTH, CONV_WIDTH, WIDTH_B), f) * CONV_WIDTH ** -0.5,
        'b_dw': 0.02 * n(ks[14], (DEPTH, WIDTH_B), f),
        'conv_ln_g': 1.0 + 0.05 * n(ks[15], (DEPTH, WIDTH_B), f),
        'conv_ln_b': 0.02 * n(ks[16], (DEPTH, WIDTH_B), f),
        'w_out': n(ks[17], (DEPTH, WIDTH_A + WIDTH_B, D_MODEL), f) * (WIDTH_A + WIDTH_B) ** -0.5,
        'ln2_g': 1.0 + 0.05 * n(ks[18], (DEPTH, D_MODEL), f),
        'w_up': n(ks[19], (DEPTH, D_MODEL, D_FF), f) * D_MODEL ** -0.5,
        'w_down': n(ks[20], (DEPTH, D_FF, D_MODEL), f) * D_FF ** -0.5,
        'ln_f_g': 1.0 + 0.05 * n(ks[21], (D_MODEL,), f),
    }


def reference(x_prompt, x_sample, cache_k, cache_v, state_conv, rel_bias, ln1_g, w_in,
              lambda_q1, lambda_k1, lambda_q2, lambda_k2, subln_g, w_dw, b_dw, conv_ln_g,
              conv_ln_b, w_out, ln2_g, w_up, w_down, ln_f_g):
    yp, ys = x_prompt, x_sample
    kp, vp, cp, ksm, vsm, csm = [], [], [], [], [], []
    for l in range(DEPTH):
        lam_init = 0.8 - 0.6 * math.exp(-0.3 * l)
        wts = (ln1_g[l], w_in[l], lambda_q1[l], lambda_k1[l], lambda_q2[l], lambda_k2[l],
               subln_g[l], w_dw[l], b_dw[l], conv_ln_g[l], conv_ln_b[l], w_out[l], ln2_g[l],
               w_up[l], w_down[l])
        zero_prefix = jnp.zeros((yp.shape[0], CONV_PAD, WIDTH_B), yp.dtype)
        yp, k1, v1, c1 = _layer(yp, None, None, zero_prefix, rel_bias, lam_init, *wts)
        ys, k2, v2, c2 = _layer(ys, cache_k[l], cache_v[l], state_conv[l], rel_bias, lam_init, *wts)
        kp.append(k1); vp.append(v1); cp.append(c1)
        ksm.append(k2); vsm.append(v2); csm.append(c2)
    yp = _rms_norm(yp, ln_f_g)
    ys = _rms_norm(ys, ln_f_g)
    return (yp, ys, jnp.stack(kp), jnp.stack(vp), jnp.stack(cp), jnp.stack(ksm), jnp.stack(vsm), jnp.stack(csm))
```

```python
import functools
import math

import jax
import jax.numpy as jnp
from jax import lax
from jax.experimental import pallas as pl
from jax.experimental.pallas import tpu as pltpu

D_MODEL = 1024
CHUNK = 64
N_HEADS = 4
HEAD_DIM = 64
V_DIM = 2 * HEAD_DIM
QK_WIDTH = N_HEADS * 2 * HEAD_DIM
WIDTH_A = N_HEADS * V_DIM
WIDTH_B = D_MODEL - WIDTH_A
CONV_WIDTH = 31
CONV_PAD = CONV_WIDTH - 1
D_FF = 4 * D_MODEL
N_BUCKETS = 32
MAX_DISTANCE = 128
EPS = 1e-6
NEG_INF = -1e30
LAM_INIT = 0.8 - 0.6 * math.exp(-0.3 * 0)

SUBLANES = 8
HALO = 32
VMEM_LIMIT_BYTES = 56 * 1024 * 1024

F32 = jnp.float32
BF16 = jnp.bfloat16
_NT = (((1,), (1,)), ((), ()))


def _sigmoid(x):
    return 1.0 / (1.0 + jnp.exp(-x))


def _rms_norm_rows(x, g):
    return x * lax.rsqrt(jnp.mean(x * x, axis=-1, keepdims=True) + EPS) * g


def _params(*semantics):
    return pltpu.CompilerParams(dimension_semantics=semantics, vmem_limit_bytes=VMEM_LIMIT_BYTES)


def _resident(shape):
    zeros = (0,) * len(shape)
    return pl.BlockSpec(shape, lambda *_: zeros, pipeline_mode=pl.Buffered(1))


def _in_proj_kernel(x_ref, g_ref, w_ref, q_ref, k_ref, v_ref, kb_ref, vb_ref, u_ref):
    h = _rms_norm_rows(x_ref[...], g_ref[...]).astype(BF16)

    def cols(lo, hi):
        return jnp.dot(h, w_ref[:, lo:hi], preferred_element_type=F32)

    a_cols = 2 * QK_WIDTH + WIDTH_A
    q_ref[...] = (cols(0, QK_WIDTH) * HEAD_DIM ** -0.5).astype(BF16)
    k = cols(QK_WIDTH, 2 * QK_WIDTH)
    k_ref[...] = k
    kb_ref[...] = k.astype(BF16)
    v = cols(2 * QK_WIDTH, a_cols)
    v_ref[...] = v
    vb_ref[...] = v.astype(BF16)
    u_ref[...] = cols(a_cols, a_cols + WIDTH_B) * _sigmoid(cols(a_cols + WIDTH_B, a_cols + 2 * WIDTH_B))


def _in_proj(x, ln_g, w_in_bf, tm):
    n = x.shape[0]
    rows = lambda width: pl.BlockSpec((tm, width), lambda i: (i, 0))
    out = lambda dt: jax.ShapeDtypeStruct((n, QK_WIDTH), dt)
    return pl.pallas_call(
        _in_proj_kernel,
        grid=(n // tm,),
        in_specs=[rows(D_MODEL), _resident((1, D_MODEL)), _resident(w_in_bf.shape)],
        out_specs=[rows(QK_WIDTH)] * 6,
        out_shape=[out(BF16), out(F32), out(F32), out(BF16), out(BF16), out(F32)],
        compiler_params=_params("parallel"),
        name="in_proj",
    )(x, ln_g, w_in_bf)


def _lambda(lq1_ref, lk1_ref, lq2_ref, lk2_ref):
    d1 = jnp.sum(lq1_ref[...] * lk1_ref[...], axis=-1, keepdims=True)
    d2 = jnp.sum(lq2_ref[...] * lk2_ref[...], axis=-1, keepdims=True)
    return jnp.exp(d1) - jnp.exp(d2) + LAM_INIT


def _stack_maps(qh):
    lane = lax.broadcasted_iota(jnp.int32, qh.shape, 1)
    zero = jnp.zeros_like(qh)
    return jnp.concatenate([jnp.where(lane < HEAD_DIM, qh, zero), jnp.where(lane >= HEAD_DIM, qh, zero)], axis=0)


def _with_ones(v):
    return jnp.concatenate([v, jnp.ones_like(v)], axis=1)


def _finish_head(acc, lam, g, tq):
    o1 = acc[:tq, :V_DIM] / acc[:tq, V_DIM:]
    o2 = acc[tq:, :V_DIM] / acc[tq:, V_DIM:]
    o = o1 - lam * o2
    return _rms_norm_rows(o, g) * (1.0 - LAM_INIT)


def _prompt_attn_kernel(q_ref, k_ref, v_ref, bd_ref, bn_ref, lq1_ref, lk1_ref, lq2_ref, lk2_ref, g_ref,
                        o_ref, m_sc, acc_sc, *, tq):
    i = pl.program_id(1)
    lam = _lambda(lq1_ref, lk1_ref, lq2_ref, lk2_ref)
    row = lax.broadcasted_iota(jnp.int32, (tq, tq), 0)
    col = lax.broadcasted_iota(jnp.int32, (tq, tq), 1)
    visible = (col // CHUNK) <= (row // CHUNK)
    visible2 = jnp.concatenate([visible, visible], axis=0)

    for h in range(N_HEADS):
        hs = slice(h * V_DIM, (h + 1) * V_DIM)
        q2 = _stack_maps(q_ref[:, hs])
        m_sc[...] = jnp.full(m_sc.shape, NEG_INF, F32)
        acc_sc[...] = jnp.zeros(acc_sc.shape, F32)

        def key_tile(start, bias, mask, q2=q2, hs=hs):
            s = lax.dot_general(q2, k_ref[pl.ds(start, tq), hs], _NT, preferred_element_type=F32)
            if bias is not None:
                s = s + jnp.concatenate([bias, bias], axis=0)
            if mask is not None:
                s = jnp.where(mask, s, NEG_INF)
            m_prev = m_sc[...]
            m_new = jnp.maximum(m_prev, jnp.max(s, axis=1, keepdims=True))
            alpha = jnp.exp(m_prev - m_new)
            p = jnp.exp(s - jnp.concatenate([m_new] * (tq // 128), axis=1))
            pv = jnp.dot(p.astype(BF16), _with_ones(v_ref[pl.ds(start, tq), hs]), preferred_element_type=F32)
            acc_sc[...] = jnp.concatenate([alpha, alpha], axis=1) * acc_sc[...] + pv
            m_sc[...] = m_new

        def far_tile(j, carry):
            key_tile(pl.multiple_of(j * tq, tq), None, None)
            return carry

        lax.fori_loop(0, jnp.maximum(i - 1, 0), far_tile, 0)

        @pl.when(i >= 1)
        def _(key_tile=key_tile, h=h):
            key_tile(pl.multiple_of((i - 1) * tq, tq), bn_ref[h], None)

        key_tile(pl.multiple_of(i * tq, tq), bd_ref[h], visible2)
        o_ref[:, hs] = _finish_head(acc_sc[...], lam, g_ref[...], tq).astype(o_ref.dtype)


def _prompt_attn(q, kb, vb, bias_diag, bias_near, lam_params, subln_g, batch, seq, tq):
    nq = seq // tq
    small = [_resident(p.shape) for p in lam_params]
    return pl.pallas_call(
        functools.partial(_prompt_attn_kernel, tq=tq),
        grid=(batch, nq),
        in_specs=[pl.BlockSpec((tq, QK_WIDTH), lambda b, i: (b * nq + i, 0)),
                  pl.BlockSpec((seq, QK_WIDTH), lambda b, i: (b, 0)),
                  pl.BlockSpec((seq, WIDTH_A), lambda b, i: (b, 0)),
                  _resident(bias_diag.shape), _resident(bias_near.shape), *small, _resident(subln_g.shape)],
        out_specs=pl.BlockSpec((tq, WIDTH_A), lambda b, i: (b * nq + i, 0)),
        out_shape=jax.ShapeDtypeStruct((batch * seq, WIDTH_A), BF16),
        scratch_shapes=[pltpu.VMEM((2 * tq, 128), F32), pltpu.VMEM((2 * tq, 2 * V_DIM), F32)],
        compiler_params=_params("parallel", "arbitrary"),
        name="prompt_attn",
    )(q, kb, vb, bias_diag, bias_near, *lam_params, subln_g)


def _sample_attn_kernel(q_ref, ck_ref, cv_ref, kn_ref, vn_ref, bp_ref, bn_ref, lq1_ref, lk1_ref, lq2_ref, lk2_ref,
                        g_ref, o_ref, *, past_len):
    t = q_ref.shape[0]
    lam = _lambda(lq1_ref, lk1_ref, lq2_ref, lk2_ref)
    q_pos = past_len + lax.broadcasted_iota(jnp.int32, (t, t), 0)
    k_pos = past_len + lax.broadcasted_iota(jnp.int32, (t, t), 1)
    visible = (k_pos // CHUNK) <= (q_pos // CHUNK)
    visible2 = jnp.concatenate([visible, visible], axis=0)

    for h in range(N_HEADS):
        hs = slice(h * V_DIM, (h + 1) * V_DIM)
        q2 = _stack_maps(q_ref[:, hs])
        bias_past = bp_ref[h]
        s_past = lax.dot_general(q2, ck_ref[0, :, hs].astype(BF16), _NT, preferred_element_type=F32)
        s_past = s_past + jnp.concatenate([bias_past, bias_past], axis=0)
        bias_new = bn_ref[h]
        s_new = lax.dot_general(q2, kn_ref[:, hs], _NT, preferred_element_type=F32)
        s_new = jnp.where(visible2, s_new + jnp.concatenate([bias_new, bias_new], axis=0), NEG_INF)
        m = jnp.maximum(jnp.max(s_past, axis=1, keepdims=True), jnp.max(s_new, axis=1, keepdims=True))
        p_past = jnp.exp(s_past - m).astype(BF16)
        p_new = jnp.exp(s_new - m).astype(BF16)
        acc = (jnp.dot(p_past, _with_ones(cv_ref[0, :, hs].astype(BF16)), preferred_element_type=F32)
               + jnp.dot(p_new, _with_ones(vn_ref[:, hs]), preferred_element_type=F32))
        o_ref[:, hs] = _finish_head(acc, lam, g_ref[...], t).astype(o_ref.dtype)


def _sample_attn(q, cache_k, cache_v, kb, vb, bias_past, bias_new, lam_params, subln_g):
    batch, past_len, _ = cache_k.shape
    t = q.shape[0] // batch
    small = [_resident(p.shape) for p in lam_params]
    new_rows = pl.BlockSpec((t, QK_WIDTH), lambda b: (b, 0))
    cached = pl.BlockSpec((1, past_len, QK_WIDTH), lambda b: (b, 0, 0))
    return pl.pallas_call(
        functools.partial(_sample_attn_kernel, past_len=past_len),
        grid=(batch,),
        in_specs=[new_rows, cached, cached, new_rows, new_rows,
                  _resident(bias_past.shape), _resident(bias_new.shape), *small, _resident(subln_g.shape)],
        out_specs=new_rows,
        out_shape=jax.ShapeDtypeStruct(q.shape, BF16),
        compiler_params=_params("parallel"),
        name="sample_attn",
    )(q, cache_k, cache_v, kb, vb, bias_past, bias_new, *lam_params, subln_g)


def _conv_kernel(pre_ref, halo_ref, u_ref, w_ref, b_ref, g_ref, beta_ref, c_ref, ubuf, *, tt, rc):
    t = pl.program_id(1)

    @pl.when(t == 0)
    def _():
        ubuf[0:HALO, :] = pre_ref[0]

    @pl.when(t > 0)
    def _():
        ubuf[0:HALO, :] = halo_ref[0]

    ubuf[HALO:HALO + tt, :] = u_ref[0]

    def row_chunk(r, carry):
        r0 = pl.multiple_of(r * rc, rc)
        y = b_ref[...]
        for s in range(SUBLANES):
            taps = [(a, SUBLANES * a + s - (HALO - CONV_PAD)) for a in range(HALO // SUBLANES + 1)]
            taps = [(a, j) for a, j in taps if 0 <= j < CONV_WIDTH]
            rows = rc + (SUBLANES if s else 0)
            terms = [ubuf[pl.ds(r0 + SUBLANES * a, rows), :] * w_ref[j:j + 1, :] for a, j in taps]
            y = y + functools.reduce(jnp.add, terms)[s:s + rc]
        mu = jnp.mean(y, axis=-1, keepdims=True)
        var = jnp.mean(jnp.square(y - mu), axis=-1, keepdims=True)
        yn = (y - mu) * lax.rsqrt(var + EPS) * g_ref[...] + beta_ref[...]
        c_ref[0, pl.ds(r0, rc), :] = (yn * _sigmoid(yn)).astype(c_ref.dtype)
        return carry

    lax.fori_loop(0, tt // rc, row_chunk, 0)


def _conv(prefix, u, w_dw, b_dw, ln_g, ln_b, tt, rc):
    batch, seq, _ = u.shape
    halo_blocks = tt // HALO
    small = [_resident(p.shape) for p in (w_dw, b_dw, ln_g, ln_b)]
    return pl.pallas_call(
        functools.partial(_conv_kernel, tt=tt, rc=rc),
        grid=(batch, seq // tt),
        in_specs=[pl.BlockSpec((1, HALO, WIDTH_B), lambda b, t: (b, 0, 0)),
                  pl.BlockSpec((1, HALO, WIDTH_B), lambda b, t: (b, jnp.maximum(t * halo_blocks - 1, 0), 0)),
                  pl.BlockSpec((1, tt, WIDTH_B), lambda b, t: (b, t, 0)), *small],
        out_specs=pl.BlockSpec((1, tt, WIDTH_B), lambda b, t: (b, t, 0)),
        out_shape=jax.ShapeDtypeStruct(u.shape, BF16),
        scratch_shapes=[pltpu.VMEM((HALO + tt, WIDTH_B), F32)],
        compiler_params=_params("parallel", "arbitrary"),
        name="conv_ln_swish",
    )(prefix, u, u, w_dw, b_dw, ln_g, ln_b)


def _out_ffn_kernel(x_ref, a_ref, c_ref, wo_ref, g2_ref, wu_ref, wd_ref, gf_ref, y_ref, *, ff_chunk):
    x = (x_ref[...]
         + jnp.dot(a_ref[...], wo_ref[:WIDTH_A, :], preferred_element_type=F32)
         + jnp.dot(c_ref[...], wo_ref[WIDTH_A:, :], preferred_element_type=F32))
    h2 = _rms_norm_rows(x, g2_ref[...]).astype(BF16)
    ffn = None
    for lo in range(0, D_FF, ff_chunk):
        hid = jnp.dot(h2, wu_ref[:, lo:lo + ff_chunk], preferred_element_type=F32)
        hid = jnp.square(jnp.maximum(hid, 0.0)).astype(BF16)
        down = jnp.dot(hid, wd_ref[lo:lo + ff_chunk, :], preferred_element_type=F32)
        ffn = down if ffn is None else ffn + down
    y_ref[...] = _rms_norm_rows(x + ffn, gf_ref[...])


def _out_ffn(x, a, c, w_out_bf, ln2_g, w_up_bf, w_down_bf, ln_f_g, tm, ff_chunk):
    n = x.shape[0]
    rows = lambda width: pl.BlockSpec((tm, width), lambda i: (i, 0))
    return pl.pallas_call(
        functools.partial(_out_ffn_kernel, ff_chunk=ff_chunk),
        grid=(n // tm,),
        in_specs=[rows(D_MODEL), rows(WIDTH_A), rows(WIDTH_B), _resident(w_out_bf.shape), _resident(ln2_g.shape),
                  _resident(w_up_bf.shape), _resident(w_down_bf.shape), _resident(ln_f_g.shape)],
        out_specs=rows(D_MODEL),
        out_shape=jax.ShapeDtypeStruct(x.shape, F32),
        compiler_params=_params("parallel"),
        name="out_ffn",
    )(x, a, c, w_out_bf, ln2_g, w_up_bf, w_down_bf, ln_f_g)


def _bucket_of(rel):
    half = N_BUCKETS // 2
    max_exact = half // 2
    n = -rel
    ret = jnp.where(n < 0, half, 0)
    n = jnp.abs(n)
    nf = jnp.maximum(n, 1).astype(F32)
    large = max_exact + (jnp.log(nf / max_exact) / math.log(MAX_DISTANCE / max_exact)
                         * (half - max_exact)).astype(jnp.int32)
    large = jnp.minimum(large, half - 1)
    return ret + jnp.where(n < max_exact, n, large)


def _bias_table(rel_bias, q_pos, k_pos, far_rel):
    table = rel_bias[_bucket_of(k_pos[None, :] - q_pos[:, None])] - rel_bias[_bucket_of(far_rel)]
    return jnp.transpose(table, (2, 0, 1)).astype(F32)


def kernel(x_prompt, x_sample, cache_k, cache_v, state_conv, rel_bias, ln1_g, w_in, lambda_q1, lambda_k1, lambda_q2,
           lambda_k2, subln_g, w_dw, b_dw, conv_ln_g, conv_ln_b, w_out, ln2_g, w_up, w_down, ln_f_g):
    batch, seq, _ = x_prompt.shape
    dec_batch, dec_seq, _ = x_sample.shape
    past_len = cache_k.shape[2]
    tq = 256
    assert w_in.shape[0] == 1 and seq % tq == 0 and past_len % CHUNK == 0

    w_in_bf, w_out_bf = w_in[0].astype(BF16), w_out[0].astype(BF16)
    w_up_bf, w_down_bf = w_up[0].astype(BF16), w_down[0].astype(BF16)
    lam_params = (lambda_q1, lambda_k1, lambda_q2, lambda_k2)

    far_rel = jnp.int32(-(2 * tq))
    pos = jnp.arange(tq)
    bias_diag = _bias_table(rel_bias, pos + tq, pos + tq, far_rel)
    bias_near = _bias_table(rel_bias, pos + tq, pos, far_rel)
    q_pos_s = past_len + jnp.arange(dec_seq)
    bias_past = _bias_table(rel_bias, q_pos_s, jnp.arange(past_len), far_rel)
    bias_new = _bias_table(rel_bias, q_pos_s, q_pos_s, far_rel)

    xp = x_prompt.reshape(batch * seq, D_MODEL)
    q_p, k_p, v_p, kb_p, vb_p, u_p = _in_proj(xp, ln1_g, w_in_bf, tm=512)
    a_p = _prompt_attn(q_p, kb_p, vb_p, bias_diag, bias_near, lam_params, subln_g, batch, seq, tq)
    u_p3 = u_p.reshape(batch, seq, WIDTH_B)
    c_p = _conv(jnp.zeros((batch, HALO, WIDTH_B), F32), u_p3, w_dw[0], b_dw, conv_ln_g, conv_ln_b, tt=512, rc=16)
    y_p = _out_ffn(xp, a_p, c_p.reshape(batch * seq, WIDTH_B), w_out_bf, ln2_g, w_up_bf, w_down_bf,
                   ln_f_g[None, :], tm=512, ff_chunk=1024)

    xs = x_sample.reshape(dec_batch * dec_seq, D_MODEL)
    q_s, k_s, v_s, kb_s, vb_s, u_s = _in_proj(xs, ln1_g, w_in_bf, tm=512)
    a_s = _sample_attn(q_s, cache_k[0].reshape(dec_batch, past_len, QK_WIDTH),
                       cache_v[0].reshape(dec_batch, past_len, WIDTH_A), kb_s, vb_s, bias_past, bias_new,
                       lam_params, subln_g)
    u_s3 = u_s.reshape(dec_batch, dec_seq, WIDTH_B)
    prefix_s = jnp.pad(state_conv[0], ((0, 0), (HALO - CONV_PAD, 0), (0, 0)))
    c_s = _conv(prefix_s, u_s3, w_dw[0], b_dw, conv_ln_g, conv_ln_b, tt=dec_seq, rc=16)
    y_s = _out_ffn(xs, a_s, c_s.reshape(dec_batch * dec_seq, WIDTH_B), w_out_bf, ln2_g, w_up_bf, w_down_bf,
                   ln_f_g[None, :], tm=512, ff_chunk=1024)

    heads = lambda t, b, s: t.reshape(1, b, s, N_HEADS, V_DIM)
    return (y_p.reshape(batch, seq, D_MODEL), y_s.reshape(dec_batch, dec_seq, D_MODEL),
            heads(k_p, batch, seq), heads(v_p, batch, seq), u_p3[None, :, seq - CONV_PAD:],
            heads(k_s, dec_batch, dec_seq), heads(v_s, dec_batch, dec_seq), u_s3[None, :, dec_seq - CONV_PAD:])
```

```python
import functools
import math

import jax
import jax.numpy as jnp
from jax import lax
from jax.experimental import pallas as pl
from jax.experimental.pallas import tpu as pltpu

D_MODEL = 1024
CHUNK = 64
N_HEADS = 4
HEAD_DIM = 64
V_DIM = 2 * HEAD_DIM
QK_WIDTH = N_HEADS * 2 * HEAD_DIM
WIDTH_A = N_HEADS * V_DIM
WIDTH_B = D_MODEL - WIDTH_A
CONV_WIDTH = 31
CONV_PAD = CONV_WIDTH - 1
D_FF = 4 * D_MODEL
N_BUCKETS = 32
MAX_DISTANCE = 128
EPS = 1e-6
NEG_INF = -1e30
LAM_INIT = 0.8 - 0.6 * math.exp(-0.3 * 0)

LANES = 128
SUBLANES = 8
HALO = 32
VMEM_LIMIT_BYTES = 56 * 1024 * 1024

F32 = jnp.float32
BF16 = jnp.bfloat16
_NT = (((1,), (1,)), ((), ()))


def _sigmoid(x):
    return 1.0 / (1.0 + jnp.exp(-x))


def _rms_norm_rows(x, g):
    return x * lax.rsqrt(jnp.mean(x * x, axis=-1, keepdims=True) + EPS) * g


def _params(*semantics):
    return pltpu.CompilerParams(dimension_semantics=semantics, vmem_limit_bytes=VMEM_LIMIT_BYTES)


def _resident(shape):
    zeros = (0,) * len(shape)
    return pl.BlockSpec(shape, lambda *_: zeros, pipeline_mode=pl.Buffered(1))


def _head_cols(h):
    return slice(h * V_DIM, (h + 1) * V_DIM)


def _head_rows(h, n):
    return pl.ds(h, n, stride=N_HEADS)


def _in_proj_kernel(x_ref, g_ref, w_ref, q_ref, k_ref, v_ref, kb_ref, vb_ref, u_ref):
    tm = x_ref.shape[0]
    h = _rms_norm_rows(x_ref[...], g_ref[...]).astype(BF16)

    def cols(lo, hi):
        return jnp.dot(h, w_ref[:, lo:hi], preferred_element_type=F32)

    a_cols = 2 * QK_WIDTH + WIDTH_A
    q_ref[...] = (cols(0, QK_WIDTH) * HEAD_DIM ** -0.5).astype(BF16)
    for full, out_ref, bf_ref in ((cols(QK_WIDTH, 2 * QK_WIDTH), k_ref, kb_ref),
                                  (cols(2 * QK_WIDTH, a_cols), v_ref, vb_ref)):
        bf_ref[...] = full.astype(BF16)
        for hd in range(N_HEADS):
            out_ref[_head_rows(hd, tm), :] = full[:, _head_cols(hd)]
    u_ref[...] = cols(a_cols, a_cols + WIDTH_B) * _sigmoid(cols(a_cols + WIDTH_B, a_cols + 2 * WIDTH_B))


def _in_proj(x, ln_g, w_in_bf, tm):
    n = x.shape[0]
    rows = lambda width: pl.BlockSpec((tm, width), lambda i: (i, 0))
    by_head = pl.BlockSpec((tm * N_HEADS, V_DIM), lambda i: (i, 0))
    wide = lambda dt: jax.ShapeDtypeStruct((n, QK_WIDTH), dt)
    tall = jax.ShapeDtypeStruct((n * N_HEADS, V_DIM), F32)
    return pl.pallas_call(
        _in_proj_kernel,
        grid=(n // tm,),
        in_specs=[rows(D_MODEL), _resident((1, D_MODEL)), _resident(w_in_bf.shape)],
        out_specs=[rows(QK_WIDTH), by_head, by_head, rows(QK_WIDTH), rows(WIDTH_A), rows(WIDTH_B)],
        out_shape=[wide(BF16), tall, tall, wide(BF16), wide(BF16), wide(F32)],
        compiler_params=_params("parallel"),
        name="in_proj",
    )(x, ln_g, w_in_bf)


def _lambda(lq1_ref, lk1_ref, lq2_ref, lk2_ref):
    d1 = jnp.sum(lq1_ref[...] * lk1_ref[...], axis=-1, keepdims=True)
    d2 = jnp.sum(lq2_ref[...] * lk2_ref[...], axis=-1, keepdims=True)
    return jnp.exp(d1) - jnp.exp(d2) + LAM_INIT


def _stack_maps(qh):
    lane = lax.broadcasted_iota(jnp.int32, qh.shape, 1)
    zero = jnp.zeros_like(qh)
    return jnp.concatenate([jnp.where(lane < HEAD_DIM, qh, zero), jnp.where(lane >= HEAD_DIM, qh, zero)], axis=0)


def _with_ones(v):
    return jnp.concatenate([v, jnp.ones_like(v)], axis=1)


def _finish_head(acc, lam, g, tq):
    o1 = acc[:tq, :V_DIM] / acc[:tq, V_DIM:]
    o2 = acc[tq:, :V_DIM] / acc[tq:, V_DIM:]
    o = o1 - lam * o2
    return _rms_norm_rows(o, g) * (1.0 - LAM_INIT)


def _prompt_attn_kernel(q_ref, k_ref, v_ref, bt_ref, lq1_ref, lk1_ref, lq2_ref, lk2_ref, g_ref,
                        o_ref, q2_sc, m_sc, acc_sc, *, tq):
    i = pl.program_id(1)
    for h in range(N_HEADS):
        q2_sc[h] = _stack_maps(q_ref[:, _head_cols(h)])
    m_sc[...] = jnp.full(m_sc.shape, NEG_INF, F32)
    acc_sc[...] = jnp.zeros(acc_sc.shape, F32)

    def key_tile(start, tk, bias_of_head):
        for h in range(N_HEADS):
            hs = _head_cols(h)
            s = lax.dot_general(q2_sc[h], k_ref[pl.ds(start, tk), hs], _NT, preferred_element_type=F32)
            if bias_of_head is not None:
                bias = bias_of_head(h)
                s = s + jnp.concatenate([bias, bias], axis=0)
            m_prev = m_sc[h]
            m_new = jnp.maximum(m_prev, jnp.max(s, axis=1, keepdims=True))
            alpha = jnp.exp(m_prev - m_new)
            p = jnp.exp(s - jnp.concatenate([m_new] * (tk // LANES), axis=1))
            pv = jnp.dot(p.astype(BF16), _with_ones(v_ref[pl.ds(start, tk), hs]), preferred_element_type=F32)
            acc_sc[h] = jnp.concatenate([alpha, alpha], axis=1) * acc_sc[h] + pv
            m_sc[h] = m_new

    n_far = jnp.maximum(i - 1, 0)

    def far_pair(j, carry):
        key_tile(pl.multiple_of(j * 2 * tq, 2 * tq), 2 * tq, None)
        return carry

    lax.fori_loop(0, n_far // 2, far_pair, 0)

    @pl.when(n_far % 2 == 1)
    def _():
        key_tile(pl.multiple_of((n_far - 1) * tq, tq), tq, None)

    @pl.when(i >= 1)
    def _():
        key_tile(pl.multiple_of((i - 1) * tq, tq), 2 * tq, lambda h: bt_ref[h])

    @pl.when(i == 0)
    def _():
        key_tile(0, tq, lambda h: bt_ref[h, :, tq:])

    lam = _lambda(lq1_ref, lk1_ref, lq2_ref, lk2_ref)
    for h in range(N_HEADS):
        o_ref[:, _head_cols(h)] = _finish_head(acc_sc[h], lam, g_ref[...], tq).astype(o_ref.dtype)


def _prompt_attn(q, kb, vb, bias_tail, lam_params, subln_g, batch, seq, tq):
    nq = seq // tq
    small = [_resident(p.shape) for p in lam_params]
    return pl.pallas_call(
        functools.partial(_prompt_attn_kernel, tq=tq),
        grid=(batch, nq),
        in_specs=[pl.BlockSpec((tq, QK_WIDTH), lambda b, i: (b * nq + i, 0)),
                  pl.BlockSpec((seq, QK_WIDTH), lambda b, i: (b, 0)),
                  pl.BlockSpec((seq, WIDTH_A), lambda b, i: (b, 0)),
                  _resident(bias_tail.shape), *small, _resident(subln_g.shape)],
        out_specs=pl.BlockSpec((tq, WIDTH_A), lambda b, i: (b * nq + i, 0)),
        out_shape=jax.ShapeDtypeStruct((batch * seq, WIDTH_A), BF16),
        scratch_shapes=[pltpu.VMEM((N_HEADS, 2 * tq, V_DIM), BF16), pltpu.VMEM((N_HEADS, 2 * tq, LANES), F32),
                        pltpu.VMEM((N_HEADS, 2 * tq, 2 * V_DIM), F32)],
        compiler_params=_params("parallel", "arbitrary"),
        name="prompt_attn",
    )(q, kb, vb, bias_tail, *lam_params, subln_g)


def _sample_attn_kernel(q_ref, ck_ref, cv_ref, kn_ref, vn_ref, bp_ref, bn_ref, lq1_ref, lk1_ref, lq2_ref, lk2_ref,
                        g_ref, o_ref, *, past_len):
    t = q_ref.shape[0]
    lam = _lambda(lq1_ref, lk1_ref, lq2_ref, lk2_ref)
    for h in range(N_HEADS):
        hs = _head_cols(h)
        q2 = _stack_maps(q_ref[:, hs])
        bias_past, bias_new = bp_ref[h], bn_ref[h]
        k_past = ck_ref[0, _head_rows(h, past_len), :].astype(BF16)
        s_past = lax.dot_general(q2, k_past, _NT, preferred_element_type=F32)
        s_past = s_past + jnp.concatenate([bias_past, bias_past], axis=0)
        s_new = lax.dot_general(q2, kn_ref[:, hs], _NT, preferred_element_type=F32)
        s_new = s_new + jnp.concatenate([bias_new, bias_new], axis=0)
        m = jnp.maximum(jnp.max(s_past, axis=1, keepdims=True), jnp.max(s_new, axis=1, keepdims=True))
        p_past = jnp.exp(s_past - m).astype(BF16)
        p_new = jnp.exp(s_new - m).astype(BF16)
        v_past = cv_ref[0, _head_rows(h, past_len), :].astype(BF16)
        acc = (jnp.dot(p_past, _with_ones(v_past), preferred_element_type=F32)
               + jnp.dot(p_new, _with_ones(vn_ref[:, hs]), preferred_element_type=F32))
        o_ref[:, hs] = _finish_head(acc, lam, g_ref[...], t).astype(o_ref.dtype)


def _sample_attn(q, cache_k, cache_v, kb, vb, bias_past, bias_new, lam_params, subln_g, past_len):
    batch = cache_k.shape[0]
    t = q.shape[0] // batch
    small = [_resident(p.shape) for p in lam_params]
    new_rows = pl.BlockSpec((t, QK_WIDTH), lambda b: (b, 0))
    cached = pl.BlockSpec((1, past_len * N_HEADS, V_DIM), lambda b: (b, 0, 0))
    return pl.pallas_call(
        functools.partial(_sample_attn_kernel, past_len=past_len),
        grid=(batch,),
        in_specs=[new_rows, cached, cached, new_rows, new_rows,
                  _resident(bias_past.shape), _resident(bias_new.shape), *small, _resident(subln_g.shape)],
        out_specs=new_rows,
        out_shape=jax.ShapeDtypeStruct(q.shape, BF16),
        compiler_params=_params("parallel"),
        name="sample_attn",
    )(q, cache_k, cache_v, kb, vb, bias_past, bias_new, *lam_params, subln_g)


def _conv_kernel(pre_ref, halo_ref, u_ref, wb_ref, b_ref, g_ref, beta_ref, c_ref, ubuf, *, tt, rc):
    t = pl.program_id(1)

    @pl.when(t == 0)
    def _():
        ubuf[0:HALO, :] = pre_ref[0]

    @pl.when(t > 0)
    def _():
        ubuf[0:HALO, :] = halo_ref[0]

    ubuf[HALO:HALO + tt, :] = u_ref[0]

    def row_chunk(r, carry):
        r0 = pl.multiple_of(r * rc, rc)
        lane_groups = []
        for lg in range(WIDTH_B // LANES):
            ls = slice(lg * LANES, (lg + 1) * LANES)
            y = None
            for s in range(SUBLANES):
                taps = [(a, SUBLANES * a + s - (HALO - CONV_PAD)) for a in range(HALO // SUBLANES + 1)]
                taps = [(a, j) for a, j in taps if 0 <= j < CONV_WIDTH]
                rows = rc + (SUBLANES if s else 0)
                z = None
                for a, j in taps:
                    window = ubuf[pl.ds(r0 + SUBLANES * a, rows), ls].reshape(rows // SUBLANES, SUBLANES, LANES)
                    term = window * wb_ref[j, :, ls]
                    z = term if z is None else z + term
                z = z.reshape(rows, LANES)[s:s + rc]
                y = z if y is None else y + z
            lane_groups.append(y)
        y = jnp.concatenate(lane_groups, axis=1) + b_ref[...]
        mu = jnp.mean(y, axis=-1, keepdims=True)
        var = jnp.mean(jnp.square(y - mu), axis=-1, keepdims=True)
        yn = (y - mu) * lax.rsqrt(var + EPS) * g_ref[...] + beta_ref[...]
        c_ref[0, pl.ds(r0, rc), :] = (yn * _sigmoid(yn)).astype(c_ref.dtype)
        return carry

    lax.fori_loop(0, tt // rc, row_chunk, 0)


def _conv(prefix, u, w_rows, b_dw, ln_g, ln_b, tt, rc):
    batch, seq, _ = u.shape
    halo_blocks = tt // HALO
    small = [_resident(p.shape) for p in (w_rows, b_dw, ln_g, ln_b)]
    return pl.pallas_call(
        functools.partial(_conv_kernel, tt=tt, rc=rc),
        grid=(batch, seq // tt),
        in_specs=[pl.BlockSpec((1, HALO, WIDTH_B), lambda b, t: (b, 0, 0)),
                  pl.BlockSpec((1, HALO, WIDTH_B), lambda b, t: (b, jnp.maximum(t * halo_blocks - 1, 0), 0)),
                  pl.BlockSpec((1, tt, WIDTH_B), lambda b, t: (b, t, 0)), *small],
        out_specs=pl.BlockSpec((1, tt, WIDTH_B), lambda b, t: (b, t, 0)),
        out_shape=jax.ShapeDtypeStruct(u.shape, BF16),
        scratch_shapes=[pltpu.VMEM((HALO + tt, WIDTH_B), F32)],
        compiler_params=_params("parallel", "arbitrary"),
        name="conv_ln_swish",
    )(prefix, u, u, w_rows, b_dw, ln_g, ln_b)


def _out_ffn_kernel(x_ref, a_ref, c_ref, wo_ref, g2_ref, wu_ref, wd_ref, gf_ref, y_ref, *, ff_chunk):
    x = (x_ref[...]
         + jnp.dot(a_ref[...], wo_ref[:WIDTH_A, :], preferred_element_type=F32)
         + jnp.dot(c_ref[...], wo_ref[WIDTH_A:, :], preferred_element_type=F32))
    h2 = _rms_norm_rows(x, g2_ref[...]).astype(BF16)
    ffn = None
    for lo in range(0, D_FF, ff_chunk):
        hid = jnp.dot(h2, wu_ref[:, lo:lo + ff_chunk], preferred_element_type=F32)
        hid = jnp.square(jnp.maximum(hid, 0.0)).astype(BF16)
        down = jnp.dot(hid, wd_ref[lo:lo + ff_chunk, :], preferred_element_type=F32)
        ffn = down if ffn is None else ffn + down
    y_ref[...] = _rms_norm_rows(x + ffn, gf_ref[...])


def _out_ffn(x, a, c, w_out_bf, ln2_g, w_up_bf, w_down_bf, ln_f_g, tm, ff_chunk):
    n = x.shape[0]
    rows = lambda width: pl.BlockSpec((tm, width), lambda i: (i, 0))
    return pl.pallas_call(
        functools.partial(_out_ffn_kernel, ff_chunk=ff_chunk),
        grid=(n // tm,),
        in_specs=[rows(D_MODEL), rows(WIDTH_A), rows(WIDTH_B), _resident(w_out_bf.shape), _resident(ln2_g.shape),
                  _resident(w_up_bf.shape), _resident(w_down_bf.shape), _resident(ln_f_g.shape)],
        out_specs=rows(D_MODEL),
        out_shape=jax.ShapeDtypeStruct(x.shape, F32),
        compiler_params=_params("parallel"),
        name="out_ffn",
    )(x, a, c, w_out_bf, ln2_g, w_up_bf, w_down_bf, ln_f_g)


def _bucket_of(rel):
    half = N_BUCKETS // 2
    max_exact = half // 2
    n = -rel
    ret = jnp.where(n < 0, half, 0)
    n = jnp.abs(n)
    nf = jnp.maximum(n, 1).astype(F32)
    large = max_exact + (jnp.log(nf / max_exact) / math.log(MAX_DISTANCE / max_exact)
                         * (half - max_exact)).astype(jnp.int32)
    large = jnp.minimum(large, half - 1)
    return ret + jnp.where(n < max_exact, n, large)


def _bias_table(rel_bias, q_pos, k_pos, far_rel):
    shifted = (rel_bias - rel_bias[_bucket_of(far_rel)]).T
    bucket = _bucket_of(k_pos[None, :] - q_pos[:, None])
    hit = bucket[None, :, :, None] == jnp.arange(N_BUCKETS)
    table = jnp.sum(jnp.where(hit, shifted[:, None, None, :], 0.0), axis=-1)
    visible = (k_pos[None, :] // CHUNK) <= (q_pos[:, None] // CHUNK)
    return jnp.where(visible[None], table, NEG_INF).astype(F32)


def kernel(x_prompt, x_sample, cache_k, cache_v, state_conv, rel_bias, ln1_g, w_in, lambda_q1, lambda_k1, lambda_q2,
           lambda_k2, subln_g, w_dw, b_dw, conv_ln_g, conv_ln_b, w_out, ln2_g, w_up, w_down, ln_f_g):
    batch, seq, _ = x_prompt.shape
    dec_batch, dec_seq, _ = x_sample.shape
    past_len = cache_k.shape[2]
    tq = 256
    assert w_in.shape[0] == 1 and seq % (2 * tq) == 0 and seq >= CONV_PAD and dec_seq >= CONV_PAD

    w_in_bf, w_out_bf = w_in[0].astype(BF16), w_out[0].astype(BF16)
    w_up_bf, w_down_bf = w_up[0].astype(BF16), w_down[0].astype(BF16)
    w_rows = jnp.broadcast_to(w_dw[0][:, None, :], (CONV_WIDTH, SUBLANES, WIDTH_B))
    lam_params = (lambda_q1, lambda_k1, lambda_q2, lambda_k2)

    far_rel = jnp.int32(-(2 * tq))
    bias_tail = _bias_table(rel_bias, tq + jnp.arange(tq), jnp.arange(2 * tq), far_rel)
    q_pos_s = past_len + jnp.arange(dec_seq)
    bias_past = _bias_table(rel_bias, q_pos_s, jnp.arange(past_len), far_rel)
    bias_new = _bias_table(rel_bias, q_pos_s, q_pos_s, far_rel)

    xp = x_prompt.reshape(batch * seq, D_MODEL)
    q_p, k_p, v_p, kb_p, vb_p, u_p = _in_proj(xp, ln1_g, w_in_bf, tm=512)
    a_p = _prompt_attn(q_p, kb_p, vb_p, bias_tail, lam_params, subln_g, batch, seq, tq)
    u_p3 = u_p.reshape(batch, seq, WIDTH_B)
    c_p = _conv(jnp.zeros((batch, HALO, WIDTH_B), F32), u_p3, w_rows, b_dw, conv_ln_g, conv_ln_b, tt=512, rc=64)
    y_p = _out_ffn(xp, a_p, c_p.reshape(batch * seq, WIDTH_B), w_out_bf, ln2_g, w_up_bf, w_down_bf,
                   ln_f_g[None, :], tm=512, ff_chunk=1024)

    xs = x_sample.reshape(dec_batch * dec_seq, D_MODEL)
    q_s, k_s, v_s, kb_s, vb_s, u_s = _in_proj(xs, ln1_g, w_in_bf, tm=512)
    by_head = lambda cache: cache[0].reshape(dec_batch, past_len * N_HEADS, V_DIM)
    a_s = _sample_attn(q_s, by_head(cache_k), by_head(cache_v), kb_s, vb_s, bias_past, bias_new,
                       lam_params, subln_g, past_len)
    u_s3 = u_s.reshape(dec_batch, dec_seq, WIDTH_B)
    prefix_s = jnp.pad(state_conv[0], ((0, 0), (HALO - CONV_PAD, 0), (0, 0)))
    c_s = _conv(prefix_s, u_s3, w_rows, b_dw, conv_ln_g, conv_ln_b, tt=dec_seq, rc=dec_seq)
    y_s = _out_ffn(xs, a_s, c_s.reshape(dec_batch * dec_seq, WIDTH_B), w_out_bf, ln2_g, w_up_bf, w_down_bf,
                   ln_f_g[None, :], tm=512, ff_chunk=1024)

    heads = lambda t, b, s: t.reshape(1, b, s, N_HEADS, V_DIM)
    return (y_p.reshape(batch, seq, D_MODEL), y_s.reshape(dec_batch, dec_seq, D_MODEL),
            heads(k_p, batch, seq), heads(v_p, batch, seq), u_p3[None, :, seq - CONV_PAD:],
            heads(k_s, dec_batch, dec_seq), heads(v_s, dec_batch, dec_seq), u_s3[None, :, dec_seq - CONV_PAD:])
```

```python
import functools
import math

import jax
import jax.numpy as jnp
from jax import lax
from jax.experimental import pallas as pl
from jax.experimental.pallas import tpu as pltpu

D_MODEL = 1024
CHUNK = 64
N_HEADS = 4
HEAD_DIM = 64
V_DIM = 2 * HEAD_DIM
QK_WIDTH = N_HEADS * 2 * HEAD_DIM
WIDTH_A = N_HEADS * V_DIM
WIDTH_B = D_MODEL - WIDTH_A
CONV_WIDTH = 31
CONV_PAD = CONV_WIDTH - 1
D_FF = 4 * D_MODEL
N_BUCKETS = 32
MAX_DISTANCE = 128
EPS = 1e-6
NEG_INF = -1e30
LAM_INIT = 0.8 - 0.6 * math.exp(-0.3 * 0)

LANES = 128
SUBLANES = 8
HALO = 32
VMEM_LIMIT_BYTES = 56 * 1024 * 1024

F32 = jnp.float32
BF16 = jnp.bfloat16
_NT = (((1,), (1,)), ((), ()))


def _sigmoid(x):
    return 1.0 / (1.0 + jnp.exp(-x))


def _rms_norm_rows(x, g):
    return x * lax.rsqrt(jnp.mean(x * x, axis=-1, keepdims=True) + EPS) * g


def _params(*semantics):
    return pltpu.CompilerParams(dimension_semantics=semantics, vmem_limit_bytes=VMEM_LIMIT_BYTES)


def _resident(shape):
    zeros = (0,) * len(shape)
    return pl.BlockSpec(shape, lambda *_: zeros, pipeline_mode=pl.Buffered(1))


def _head_cols(h):
    return slice(h * V_DIM, (h + 1) * V_DIM)


def _head_rows(h, n):
    return pl.ds(h, n, stride=N_HEADS)


def _in_proj_kernel(x_ref, g_ref, w_ref, q_ref, k_ref, v_ref, kb_ref, vb_ref, u_ref):
    tm = x_ref.shape[0]
    h = _rms_norm_rows(x_ref[...], g_ref[...]).astype(BF16)

    def cols(lo, hi):
        return jnp.dot(h, w_ref[:, lo:hi], preferred_element_type=F32)

    a_cols = 2 * QK_WIDTH + WIDTH_A
    q_ref[...] = (cols(0, QK_WIDTH) * HEAD_DIM ** -0.5).astype(BF16)
    for full, out_ref, bf_ref in ((cols(QK_WIDTH, 2 * QK_WIDTH), k_ref, kb_ref),
                                  (cols(2 * QK_WIDTH, a_cols), v_ref, vb_ref)):
        bf_ref[...] = full.astype(BF16)
        for hd in range(N_HEADS):
            out_ref[_head_rows(hd, tm), :] = full[:, _head_cols(hd)]
    u_ref[...] = cols(a_cols, a_cols + WIDTH_B) * _sigmoid(cols(a_cols + WIDTH_B, a_cols + 2 * WIDTH_B))


def _in_proj(x, ln_g, w_in_bf, tm):
    n = x.shape[0]
    rows = lambda width: pl.BlockSpec((tm, width), lambda i: (i, 0))
    by_head = pl.BlockSpec((tm * N_HEADS, V_DIM), lambda i: (i, 0))
    wide = lambda dt: jax.ShapeDtypeStruct((n, QK_WIDTH), dt)
    tall = jax.ShapeDtypeStruct((n * N_HEADS, V_DIM), F32)
    return pl.pallas_call(
        _in_proj_kernel,
        grid=(n // tm,),
        in_specs=[rows(D_MODEL), _resident((1, D_MODEL)), _resident(w_in_bf.shape)],
        out_specs=[rows(QK_WIDTH), by_head, by_head, rows(QK_WIDTH), rows(WIDTH_A), rows(WIDTH_B)],
        out_shape=[wide(BF16), tall, tall, wide(BF16), wide(BF16), wide(F32)],
        compiler_params=_params("parallel"),
        name="in_proj",
    )(x, ln_g, w_in_bf)


def _lambda(lq1_ref, lk1_ref, lq2_ref, lk2_ref):
    d1 = jnp.sum(lq1_ref[...] * lk1_ref[...], axis=-1, keepdims=True)
    d2 = jnp.sum(lq2_ref[...] * lk2_ref[...], axis=-1, keepdims=True)
    return jnp.exp(d1) - jnp.exp(d2) + LAM_INIT


def _stack_maps(qh):
    lane = lax.broadcasted_iota(jnp.int32, qh.shape, 1)
    zero = jnp.zeros_like(qh)
    return jnp.concatenate([jnp.where(lane < HEAD_DIM, qh, zero), jnp.where(lane >= HEAD_DIM, qh, zero)], axis=0)


def _with_ones(v):
    return jnp.concatenate([v, jnp.ones_like(v)], axis=1)


def _finish_head(acc, lam, g, tq):
    o1 = acc[:tq, :V_DIM] / acc[:tq, V_DIM:]
    o2 = acc[tq:, :V_DIM] / acc[tq:, V_DIM:]
    o = o1 - lam * o2
    return _rms_norm_rows(o, g) * (1.0 - LAM_INIT)


def _prompt_attn_kernel(q_ref, k_ref, v_ref, bt_ref, lq1_ref, lk1_ref, lq2_ref, lk2_ref, g_ref,
                        o_ref, q2_sc, m_sc, acc_sc, *, tq):
    i = pl.program_id(1)
    for h in range(N_HEADS):
        q2_sc[h] = _stack_maps(q_ref[:, _head_cols(h)])
    m_sc[...] = jnp.full(m_sc.shape, NEG_INF, F32)
    acc_sc[...] = jnp.zeros(acc_sc.shape, F32)

    def key_tile(start, tk, bias_of_head):
        for h in range(N_HEADS):
            hs = _head_cols(h)
            s = lax.dot_general(q2_sc[h], k_ref[pl.ds(start, tk), hs], _NT, preferred_element_type=F32)
            if bias_of_head is not None:
                bias = bias_of_head(h)
                s = s + jnp.concatenate([bias, bias], axis=0)
            m_prev = m_sc[h]
            m_new = jnp.maximum(m_prev, jnp.max(s, axis=1, keepdims=True))
            alpha = jnp.exp(m_prev - m_new)
            p = jnp.exp(s - jnp.concatenate([m_new] * (tk // LANES), axis=1))
            pv = jnp.dot(p.astype(BF16), _with_ones(v_ref[pl.ds(start, tk), hs]), preferred_element_type=F32)
            acc_sc[h] = jnp.concatenate([alpha, alpha], axis=1) * acc_sc[h] + pv
            m_sc[h] = m_new

    n_far = jnp.maximum(i - 1, 0)

    def far_pair(j, carry):
        key_tile(pl.multiple_of(j * 2 * tq, 2 * tq), 2 * tq, None)
        return carry

    lax.fori_loop(0, n_far // 2, far_pair, 0)

    @pl.when(n_far % 2 == 1)
    def _():
        key_tile(pl.multiple_of((n_far - 1) * tq, tq), tq, None)

    @pl.when(i >= 1)
    def _():
        key_tile(pl.multiple_of((i - 1) * tq, tq), 2 * tq, lambda h: bt_ref[h])

    @pl.when(i == 0)
    def _():
        key_tile(0, tq, lambda h: bt_ref[h, :, tq:])

    lam = _lambda(lq1_ref, lk1_ref, lq2_ref, lk2_ref)
    for h in range(N_HEADS):
        o_ref[:, _head_cols(h)] = _finish_head(acc_sc[h], lam, g_ref[...], tq).astype(o_ref.dtype)


def _prompt_attn(q, kb, vb, bias_tail, lam_params, subln_g, batch, seq, tq):
    nq = seq // tq
    small = [_resident(p.shape) for p in lam_params]
    return pl.pallas_call(
        functools.partial(_prompt_attn_kernel, tq=tq),
        grid=(batch, nq),
        in_specs=[pl.BlockSpec((tq, QK_WIDTH), lambda b, i: (b * nq + i, 0)),
                  pl.BlockSpec((seq, QK_WIDTH), lambda b, i: (b, 0)),
                  pl.BlockSpec((seq, WIDTH_A), lambda b, i: (b, 0)),
                  _resident(bias_tail.shape), *small, _resident(subln_g.shape)],
        out_specs=pl.BlockSpec((tq, WIDTH_A), lambda b, i: (b * nq + i, 0)),
        out_shape=jax.ShapeDtypeStruct((batch * seq, WIDTH_A), BF16),
        scratch_shapes=[pltpu.VMEM((N_HEADS, 2 * tq, V_DIM), BF16), pltpu.VMEM((N_HEADS, 2 * tq, LANES), F32),
                        pltpu.VMEM((N_HEADS, 2 * tq, 2 * V_DIM), F32)],
        compiler_params=_params("parallel", "arbitrary"),
        name="prompt_attn",
    )(q, kb, vb, bias_tail, *lam_params, subln_g)


def _sample_attn_kernel(q_ref, ck_ref, cv_ref, kn_ref, vn_ref, bp_ref, bn_ref, lq1_ref, lk1_ref, lq2_ref, lk2_ref,
                        g_ref, o_ref, *, past_len):
    t = q_ref.shape[0]
    lam = _lambda(lq1_ref, lk1_ref, lq2_ref, lk2_ref)
    for h in range(N_HEADS):
        hs = _head_cols(h)
        q2 = _stack_maps(q_ref[:, hs])
        bias_past, bias_new = bp_ref[h], bn_ref[h]
        k_past = ck_ref[0, _head_rows(h, past_len), :].astype(BF16)
        s_past = lax.dot_general(q2, k_past, _NT, preferred_element_type=F32)
        s_past = s_past + jnp.concatenate([bias_past, bias_past], axis=0)
        s_new = lax.dot_general(q2, kn_ref[:, hs], _NT, preferred_element_type=F32)
        s_new = s_new + jnp.concatenate([bias_new, bias_new], axis=0)
        m = jnp.maximum(jnp.max(s_past, axis=1, keepdims=True), jnp.max(s_new, axis=1, keepdims=True))
        p_past = jnp.exp(s_past - m).astype(BF16)
        p_new = jnp.exp(s_new - m).astype(BF16)
        v_past = cv_ref[0, _head_rows(h, past_len), :].astype(BF16)
        acc = (jnp.dot(p_past, _with_ones(v_past), preferred_element_type=F32)
               + jnp.dot(p_new, _with_ones(vn_ref[:, hs]), preferred_element_type=F32))
        o_ref[:, hs] = _finish_head(acc, lam, g_ref[...], t).astype(o_ref.dtype)


def _sample_attn(q, cache_k, cache_v, kb, vb, bias_past, bias_new, lam_params, subln_g, past_len):
    batch = cache_k.shape[0]
    t = q.shape[0] // batch
    small = [_resident(p.shape) for p in lam_params]
    new_rows = pl.BlockSpec((t, QK_WIDTH), lambda b: (b, 0))
    cached = pl.BlockSpec((1, past_len * N_HEADS, V_DIM), lambda b: (b, 0, 0))
    return pl.pallas_call(
        functools.partial(_sample_attn_kernel, past_len=past_len),
        grid=(batch,),
        in_specs=[new_rows, cached, cached, new_rows, new_rows,
                  _resident(bias_past.shape), _resident(bias_new.shape), *small, _resident(subln_g.shape)],
        out_specs=new_rows,
        out_shape=jax.ShapeDtypeStruct(q.shape, BF16),
        compiler_params=_params("parallel"),
        name="sample_attn",
    )(q, cache_k, cache_v, kb, vb, bias_past, bias_new, *lam_params, subln_g)


def _conv_kernel(pre_ref, halo_ref, u_ref, wb_ref, b_ref, g_ref, beta_ref, c_ref, ubuf, ybuf, *, tt, rc):
    n_groups = WIDTH_B // LANES
    n_chunks = tt // rc
    history = jnp.where(pl.program_id(1) == 0, pre_ref[0], halo_ref[0])
    for k in range(n_chunks):
        for lg in range(n_groups):
            ls = slice(lg * LANES, (lg + 1) * LANES)
            slab = k * n_groups + lg
            ubuf[slab, 0:HALO, :] = history[:, ls] if k == 0 else u_ref[0, k * rc - HALO:k * rc, ls]
            ubuf[slab, HALO:, :] = u_ref[0, k * rc:(k + 1) * rc, ls]

    def conv_slab(slab, carry):
        lg = slab % n_groups
        acc = None
        for j in range(CONV_WIDTH):
            first = HALO - CONV_PAD + j
            window = ubuf[slab, first:first + rc, :].reshape(rc // SUBLANES, SUBLANES, LANES)
            term = window * wb_ref[lg, j]
            acc = term if acc is None else acc + term
        ybuf[slab] = acc.reshape(rc, LANES)
        return carry

    lax.fori_loop(0, n_chunks * n_groups, conv_slab, 0)

    def norm_chunk(k, carry):
        y = jnp.concatenate([ybuf[k * n_groups + lg] for lg in range(n_groups)], axis=1) + b_ref[...]
        mu = jnp.mean(y, axis=-1, keepdims=True)
        var = jnp.mean(jnp.square(y - mu), axis=-1, keepdims=True)
        yn = (y - mu) * lax.rsqrt(var + EPS) * g_ref[...] + beta_ref[...]
        c_ref[0, pl.ds(pl.multiple_of(k * rc, rc), rc), :] = (yn * _sigmoid(yn)).astype(c_ref.dtype)
        return carry

    lax.fori_loop(0, n_chunks, norm_chunk, 0, unroll=True)


def _conv(prefix, u, w_rows, b_dw, ln_g, ln_b, tt, rc):
    batch, seq, _ = u.shape
    halo_blocks = tt // HALO
    small = [_resident(p.shape) for p in (w_rows, b_dw, ln_g, ln_b)]
    return pl.pallas_call(
        functools.partial(_conv_kernel, tt=tt, rc=rc),
        grid=(batch, seq // tt),
        in_specs=[pl.BlockSpec((1, HALO, WIDTH_B), lambda b, t: (b, 0, 0)),
                  pl.BlockSpec((1, HALO, WIDTH_B), lambda b, t: (b, jnp.maximum(t * halo_blocks - 1, 0), 0)),
                  pl.BlockSpec((1, tt, WIDTH_B), lambda b, t: (b, t, 0)), *small],
        out_specs=pl.BlockSpec((1, tt, WIDTH_B), lambda b, t: (b, t, 0)),
        out_shape=jax.ShapeDtypeStruct(u.shape, BF16),
        scratch_shapes=[pltpu.VMEM((tt // rc * (WIDTH_B // LANES), HALO + rc, LANES), F32),
                        pltpu.VMEM((tt // rc * (WIDTH_B // LANES), rc, LANES), F32)],
        compiler_params=_params("parallel", "parallel"),
        name="conv_ln_swish",
    )(prefix, u, u, w_rows, b_dw, ln_g, ln_b)


def _out_ffn_kernel(x_ref, a_ref, c_ref, wo_ref, g2_ref, wu_ref, wd_ref, gf_ref, y_ref, *, ff_chunk):
    x = (x_ref[...]
         + jnp.dot(a_ref[...], wo_ref[:WIDTH_A, :], preferred_element_type=F32)
         + jnp.dot(c_ref[...], wo_ref[WIDTH_A:, :], preferred_element_type=F32))
    h2 = _rms_norm_rows(x, g2_ref[...]).astype(BF16)
    ffn = None
    for lo in range(0, D_FF, ff_chunk):
        hid = jnp.dot(h2, wu_ref[:, lo:lo + ff_chunk], preferred_element_type=F32)
        hid = jnp.square(jnp.maximum(hid, 0.0)).astype(BF16)
        down = jnp.dot(hid, wd_ref[lo:lo + ff_chunk, :], preferred_element_type=F32)
        ffn = down if ffn is None else ffn + down
    y_ref[...] = _rms_norm_rows(x + ffn, gf_ref[...])


def _out_ffn(x, a, c, w_out_bf, ln2_g, w_up_bf, w_down_bf, ln_f_g, tm, ff_chunk):
    n = x.shape[0]
    rows = lambda width: pl.BlockSpec((tm, width), lambda i: (i, 0))
    return pl.pallas_call(
        functools.partial(_out_ffn_kernel, ff_chunk=ff_chunk),
        grid=(n // tm,),
        in_specs=[rows(D_MODEL), rows(WIDTH_A), rows(WIDTH_B), _resident(w_out_bf.shape), _resident(ln2_g.shape),
                  _resident(w_up_bf.shape), _resident(w_down_bf.shape), _resident(ln_f_g.shape)],
        out_specs=rows(D_MODEL),
        out_shape=jax.ShapeDtypeStruct(x.shape, F32),
        compiler_params=_params("parallel"),
        name="out_ffn",
    )(x, a, c, w_out_bf, ln2_g, w_up_bf, w_down_bf, ln_f_g)


def _bucket_of(rel):
    half = N_BUCKETS // 2
    max_exact = half // 2
    n = -rel
    ret = jnp.where(n < 0, half, 0)
    n = jnp.abs(n)
    nf = jnp.maximum(n, 1).astype(F32)
    large = max_exact + (jnp.log(nf / max_exact) / math.log(MAX_DISTANCE / max_exact)
                         * (half - max_exact)).astype(jnp.int32)
    large = jnp.minimum(large, half - 1)
    return ret + jnp.where(n < max_exact, n, large)


def _bias_table(rel_bias, q_pos, k_pos, far_rel):
    shifted = (rel_bias - rel_bias[_bucket_of(far_rel)]).T
    bucket = _bucket_of(k_pos[None, :] - q_pos[:, None])
    hit = bucket[None, :, :, None] == jnp.arange(N_BUCKETS)
    table = jnp.sum(jnp.where(hit, shifted[:, None, None, :], 0.0), axis=-1)
    visible = (k_pos[None, :] // CHUNK) <= (q_pos[:, None] // CHUNK)
    return jnp.where(visible[None], table, NEG_INF).astype(F32)


def kernel(x_prompt, x_sample, cache_k, cache_v, state_conv, rel_bias, ln1_g, w_in, lambda_q1, lambda_k1, lambda_q2,
           lambda_k2, subln_g, w_dw, b_dw, conv_ln_g, conv_ln_b, w_out, ln2_g, w_up, w_down, ln_f_g):
    batch, seq, _ = x_prompt.shape
    dec_batch, dec_seq, _ = x_sample.shape
    past_len = cache_k.shape[2]
    tq = 256
    assert w_in.shape[0] == 1 and seq % (2 * tq) == 0 and seq >= CONV_PAD and dec_seq >= CONV_PAD

    w_in_bf, w_out_bf = w_in[0].astype(BF16), w_out[0].astype(BF16)
    w_up_bf, w_down_bf = w_up[0].astype(BF16), w_down[0].astype(BF16)
    w_taps = w_dw[0].reshape(CONV_WIDTH, WIDTH_B // LANES, 1, LANES).transpose(1, 0, 2, 3)
    w_rows = jnp.broadcast_to(w_taps, (WIDTH_B // LANES, CONV_WIDTH, SUBLANES, LANES))
    lam_params = (lambda_q1, lambda_k1, lambda_q2, lambda_k2)

    far_rel = jnp.int32(-(2 * tq))
    bias_tail = _bias_table(rel_bias, tq + jnp.arange(tq), jnp.arange(2 * tq), far_rel)
    q_pos_s = past_len + jnp.arange(dec_seq)
    bias_past = _bias_table(rel_bias, q_pos_s, jnp.arange(past_len), far_rel)
    bias_new = _bias_table(rel_bias, q_pos_s, q_pos_s, far_rel)

    xp = x_prompt.reshape(batch * seq, D_MODEL)
    q_p, k_p, v_p, kb_p, vb_p, u_p = _in_proj(xp, ln1_g, w_in_bf, tm=512)
    a_p = _prompt_attn(q_p, kb_p, vb_p, bias_tail, lam_params, subln_g, batch, seq, tq)
    u_p3 = u_p.reshape(batch, seq, WIDTH_B)
    c_p = _conv(jnp.zeros((batch, HALO, WIDTH_B), F32), u_p3, w_rows, b_dw, conv_ln_g, conv_ln_b, tt=512, rc=64)
    y_p = _out_ffn(xp, a_p, c_p.reshape(batch * seq, WIDTH_B), w_out_bf, ln2_g, w_up_bf, w_down_bf,
                   ln_f_g[None, :], tm=512, ff_chunk=1024)

    xs = x_sample.reshape(dec_batch * dec_seq, D_MODEL)
    q_s, k_s, v_s, kb_s, vb_s, u_s = _in_proj(xs, ln1_g, w_in_bf, tm=512)
    by_head = lambda cache: cache[0].reshape(dec_batch, past_len * N_HEADS, V_DIM)
    a_s = _sample_attn(q_s, by_head(cache_k), by_head(cache_v), kb_s, vb_s, bias_past, bias_new,
                       lam_params, subln_g, past_len)
    u_s3 = u_s.reshape(dec_batch, dec_seq, WIDTH_B)
    prefix_s = jnp.pad(state_conv[0], ((0, 0), (HALO - CONV_PAD, 0), (0, 0)))
    c_s = _conv(prefix_s, u_s3, w_rows, b_dw, conv_ln_g, conv_ln_b, tt=dec_seq, rc=dec_seq)
    y_s = _out_ffn(xs, a_s, c_s.reshape(dec_batch * dec_seq, WIDTH_B), w_out_bf, ln2_g, w_up_bf, w_down_bf,
                   ln_f_g[None, :], tm=512, ff_chunk=1024)

    heads = lambda t, b, s: t.reshape(1, b, s, N_HEADS, V_DIM)
    return (y_p.reshape(batch, seq, D_MODEL), y_s.reshape(dec_batch, dec_seq, D_MODEL),
            heads(k_p, batch, seq), heads(v_p, batch, seq), u_p3[None, :, seq - CONV_PAD:],
            heads(k_s, dec_batch, dec_seq), heads(v_s, dec_batch, dec_seq), u_s3[None, :, dec_seq - CONV_PAD:])
```

```python
import functools
import math

import jax
import jax.numpy as jnp
from jax import lax
from jax.experimental import pallas as pl
from jax.experimental.pallas import tpu as pltpu

D_MODEL = 1024
CHUNK = 64
N_HEADS = 4
HEAD_DIM = 64
V_DIM = 2 * HEAD_DIM
QK_WIDTH = N_HEADS * 2 * HEAD_DIM
WIDTH_A = N_HEADS * V_DIM
WIDTH_B = D_MODEL - WIDTH_A
CONV_WIDTH = 31
CONV_PAD = CONV_WIDTH - 1
D_FF = 4 * D_MODEL
N_BUCKETS = 32
MAX_DISTANCE = 128
EPS = 1e-6
NEG_INF = -1e30
LAM_INIT = 0.8 - 0.6 * math.exp(-0.3 * 0)

LANES = 128
SUBLANES = 8
HALO = 32
KV_TILE = 256
ONES_ROWS = 16
VMEM_LIMIT_BYTES = 56 * 1024 * 1024

F32 = jnp.float32
BF16 = jnp.bfloat16
_NT = (((1,), (1,)), ((), ()))


def _sigmoid(x):
    return 1.0 / (1.0 + jnp.exp(-x))


def _rms_norm_rows(x, g):
    return x * lax.rsqrt(jnp.mean(x * x, axis=-1, keepdims=True) + EPS) * g


def _params(*semantics):
    return pltpu.CompilerParams(dimension_semantics=semantics, vmem_limit_bytes=VMEM_LIMIT_BYTES)


def _resident(shape):
    zeros = (0,) * len(shape)
    return pl.BlockSpec(shape, lambda *_: zeros, pipeline_mode=pl.Buffered(1))


def _head_cols(h):
    return slice(h * V_DIM, (h + 1) * V_DIM)


def _head_rows(h, n):
    return pl.ds(h, n, stride=N_HEADS)


def _in_proj_kernel(x_ref, g_ref, w_ref, q_ref, k_ref, v_ref, kb_ref, vb_ref, vt_ref, u_ref):
    tm = x_ref.shape[0]
    h = _rms_norm_rows(x_ref[...], g_ref[...]).astype(BF16)

    def cols(lo, hi):
        return jnp.dot(h, w_ref[:, lo:hi], preferred_element_type=F32)

    a_cols = 2 * QK_WIDTH + WIDTH_A
    q_ref[...] = (cols(0, QK_WIDTH) * HEAD_DIM ** -0.5).astype(BF16)
    k, v = cols(QK_WIDTH, 2 * QK_WIDTH), cols(2 * QK_WIDTH, a_cols)
    for full, out_ref, bf_ref in ((k, k_ref, kb_ref), (v, v_ref, vb_ref)):
        bf_ref[...] = full.astype(BF16)
        for hd in range(N_HEADS):
            out_ref[_head_rows(hd, tm), :] = full[:, _head_cols(hd)]
    v_t = v.T
    for c in range(tm // KV_TILE):
        vt_ref[c] = v_t[:, c * KV_TILE:(c + 1) * KV_TILE].astype(BF16)
    u_ref[...] = cols(a_cols, a_cols + WIDTH_B) * _sigmoid(cols(a_cols + WIDTH_B, a_cols + 2 * WIDTH_B))


def _in_proj(x, ln_g, w_in_bf, tm):
    n = x.shape[0]
    rows = lambda width: pl.BlockSpec((tm, width), lambda i: (i, 0))
    by_head = pl.BlockSpec((tm * N_HEADS, V_DIM), lambda i: (i, 0))
    wide = lambda dt: jax.ShapeDtypeStruct((n, QK_WIDTH), dt)
    tall = jax.ShapeDtypeStruct((n * N_HEADS, V_DIM), F32)
    return pl.pallas_call(
        _in_proj_kernel,
        grid=(n // tm,),
        in_specs=[rows(D_MODEL), _resident((1, D_MODEL)), _resident(w_in_bf.shape)],
        out_specs=[rows(QK_WIDTH), by_head, by_head, rows(QK_WIDTH), rows(WIDTH_A),
                   pl.BlockSpec((tm // KV_TILE, WIDTH_A, KV_TILE), lambda i: (i, 0, 0)), rows(WIDTH_B)],
        out_shape=[wide(BF16), tall, tall, wide(BF16), wide(BF16),
                   jax.ShapeDtypeStruct((n // KV_TILE, WIDTH_A, KV_TILE), BF16), wide(F32)],
        compiler_params=_params("parallel"),
        name="in_proj",
    )(x, ln_g, w_in_bf)


def _lambda(lq1_ref, lk1_ref, lq2_ref, lk2_ref):
    d1 = jnp.sum(lq1_ref[...] * lk1_ref[...], axis=-1, keepdims=True)
    d2 = jnp.sum(lq2_ref[...] * lk2_ref[...], axis=-1, keepdims=True)
    return jnp.exp(d1) - jnp.exp(d2) + LAM_INIT


def _stack_maps(qh):
    lane = lax.broadcasted_iota(jnp.int32, qh.shape, 1)
    zero = jnp.zeros_like(qh)
    return jnp.concatenate([jnp.where(lane < HEAD_DIM, qh, zero), jnp.where(lane >= HEAD_DIM, qh, zero)], axis=0)


def _with_ones(v):
    return jnp.concatenate([v, jnp.ones_like(v)], axis=1)


def _finish_head(acc, lam, g, tq):
    o1 = acc[:tq, :V_DIM] / acc[:tq, V_DIM:]
    o2 = acc[tq:, :V_DIM] / acc[tq:, V_DIM:]
    o = o1 - lam * o2
    return _rms_norm_rows(o, g) * (1.0 - LAM_INIT)


def _prompt_attn_kernel(q_ref, k_ref, vt_ref, bt_ref, lq1_ref, lk1_ref, lq2_ref, lk2_ref, g_ref,
                        o_ref, q2_sc, m_sc, acc_sc, *, tq):
    i = pl.program_id(1)
    for h in range(N_HEADS):
        q2_sc[h] = _stack_maps(q_ref[:, _head_cols(h)])
    m_sc[...] = jnp.full(m_sc.shape, NEG_INF, F32)
    acc_sc[...] = jnp.zeros(acc_sc.shape, F32)

    def key_tile(start, tk, bias_of_head):
        first_tile = start // KV_TILE

        def scores(h):
            return lax.dot_general(k_ref[pl.ds(start, tk), _head_cols(h)], q2_sc[h], _NT,
                                   preferred_element_type=F32)

        s_next = scores(0)
        for h in range(N_HEADS):
            hs = _head_cols(h)
            s, s_next = s_next, (scores(h + 1) if h + 1 < N_HEADS else None)
            if bias_of_head is not None:
                bias = bias_of_head(h)
                s = s + jnp.concatenate([bias, bias], axis=1)
            m_prev = m_sc[h]
            m_new = jnp.maximum(m_prev, jnp.max(s, axis=0, keepdims=True))
            alpha = jnp.exp(m_prev - m_new)
            p = jnp.exp(s - m_new).astype(BF16)
            v_t = jnp.concatenate([vt_ref[first_tile + c, hs, :] for c in range(tk // KV_TILE)], axis=1)
            v_t = jnp.concatenate([v_t, jnp.ones((ONES_ROWS, tk), BF16)], axis=0)
            acc_sc[h] = alpha * acc_sc[h] + jnp.dot(v_t, p, preferred_element_type=F32)
            m_sc[h] = m_new

    n_far = jnp.maximum(i - 1, 0)

    def far_pair(j, carry):
        key_tile(pl.multiple_of(j * 2 * tq, 2 * tq), 2 * tq, None)
        return carry

    lax.fori_loop(0, n_far // 2, far_pair, 0)

    @pl.when(n_far % 2 == 1)
    def _():
        key_tile(pl.multiple_of((n_far - 1) * tq, tq), tq, None)

    @pl.when(i >= 1)
    def _():
        key_tile(pl.multiple_of((i - 1) * tq, tq), 2 * tq, lambda h: bt_ref[h])

    @pl.when(i == 0)
    def _():
        key_tile(0, tq, lambda h: bt_ref[h, tq:, :])

    lam = _lambda(lq1_ref, lk1_ref, lq2_ref, lk2_ref)
    for h in range(N_HEADS):
        acc = acc_sc[h]
        o = acc[:V_DIM] / acc[V_DIM:V_DIM + 1]
        o = o[:, :tq] - lam * o[:, tq:]
        o = o * lax.rsqrt(jnp.mean(o * o, axis=0, keepdims=True) + EPS) * g_ref[...] * (1.0 - LAM_INIT)
        o_ref[:, _head_cols(h)] = o.T.astype(o_ref.dtype)


def _prompt_attn(q, kb, v_t, bias_tail, lam_params, subln_g_col, batch, seq, tq):
    nq = seq // tq
    small = [_resident(p.shape) for p in lam_params]
    return pl.pallas_call(
        functools.partial(_prompt_attn_kernel, tq=tq),
        grid=(batch, nq),
        in_specs=[pl.BlockSpec((tq, QK_WIDTH), lambda b, i: (b * nq + i, 0)),
                  pl.BlockSpec((seq, QK_WIDTH), lambda b, i: (b, 0)),
                  pl.BlockSpec((seq // KV_TILE, WIDTH_A, KV_TILE), lambda b, i: (b, 0, 0)),
                  _resident(bias_tail.shape), *small, _resident(subln_g_col.shape)],
        out_specs=pl.BlockSpec((tq, WIDTH_A), lambda b, i: (b * nq + i, 0)),
        out_shape=jax.ShapeDtypeStruct((batch * seq, WIDTH_A), BF16),
        scratch_shapes=[pltpu.VMEM((N_HEADS, 2 * tq, V_DIM), BF16), pltpu.VMEM((N_HEADS, 1, 2 * tq), F32),
                        pltpu.VMEM((N_HEADS, V_DIM + ONES_ROWS, 2 * tq), F32)],
        compiler_params=_params("parallel", "arbitrary"),
        name="prompt_attn",
    )(q, kb, v_t, bias_tail, *lam_params, subln_g_col)


def _sample_attn_kernel(q_ref, ck_ref, cv_ref, kn_ref, vn_ref, bp_ref, bn_ref, lq1_ref, lk1_ref, lq2_ref, lk2_ref,
                        g_ref, o_ref, *, past_len):
    t = q_ref.shape[0]
    lam = _lambda(lq1_ref, lk1_ref, lq2_ref, lk2_ref)
    for h in range(N_HEADS):
        hs = _head_cols(h)
        q2 = _stack_maps(q_ref[:, hs])
        bias_past, bias_new = bp_ref[h], bn_ref[h]
        k_past = ck_ref[0, _head_rows(h, past_len), :].astype(BF16)
        s_past = lax.dot_general(q2, k_past, _NT, preferred_element_type=F32)
        s_past = s_past + jnp.concatenate([bias_past, bias_past], axis=0)
        s_new = lax.dot_general(q2, kn_ref[:, hs], _NT, preferred_element_type=F32)
        s_new = s_new + jnp.concatenate([bias_new, bias_new], axis=0)
        m = jnp.maximum(jnp.max(s_past, axis=1, keepdims=True), jnp.max(s_new, axis=1, keepdims=True))
        p_past = jnp.exp(s_past - m).astype(BF16)
        p_new = jnp.exp(s_new - m).astype(BF16)
        v_past = cv_ref[0, _head_rows(h, past_len), :].astype(BF16)
        acc = (jnp.dot(p_past, _with_ones(v_past), preferred_element_type=F32)
               + jnp.dot(p_new, _with_ones(vn_ref[:, hs]), preferred_element_type=F32))
        o_ref[:, hs] = _finish_head(acc, lam, g_ref[...], t).astype(o_ref.dtype)


def _sample_attn(q, cache_k, cache_v, kb, vb, bias_past, bias_new, lam_params, subln_g, past_len):
    batch = cache_k.shape[0]
    t = q.shape[0] // batch
    small = [_resident(p.shape) for p in lam_params]
    new_rows = pl.BlockSpec((t, QK_WIDTH), lambda b: (b, 0))
    cached = pl.BlockSpec((1, past_len * N_HEADS, V_DIM), lambda b: (b, 0, 0))
    return pl.pallas_call(
        functools.partial(_sample_attn_kernel, past_len=past_len),
        grid=(batch,),
        in_specs=[new_rows, cached, cached, new_rows, new_rows,
                  _resident(bias_past.shape), _resident(bias_new.shape), *small, _resident(subln_g.shape)],
        out_specs=new_rows,
        out_shape=jax.ShapeDtypeStruct(q.shape, BF16),
        compiler_params=_params("parallel"),
        name="sample_attn",
    )(q, cache_k, cache_v, kb, vb, bias_past, bias_new, *lam_params, subln_g)


def _conv_kernel(pre_ref, halo_ref, u_ref, wb_ref, b_ref, g_ref, beta_ref, c_ref, ubuf, ybuf, *, tt, rc):
    n_groups = WIDTH_B // LANES
    n_chunks = tt // rc
    history = jnp.where(pl.program_id(1) == 0, pre_ref[0], halo_ref[0])
    for k in range(n_chunks):
        for lg in range(n_groups):
            ls = slice(lg * LANES, (lg + 1) * LANES)
            slab = k * n_groups + lg
            ubuf[slab, 0:HALO, :] = history[:, ls] if k == 0 else u_ref[0, k * rc - HALO:k * rc, ls]
            ubuf[slab, HALO:, :] = u_ref[0, k * rc:(k + 1) * rc, ls]

    def conv_slab(slab, carry):
        lg = slab % n_groups
        acc = None
        for j in range(CONV_WIDTH):
            first = HALO - CONV_PAD + j
            window = ubuf[slab, first:first + rc, :].reshape(rc // SUBLANES, SUBLANES, LANES)
            term = window * wb_ref[lg, j]
            acc = term if acc is None else acc + term
        ybuf[slab] = acc.reshape(rc, LANES)
        return carry

    lax.fori_loop(0, n_chunks * n_groups, conv_slab, 0)

    def norm_chunk(k, carry):
        y = jnp.concatenate([ybuf[k * n_groups + lg] for lg in range(n_groups)], axis=1) + b_ref[...]
        mu = jnp.mean(y, axis=-1, keepdims=True)
        var = jnp.mean(jnp.square(y - mu), axis=-1, keepdims=True)
        yn = (y - mu) * lax.rsqrt(var + EPS) * g_ref[...] + beta_ref[...]
        c_ref[0, pl.ds(pl.multiple_of(k * rc, rc), rc), :] = (yn * _sigmoid(yn)).astype(c_ref.dtype)
        return carry

    lax.fori_loop(0, n_chunks, norm_chunk, 0, unroll=True)


def _conv(prefix, u, w_rows, b_dw, ln_g, ln_b, tt, rc):
    batch, seq, _ = u.shape
    halo_blocks = tt // HALO
    small = [_resident(p.shape) for p in (w_rows, b_dw, ln_g, ln_b)]
    return pl.pallas_call(
        functools.partial(_conv_kernel, tt=tt, rc=rc),
        grid=(batch, seq // tt),
        in_specs=[pl.BlockSpec((1, HALO, WIDTH_B), lambda b, t: (b, 0, 0)),
                  pl.BlockSpec((1, HALO, WIDTH_B), lambda b, t: (b, jnp.maximum(t * halo_blocks - 1, 0), 0)),
                  pl.BlockSpec((1, tt, WIDTH_B), lambda b, t: (b, t, 0)), *small],
        out_specs=pl.BlockSpec((1, tt, WIDTH_B), lambda b, t: (b, t, 0)),
        out_shape=jax.ShapeDtypeStruct(u.shape, BF16),
        scratch_shapes=[pltpu.VMEM((tt // rc * (WIDTH_B // LANES), HALO + rc, LANES), F32),
                        pltpu.VMEM((tt // rc * (WIDTH_B // LANES), rc, LANES), F32)],
        compiler_params=_params("parallel", "parallel"),
        name="conv_ln_swish",
    )(prefix, u, u, w_rows, b_dw, ln_g, ln_b)


def _out_ffn_kernel(x_ref, a_ref, c_ref, wo_ref, g2_ref, wu_ref, wd_ref, gf_ref, y_ref, *, ff_chunk):
    x = (x_ref[...]
         + jnp.dot(a_ref[...], wo_ref[:WIDTH_A, :], preferred_element_type=F32)
         + jnp.dot(c_ref[...], wo_ref[WIDTH_A:, :], preferred_element_type=F32))
    h2 = _rms_norm_rows(x, g2_ref[...]).astype(BF16)
    ffn = None
    for lo in range(0, D_FF, ff_chunk):
        hid = jnp.dot(h2, wu_ref[:, lo:lo + ff_chunk], preferred_element_type=F32)
        hid = jnp.square(jnp.maximum(hid, 0.0)).astype(BF16)
        down = jnp.dot(hid, wd_ref[lo:lo + ff_chunk, :], preferred_element_type=F32)
        ffn = down if ffn is None else ffn + down
    y_ref[...] = _rms_norm_rows(x + ffn, gf_ref[...])


def _out_ffn(x, a, c, w_out_bf, ln2_g, w_up_bf, w_down_bf, ln_f_g, tm, ff_chunk):
    n = x.shape[0]
    rows = lambda width: pl.BlockSpec((tm, width), lambda i: (i, 0))
    return pl.pallas_call(
        functools.partial(_out_ffn_kernel, ff_chunk=ff_chunk),
        grid=(n // tm,),
        in_specs=[rows(D_MODEL), rows(WIDTH_A), rows(WIDTH_B), _resident(w_out_bf.shape), _resident(ln2_g.shape),
                  _resident(w_up_bf.shape), _resident(w_down_bf.shape), _resident(ln_f_g.shape)],
        out_specs=rows(D_MODEL),
        out_shape=jax.ShapeDtypeStruct(x.shape, F32),
        compiler_params=_params("parallel"),
        name="out_ffn",
    )(x, a, c, w_out_bf, ln2_g, w_up_bf, w_down_bf, ln_f_g)


def _bucket_of(rel):
    half = N_BUCKETS // 2
    max_exact = half // 2
    n = -rel
    ret = jnp.where(n < 0, half, 0)
    n = jnp.abs(n)
    nf = jnp.maximum(n, 1).astype(F32)
    large = max_exact + (jnp.log(nf / max_exact) / math.log(MAX_DISTANCE / max_exact)
                         * (half - max_exact)).astype(jnp.int32)
    large = jnp.minimum(large, half - 1)
    return ret + jnp.where(n < max_exact, n, large)


def _bias_table(rel_bias, q_pos, k_pos, far_rel):
    shifted = (rel_bias - rel_bias[_bucket_of(far_rel)]).T
    bucket = _bucket_of(k_pos[None, :] - q_pos[:, None])
    hit = bucket[None, :, :, None] == jnp.arange(N_BUCKETS)
    table = jnp.sum(jnp.where(hit, shifted[:, None, None, :], 0.0), axis=-1)
    visible = (k_pos[None, :] // CHUNK) <= (q_pos[:, None] // CHUNK)
    return jnp.where(visible[None], table, NEG_INF).astype(F32)


def kernel(x_prompt, x_sample, cache_k, cache_v, state_conv, rel_bias, ln1_g, w_in, lambda_q1, lambda_k1, lambda_q2,
           lambda_k2, subln_g, w_dw, b_dw, conv_ln_g, conv_ln_b, w_out, ln2_g, w_up, w_down, ln_f_g):
    batch, seq, _ = x_prompt.shape
    dec_batch, dec_seq, _ = x_sample.shape
    past_len = cache_k.shape[2]
    tq = 256
    assert w_in.shape[0] == 1 and seq % (2 * tq) == 0 and seq >= CONV_PAD and dec_seq >= CONV_PAD

    w_in_bf, w_out_bf = w_in[0].astype(BF16), w_out[0].astype(BF16)
    w_up_bf, w_down_bf = w_up[0].astype(BF16), w_down[0].astype(BF16)
    w_taps = w_dw[0].reshape(CONV_WIDTH, WIDTH_B // LANES, 1, LANES).transpose(1, 0, 2, 3)
    w_rows = jnp.broadcast_to(w_taps, (WIDTH_B // LANES, CONV_WIDTH, SUBLANES, LANES))
    lam_params = (lambda_q1, lambda_k1, lambda_q2, lambda_k2)

    far_rel = jnp.int32(-(2 * tq))
    bias_tail = _bias_table(rel_bias, tq + jnp.arange(tq), jnp.arange(2 * tq), far_rel).transpose(0, 2, 1)
    q_pos_s = past_len + jnp.arange(dec_seq)
    bias_past = _bias_table(rel_bias, q_pos_s, jnp.arange(past_len), far_rel)
    bias_new = _bias_table(rel_bias, q_pos_s, q_pos_s, far_rel)

    xp = x_prompt.reshape(batch * seq, D_MODEL)
    q_p, k_p, v_p, kb_p, _, vt_p, u_p = _in_proj(xp, ln1_g, w_in_bf, tm=512)
    a_p = _prompt_attn(q_p, kb_p, vt_p, bias_tail, lam_params, subln_g.reshape(V_DIM, 1), batch, seq, tq)
    u_p3 = u_p.reshape(batch, seq, WIDTH_B)
    c_p = _conv(jnp.zeros((batch, HALO, WIDTH_B), F32), u_p3, w_rows, b_dw, conv_ln_g, conv_ln_b, tt=512, rc=64)
    y_p = _out_ffn(xp, a_p, c_p.reshape(batch * seq, WIDTH_B), w_out_bf, ln2_g, w_up_bf, w_down_bf,
                   ln_f_g[None, :], tm=512, ff_chunk=1024)

    xs = x_sample.reshape(dec_batch * dec_seq, D_MODEL)
    q_s, k_s, v_s, kb_s, vb_s, _, u_s = _in_proj(xs, ln1_g, w_in_bf, tm=512)
    by_head = lambda cache: cache[0].reshape(dec_batch, past_len * N_HEADS, V_DIM)
    a_s = _sample_attn(q_s, by_head(cache_k), by_head(cache_v), kb_s, vb_s, bias_past, bias_new,
                       lam_params, subln_g, past_len)
    u_s3 = u_s.reshape(dec_batch, dec_seq, WIDTH_B)
    prefix_s = jnp.pad(state_conv[0], ((0, 0), (HALO - CONV_PAD, 0), (0, 0)))
    c_s = _conv(prefix_s, u_s3, w_rows, b_dw, conv_ln_g, conv_ln_b, tt=dec_seq, rc=dec_seq)
    y_s = _out_ffn(xs, a_s, c_s.reshape(dec_batch * dec_seq, WIDTH_B), w_out_bf, ln2_g, w_up_bf, w_down_bf,
                   ln_f_g[None, :], tm=512, ff_chunk=1024)

    heads = lambda t, b, s: t.reshape(1, b, s, N_HEADS, V_DIM)
    return (y_p.reshape(batch, seq, D_MODEL), y_s.reshape(dec_batch, dec_seq, D_MODEL),
            heads(k_p, batch, seq), heads(v_p, batch, seq), u_p3[None, :, seq - CONV_PAD:],
            heads(k_s, dec_batch, dec_seq), heads(v_s, dec_batch, dec_seq), u_s3[None, :, dec_seq - CONV_PAD:])
```

```python
import functools
import math

import jax
import jax.numpy as jnp
from jax import lax
from jax.experimental import pallas as pl
from jax.experimental.pallas import tpu as pltpu

D_MODEL = 1024
CHUNK = 64
N_HEADS = 4
HEAD_DIM = 64
V_DIM = 2 * HEAD_DIM
QK_WIDTH = N_HEADS * 2 * HEAD_DIM
WIDTH_A = N_HEADS * V_DIM
WIDTH_B = D_MODEL - WIDTH_A
CONV_WIDTH = 31
CONV_PAD = CONV_WIDTH - 1
D_FF = 4 * D_MODEL
N_BUCKETS = 32
MAX_DISTANCE = 128
EPS = 1e-6
NEG_INF = -1e30
LAM_INIT = 0.8 - 0.6 * math.exp(-0.3 * 0)

LANES = 128
SUBLANES = 8
HALO = 32
VMEM_LIMIT_BYTES = 56 * 1024 * 1024

F32 = jnp.float32
BF16 = jnp.bfloat16
_NT = (((1,), (1,)), ((), ()))


def _sigmoid(x):
    return 1.0 / (1.0 + jnp.exp(-x))


def _rms_norm_rows(x, g):
    return x * lax.rsqrt(jnp.mean(x * x, axis=-1, keepdims=True) + EPS) * g


def _params(*semantics):
    return pltpu.CompilerParams(dimension_semantics=semantics, vmem_limit_bytes=VMEM_LIMIT_BYTES)


def _resident(shape):
    zeros = (0,) * len(shape)
    return pl.BlockSpec(shape, lambda *_: zeros, pipeline_mode=pl.Buffered(1))


def _head_cols(h):
    return slice(h * V_DIM, (h + 1) * V_DIM)


def _head_rows(h, n):
    return pl.ds(h, n, stride=N_HEADS)


def _in_proj_kernel(x_ref, g_ref, w_ref, q_ref, k_ref, v_ref, kb_ref, vb_ref, u_ref):
    tm = x_ref.shape[0]
    h = _rms_norm_rows(x_ref[...], g_ref[...]).astype(BF16)

    def cols(lo, hi):
        return jnp.dot(h, w_ref[:, lo:hi], preferred_element_type=F32)

    a_cols = 2 * QK_WIDTH + WIDTH_A
    q_ref[...] = (cols(0, QK_WIDTH) * HEAD_DIM ** -0.5).astype(BF16)
    k, v = cols(QK_WIDTH, 2 * QK_WIDTH), cols(2 * QK_WIDTH, a_cols)
    for full, out_ref, bf_ref in ((k, k_ref, kb_ref), (v, v_ref, vb_ref)):
        bf_ref[...] = full.astype(BF16)
        for hd in range(N_HEADS):
            out_ref[_head_rows(hd, tm), :] = full[:, _head_cols(hd)]
    u_ref[...] = cols(a_cols, a_cols + WIDTH_B) * _sigmoid(cols(a_cols + WIDTH_B, a_cols + 2 * WIDTH_B))


def _in_proj(x, ln_g, w_in_bf, tm):
    n = x.shape[0]
    rows = lambda width: pl.BlockSpec((tm, width), lambda i: (i, 0))
    by_head = pl.BlockSpec((tm * N_HEADS, V_DIM), lambda i: (i, 0))
    wide = lambda dt: jax.ShapeDtypeStruct((n, QK_WIDTH), dt)
    tall = jax.ShapeDtypeStruct((n * N_HEADS, V_DIM), F32)
    return pl.pallas_call(
        _in_proj_kernel,
        grid=(n // tm,),
        in_specs=[rows(D_MODEL), _resident((1, D_MODEL)), _resident(w_in_bf.shape)],
        out_specs=[rows(QK_WIDTH), by_head, by_head, rows(QK_WIDTH), rows(WIDTH_A), rows(WIDTH_B)],
        out_shape=[wide(BF16), tall, tall, wide(BF16), wide(BF16), wide(F32)],
        compiler_params=_params("parallel"),
        name="in_proj",
    )(x, ln_g, w_in_bf)


def _lambda(lq1_ref, lk1_ref, lq2_ref, lk2_ref):
    d1 = jnp.sum(lq1_ref[...] * lk1_ref[...], axis=-1, keepdims=True)
    d2 = jnp.sum(lq2_ref[...] * lk2_ref[...], axis=-1, keepdims=True)
    return jnp.exp(d1) - jnp.exp(d2) + LAM_INIT


def _stack_maps(qh):
    lane = lax.broadcasted_iota(jnp.int32, qh.shape, 1)
    zero = jnp.zeros_like(qh)
    return jnp.concatenate([jnp.where(lane < HEAD_DIM, qh, zero), jnp.where(lane >= HEAD_DIM, qh, zero)], axis=0)


def _with_ones(v):
    return jnp.concatenate([v, jnp.ones_like(v)], axis=1)


def _finish_head(acc, lam, g, tq):
    o1 = acc[:tq, :V_DIM] / acc[:tq, V_DIM:]
    o2 = acc[tq:, :V_DIM] / acc[tq:, V_DIM:]
    o = o1 - lam * o2
    return _rms_norm_rows(o, g) * (1.0 - LAM_INIT)


def _prompt_attn_kernel(q_ref, k_ref, v_ref, bt_ref, lq1_ref, lk1_ref, lq2_ref, lk2_ref, g_ref,
                        o_ref, q2_sc, m_sc, acc_sc, *, tq):
    i = pl.program_id(1)
    for h in range(N_HEADS):
        q2_sc[h] = _stack_maps(q_ref[:, _head_cols(h)])
    m_sc[...] = jnp.full(m_sc.shape, NEG_INF, F32)
    acc_sc[...] = jnp.zeros(acc_sc.shape, F32)

    def key_tile(start, tk, bias_of_head):
        for h in range(N_HEADS):
            hs = _head_cols(h)
            k_tile, v_tile = k_ref[pl.ds(start, tk), hs], _with_ones(v_ref[pl.ds(start, tk), hs])
            for rows in (slice(0, tq), slice(tq, 2 * tq)):
                s = lax.dot_general(q2_sc[h, rows], k_tile, _NT, preferred_element_type=F32)
                if bias_of_head is not None:
                    s = s + bias_of_head(h)
                m_prev = m_sc[h, rows]
                m_new = jnp.maximum(m_prev, jnp.max(s, axis=1, keepdims=True))
                alpha = jnp.exp(m_prev - m_new)
                p = jnp.exp(s - jnp.concatenate([m_new] * (tk // LANES), axis=1))
                pv = jnp.dot(p.astype(BF16), v_tile, preferred_element_type=F32)
                acc_sc[h, rows] = jnp.concatenate([alpha, alpha], axis=1) * acc_sc[h, rows] + pv
                m_sc[h, rows] = m_new

    n_far = jnp.maximum(i - 1, 0)

    def far_pair(j, carry):
        key_tile(pl.multiple_of(j * 2 * tq, 2 * tq), 2 * tq, None)
        return carry

    lax.fori_loop(0, n_far // 2, far_pair, 0)

    @pl.when(n_far % 2 == 1)
    def _():
        key_tile(pl.multiple_of((n_far - 1) * tq, tq), tq, None)

    @pl.when(i >= 1)
    def _():
        key_tile(pl.multiple_of((i - 1) * tq, tq), 2 * tq, lambda h: bt_ref[h])

    @pl.when(i == 0)
    def _():
        key_tile(0, tq, lambda h: bt_ref[h, :, tq:])

    lam = _lambda(lq1_ref, lk1_ref, lq2_ref, lk2_ref)
    for h in range(N_HEADS):
        o_ref[:, _head_cols(h)] = _finish_head(acc_sc[h], lam, g_ref[...], tq).astype(o_ref.dtype)


def _prompt_attn(q, kb, vb, bias_tail, lam_params, subln_g, batch, seq, tq):
    nq = seq // tq
    small = [_resident(p.shape) for p in lam_params]
    return pl.pallas_call(
        functools.partial(_prompt_attn_kernel, tq=tq),
        grid=(batch, nq),
        in_specs=[pl.BlockSpec((tq, QK_WIDTH), lambda b, i: (b * nq + i, 0)),
                  pl.BlockSpec((seq, QK_WIDTH), lambda b, i: (b, 0)),
                  pl.BlockSpec((seq, WIDTH_A), lambda b, i: (b, 0)),
                  _resident(bias_tail.shape), *small, _resident(subln_g.shape)],
        out_specs=pl.BlockSpec((tq, WIDTH_A), lambda b, i: (b * nq + i, 0)),
        out_shape=jax.ShapeDtypeStruct((batch * seq, WIDTH_A), BF16),
        scratch_shapes=[pltpu.VMEM((N_HEADS, 2 * tq, V_DIM), BF16), pltpu.VMEM((N_HEADS, 2 * tq, LANES), F32),
                        pltpu.VMEM((N_HEADS, 2 * tq, 2 * V_DIM), F32)],
        compiler_params=_params("parallel", "arbitrary"),
        name="prompt_attn",
    )(q, kb, vb, bias_tail, *lam_params, subln_g)


def _sample_attn_kernel(q_ref, ck_ref, cv_ref, kn_ref, vn_ref, bp_ref, bn_ref, lq1_ref, lk1_ref, lq2_ref, lk2_ref,
                        g_ref, o_ref, *, past_len):
    t = q_ref.shape[0]
    lam = _lambda(lq1_ref, lk1_ref, lq2_ref, lk2_ref)
    for h in range(N_HEADS):
        hs = _head_cols(h)
        q2 = _stack_maps(q_ref[:, hs])
        bias_past, bias_new = bp_ref[h], bn_ref[h]
        k_past = ck_ref[0, _head_rows(h, past_len), :].astype(BF16)
        s_past = lax.dot_general(q2, k_past, _NT, preferred_element_type=F32)
        s_past = s_past + jnp.concatenate([bias_past, bias_past], axis=0)
        s_new = lax.dot_general(q2, kn_ref[:, hs], _NT, preferred_element_type=F32)
        s_new = s_new + jnp.concatenate([bias_new, bias_new], axis=0)
        m = jnp.maximum(jnp.max(s_past, axis=1, keepdims=True), jnp.max(s_new, axis=1, keepdims=True))
        p_past = jnp.exp(s_past - m).astype(BF16)
        p_new = jnp.exp(s_new - m).astype(BF16)
        v_past = cv_ref[0, _head_rows(h, past_len), :].astype(BF16)
        acc = (jnp.dot(p_past, _with_ones(v_past), preferred_element_type=F32)
               + jnp.dot(p_new, _with_ones(vn_ref[:, hs]), preferred_element_type=F32))
        o_ref[:, hs] = _finish_head(acc, lam, g_ref[...], t).astype(o_ref.dtype)


def _sample_attn(q, cache_k, cache_v, kb, vb, bias_past, bias_new, lam_params, subln_g, past_len):
    batch = cache_k.shape[0]
    t = q.shape[0] // batch
    small = [_resident(p.shape) for p in lam_params]
    new_rows = pl.BlockSpec((t, QK_WIDTH), lambda b: (b, 0))
    cached = pl.BlockSpec((1, past_len * N_HEADS, V_DIM), lambda b: (b, 0, 0))
    return pl.pallas_call(
        functools.partial(_sample_attn_kernel, past_len=past_len),
        grid=(batch,),
        in_specs=[new_rows, cached, cached, new_rows, new_rows,
                  _resident(bias_past.shape), _resident(bias_new.shape), *small, _resident(subln_g.shape)],
        out_specs=new_rows,
        out_shape=jax.ShapeDtypeStruct(q.shape, BF16),
        compiler_params=_params("parallel"),
        name="sample_attn",
    )(q, cache_k, cache_v, kb, vb, bias_past, bias_new, *lam_params, subln_g)


def _conv_kernel(pre_ref, halo_ref, u_ref, wb_ref, b_ref, g_ref, beta_ref, c_ref, ubuf, ybuf, *, tt, rc):
    n_groups = WIDTH_B // LANES
    n_streams = u_ref.shape[0]
    n_chunks = n_streams * (tt // rc)
    for s in range(n_streams):
        history = jnp.where(pl.program_id(1) == 0, pre_ref[s], halo_ref[s])
        for k in range(tt // rc):
            for lg in range(n_groups):
                ls = slice(lg * LANES, (lg + 1) * LANES)
                slab = (s * (tt // rc) + k) * n_groups + lg
                ubuf[slab, 0:HALO, :] = history[:, ls] if k == 0 else u_ref[s, k * rc - HALO:k * rc, ls]
                ubuf[slab, HALO:, :] = u_ref[s, k * rc:(k + 1) * rc, ls]

    def conv_slab(slab, carry):
        lg = slab % n_groups
        acc = None
        for j in range(CONV_WIDTH):
            first = HALO - CONV_PAD + j
            window = ubuf[slab, first:first + rc, :].reshape(rc // SUBLANES, SUBLANES, LANES)
            term = window * wb_ref[lg, j]
            acc = term if acc is None else acc + term
        ybuf[slab] = acc.reshape(rc, LANES)
        return carry

    lax.fori_loop(0, n_chunks * n_groups, conv_slab, 0)

    for c in range(n_chunks):
        s, k = divmod(c, tt // rc)
        y = jnp.concatenate([ybuf[c * n_groups + lg] for lg in range(n_groups)], axis=1) + b_ref[...]
        mu = jnp.mean(y, axis=-1, keepdims=True)
        var = jnp.mean(jnp.square(y - mu), axis=-1, keepdims=True)
        yn = (y - mu) * lax.rsqrt(var + EPS) * g_ref[...] + beta_ref[...]
        c_ref[s, k * rc:(k + 1) * rc, :] = (yn * _sigmoid(yn)).astype(c_ref.dtype)


def _conv(prefix, u, w_rows, b_dw, ln_g, ln_b, streams, tt, rc):
    batch, seq, _ = u.shape
    halo_blocks = tt // HALO
    slabs = streams * (tt // rc) * (WIDTH_B // LANES)
    small = [_resident(p.shape) for p in (w_rows, b_dw, ln_g, ln_b)]
    return pl.pallas_call(
        functools.partial(_conv_kernel, tt=tt, rc=rc),
        grid=(batch // streams, seq // tt),
        in_specs=[pl.BlockSpec((streams, HALO, WIDTH_B), lambda b, t: (b, 0, 0)),
                  pl.BlockSpec((streams, HALO, WIDTH_B), lambda b, t: (b, jnp.maximum(t * halo_blocks - 1, 0), 0)),
                  pl.BlockSpec((streams, tt, WIDTH_B), lambda b, t: (b, t, 0)), *small],
        out_specs=pl.BlockSpec((streams, tt, WIDTH_B), lambda b, t: (b, t, 0)),
        out_shape=jax.ShapeDtypeStruct(u.shape, BF16),
        scratch_shapes=[pltpu.VMEM((slabs, HALO + rc, LANES), F32), pltpu.VMEM((slabs, rc, LANES), F32)],
        compiler_params=_params("parallel", "parallel"),
        name="conv_ln_swish",
    )(prefix, u, u, w_rows, b_dw, ln_g, ln_b)


def _out_ffn_kernel(x_ref, a_ref, c_ref, wo_ref, g2_ref, wu_ref, wd_ref, gf_ref, y_ref, *, ff_chunk):
    x = (x_ref[...]
         + jnp.dot(a_ref[...], wo_ref[:WIDTH_A, :], preferred_element_type=F32)
         + jnp.dot(c_ref[...], wo_ref[WIDTH_A:, :], preferred_element_type=F32))
    h2 = _rms_norm_rows(x, g2_ref[...]).astype(BF16)
    ffn = None
    for lo in range(0, D_FF, ff_chunk):
        hid = jnp.dot(h2, wu_ref[:, lo:lo + ff_chunk], preferred_element_type=F32)
        hid = jnp.square(jnp.maximum(hid, 0.0)).astype(BF16)
        down = jnp.dot(hid, wd_ref[lo:lo + ff_chunk, :], preferred_element_type=F32)
        ffn = down if ffn is None else ffn + down
    y_ref[...] = _rms_norm_rows(x + ffn, gf_ref[...])


def _out_ffn(x, a, c, w_out_bf, ln2_g, w_up_bf, w_down_bf, ln_f_g, tm, ff_chunk):
    n = x.shape[0]
    rows = lambda width: pl.BlockSpec((tm, width), lambda i: (i, 0))
    return pl.pallas_call(
        functools.partial(_out_ffn_kernel, ff_chunk=ff_chunk),
        grid=(n // tm,),
        in_specs=[rows(D_MODEL), rows(WIDTH_A), rows(WIDTH_B), _resident(w_out_bf.shape), _resident(ln2_g.shape),
                  _resident(w_up_bf.shape), _resident(w_down_bf.shape), _resident(ln_f_g.shape)],
        out_specs=rows(D_MODEL),
        out_shape=jax.ShapeDtypeStruct(x.shape, F32),
        compiler_params=_params("parallel"),
        name="out_ffn",
    )(x, a, c, w_out_bf, ln2_g, w_up_bf, w_down_bf, ln_f_g)


def _bucket_of(rel):
    half = N_BUCKETS // 2
    max_exact = half // 2
    n = -rel
    ret = jnp.where(n < 0, half, 0)
    n = jnp.abs(n)
    nf = jnp.maximum(n, 1).astype(F32)
    large = max_exact + (jnp.log(nf / max_exact) / math.log(MAX_DISTANCE / max_exact)
                         * (half - max_exact)).astype(jnp.int32)
    large = jnp.minimum(large, half - 1)
    return ret + jnp.where(n < max_exact, n, large)


def _bias_table(rel_bias, q0, nq, k0, nk, far_rel):
    span = nq + nk - 1
    shifted = (rel_bias - rel_bias[_bucket_of(far_rel)]).T
    hit = _bucket_of(k0 - (q0 + nq - 1) + jnp.arange(span))[:, None] == jnp.arange(N_BUCKETS)
    by_rel = jnp.sum(jnp.where(hit[None], shifted[:, None, :], 0.0), axis=-1)
    rows = jnp.pad(by_rel, ((0, 0), (0, 1)))
    rows = jnp.broadcast_to(rows[:, None, :], (N_HEADS, nq, span + 1)).reshape(N_HEADS, nq * (span + 1))
    table = rows[:, :nq * span].reshape(N_HEADS, nq, span)[:, :, nq - 1:]
    q_pos, k_pos = q0 + jnp.arange(nq), k0 + jnp.arange(nk)
    visible = (k_pos[None, :] // CHUNK) <= (q_pos[:, None] // CHUNK)
    return jnp.where(visible[None], table, NEG_INF).astype(F32)


def kernel(x_prompt, x_sample, cache_k, cache_v, state_conv, rel_bias, ln1_g, w_in, lambda_q1, lambda_k1, lambda_q2,
           lambda_k2, subln_g, w_dw, b_dw, conv_ln_g, conv_ln_b, w_out, ln2_g, w_up, w_down, ln_f_g):
    batch, seq, _ = x_prompt.shape
    dec_batch, dec_seq, _ = x_sample.shape
    past_len = cache_k.shape[2]
    tq = 256
    assert w_in.shape[0] == 1 and seq % (2 * tq) == 0 and seq >= CONV_PAD and dec_seq >= CONV_PAD

    w_in_bf, w_out_bf = w_in[0].astype(BF16), w_out[0].astype(BF16)
    w_up_bf, w_down_bf = w_up[0].astype(BF16), w_down[0].astype(BF16)
    w_taps = w_dw[0].reshape(CONV_WIDTH, WIDTH_B // LANES, 1, LANES).transpose(1, 0, 2, 3)
    w_rows = jnp.broadcast_to(w_taps, (WIDTH_B // LANES, CONV_WIDTH, SUBLANES, LANES))
    lam_params = (lambda_q1, lambda_k1, lambda_q2, lambda_k2)

    far_rel = jnp.int32(-(2 * tq))
    bias_tail = _bias_table(rel_bias, tq, tq, 0, 2 * tq, far_rel)
    bias_past = _bias_table(rel_bias, past_len, dec_seq, 0, past_len, far_rel)
    bias_new = _bias_table(rel_bias, past_len, dec_seq, past_len, dec_seq, far_rel)

    xp = x_prompt.reshape(batch * seq, D_MODEL)
    q_p, k_p, v_p, kb_p, vb_p, u_p = _in_proj(xp, ln1_g, w_in_bf, tm=512)
    a_p = _prompt_attn(q_p, kb_p, vb_p, bias_tail, lam_params, subln_g, batch, seq, tq)
    u_p3 = u_p.reshape(batch, seq, WIDTH_B)
    c_p = _conv(jnp.zeros((batch, HALO, WIDTH_B), F32), u_p3, w_rows, b_dw, conv_ln_g, conv_ln_b,
                streams=1, tt=512, rc=128)
    y_p = _out_ffn(xp, a_p, c_p.reshape(batch * seq, WIDTH_B), w_out_bf, ln2_g, w_up_bf, w_down_bf,
                   ln_f_g[None, :], tm=512, ff_chunk=1024)

    xs = x_sample.reshape(dec_batch * dec_seq, D_MODEL)
    q_s, k_s, v_s, kb_s, vb_s, u_s = _in_proj(xs, ln1_g, w_in_bf, tm=512)
    by_head = lambda cache: cache[0].reshape(dec_batch, past_len * N_HEADS, V_DIM)
    a_s = _sample_attn(q_s, by_head(cache_k), by_head(cache_v), kb_s, vb_s, bias_past, bias_new,
                       lam_params, subln_g, past_len)
    u_s3 = u_s.reshape(dec_batch, dec_seq, WIDTH_B)
    prefix_s = jnp.pad(state_conv[0], ((0, 0), (HALO - CONV_PAD, 0), (0, 0)))
    c_s = _conv(prefix_s, u_s3, w_rows, b_dw, conv_ln_g, conv_ln_b, streams=8, tt=dec_seq, rc=dec_seq)
    y_s = _out_ffn(xs, a_s, c_s.reshape(dec_batch * dec_seq, WIDTH_B), w_out_bf, ln2_g, w_up_bf, w_down_bf,
                   ln_f_g[None, :], tm=512, ff_chunk=1024)

    heads = lambda t, b, s: t.reshape(1, b, s, N_HEADS, V_DIM)
    return (y_p.reshape(batch, seq, D_MODEL), y_s.reshape(dec_batch, dec_seq, D_MODEL),
            heads(k_p, batch, seq), heads(v_p, batch, seq), u_p3[None, :, seq - CONV_PAD:],
            heads(k_s, dec_batch, dec_seq), heads(v_s, dec_batch, dec_seq), u_s3[None, :, dec_seq - CONV_PAD:])
```

```python
import functools
import math

import jax
import jax.numpy as jnp
from jax import lax
from jax.experimental import pallas as pl
from jax.experimental.pallas import tpu as pltpu

D_MODEL = 1024
CHUNK = 64
N_HEADS = 4
HEAD_DIM = 64
V_DIM = 2 * HEAD_DIM
QK_WIDTH = N_HEADS * 2 * HEAD_DIM
WIDTH_A = N_HEADS * V_DIM
WIDTH_B = D_MODEL - WIDTH_A
CONV_WIDTH = 31
CONV_PAD = CONV_WIDTH - 1
D_FF = 4 * D_MODEL
N_BUCKETS = 32
MAX_DISTANCE = 128
EPS = 1e-6
NEG_INF = -1e30
LAM_INIT = 0.8 - 0.6 * math.exp(-0.3 * 0)

LANES = 128
SUBLANES = 8
HALO = 32
VMEM_LIMIT_BYTES = 56 * 1024 * 1024

F32 = jnp.float32
BF16 = jnp.bfloat16
_NT = (((1,), (1,)), ((), ()))


def _sigmoid(x):
    return 1.0 / (1.0 + jnp.exp(-x))


def _rms_norm_rows(x, g):
    return x * lax.rsqrt(jnp.mean(x * x, axis=-1, keepdims=True) + EPS) * g


def _params(*semantics):
    return pltpu.CompilerParams(dimension_semantics=semantics, vmem_limit_bytes=VMEM_LIMIT_BYTES)


def _resident(shape):
    zeros = (0,) * len(shape)
    return pl.BlockSpec(shape, lambda *_: zeros, pipeline_mode=pl.Buffered(1))


def _head_cols(h):
    return slice(h * V_DIM, (h + 1) * V_DIM)


def _head_rows(h, n):
    return pl.ds(h, n, stride=N_HEADS)


def _in_proj_kernel(x_ref, g_ref, w_ref, q_ref, k_ref, v_ref, kb_ref, vb_ref, u_ref):
    tm = x_ref.shape[0]
    h = _rms_norm_rows(x_ref[...], g_ref[...]).astype(BF16)

    def cols(lo, hi):
        return jnp.dot(h, w_ref[:, lo:hi], preferred_element_type=F32)

    a_cols = 2 * QK_WIDTH + WIDTH_A
    q_ref[...] = (cols(0, QK_WIDTH) * HEAD_DIM ** -0.5).astype(BF16)
    k, v = cols(QK_WIDTH, 2 * QK_WIDTH), cols(2 * QK_WIDTH, a_cols)
    for full, out_ref, bf_ref in ((k, k_ref, kb_ref), (v, v_ref, vb_ref)):
        bf_ref[...] = full.astype(BF16)
        for hd in range(N_HEADS):
            out_ref[_head_rows(hd, tm), :] = full[:, _head_cols(hd)]
    u_ref[...] = cols(a_cols, a_cols + WIDTH_B) * _sigmoid(cols(a_cols + WIDTH_B, a_cols + 2 * WIDTH_B))


def _in_proj(x, ln_g, w_in_bf, tm):
    n = x.shape[0]
    rows = lambda width: pl.BlockSpec((tm, width), lambda i: (i, 0))
    by_head = pl.BlockSpec((tm * N_HEADS, V_DIM), lambda i: (i, 0))
    wide = lambda dt: jax.ShapeDtypeStruct((n, QK_WIDTH), dt)
    tall = jax.ShapeDtypeStruct((n * N_HEADS, V_DIM), F32)
    return pl.pallas_call(
        _in_proj_kernel,
        grid=(n // tm,),
        in_specs=[rows(D_MODEL), _resident((1, D_MODEL)), _resident(w_in_bf.shape)],
        out_specs=[rows(QK_WIDTH), by_head, by_head, rows(QK_WIDTH), rows(WIDTH_A), rows(WIDTH_B)],
        out_shape=[wide(BF16), tall, tall, wide(BF16), wide(BF16), wide(F32)],
        compiler_params=_params("parallel"),
        name="in_proj",
    )(x, ln_g, w_in_bf)


def _lambda(lq1_ref, lk1_ref, lq2_ref, lk2_ref):
    d1 = jnp.sum(lq1_ref[...] * lk1_ref[...], axis=-1, keepdims=True)
    d2 = jnp.sum(lq2_ref[...] * lk2_ref[...], axis=-1, keepdims=True)
    return jnp.exp(d1) - jnp.exp(d2) + LAM_INIT


def _stack_maps(qh):
    lane = lax.broadcasted_iota(jnp.int32, qh.shape, 1)
    zero = jnp.zeros_like(qh)
    return jnp.concatenate([jnp.where(lane < HEAD_DIM, qh, zero), jnp.where(lane >= HEAD_DIM, qh, zero)], axis=0)


def _with_ones(v):
    return jnp.concatenate([v, jnp.ones_like(v)], axis=1)


def _finish_head(acc, lam, g, tq):
    o1 = acc[:tq, :V_DIM] / acc[:tq, V_DIM:]
    o2 = acc[tq:, :V_DIM] / acc[tq:, V_DIM:]
    o = o1 - lam * o2
    return _rms_norm_rows(o, g) * (1.0 - LAM_INIT)


def _prompt_attn_kernel(q_ref, k_ref, v_ref, bt_ref, lq1_ref, lk1_ref, lq2_ref, lk2_ref, g_ref,
                        o_ref, q2_sc, m_sc, acc_sc, *, tq):
    def key_tile(start, tk, bias_of_head):
        for h in range(N_HEADS):
            hs = _head_cols(h)
            k_tile, v_tile = k_ref[start:start + tk, hs], _with_ones(v_ref[start:start + tk, hs])
            for rows in (slice(0, tq), slice(tq, 2 * tq)):
                s = lax.dot_general(q2_sc[h, rows], k_tile, _NT, preferred_element_type=F32)
                if bias_of_head is not None:
                    s = s + bias_of_head(h)
                m_prev = m_sc[h, rows]
                m_new = jnp.maximum(m_prev, jnp.max(s, axis=1, keepdims=True))
                alpha = jnp.exp(m_prev - m_new)
                p = jnp.exp(s - jnp.concatenate([m_new] * (tk // LANES), axis=1))
                pv = jnp.dot(p.astype(BF16), v_tile, preferred_element_type=F32)
                acc_sc[h, rows] = jnp.concatenate([alpha, alpha], axis=1) * acc_sc[h, rows] + pv
                m_sc[h, rows] = m_new

    def query_tile(i):
        for h in range(N_HEADS):
            q2_sc[h] = _stack_maps(q_ref[:, _head_cols(h)])
        m_sc[...] = jnp.full(m_sc.shape, NEG_INF, F32)
        acc_sc[...] = jnp.zeros(acc_sc.shape, F32)
        n_far = max(i - 1, 0)
        for j in range(n_far // 2):
            key_tile(j * 2 * tq, 2 * tq, None)
        if n_far % 2:
            key_tile((n_far - 1) * tq, tq, None)
        if i >= 1:
            key_tile((i - 1) * tq, 2 * tq, lambda h: bt_ref[h])
        else:
            key_tile(0, tq, lambda h: bt_ref[h, :, tq:])
        lam = _lambda(lq1_ref, lk1_ref, lq2_ref, lk2_ref)
        for h in range(N_HEADS):
            o_ref[:, _head_cols(h)] = _finish_head(acc_sc[h], lam, g_ref[...], tq).astype(o_ref.dtype)

    for i in range(k_ref.shape[0] // tq):
        pl.when(pl.program_id(1) == i)(functools.partial(query_tile, i))


def _prompt_attn(q, kb, vb, bias_tail, lam_params, subln_g, batch, seq, tq):
    nq = seq // tq
    small = [_resident(p.shape) for p in lam_params]
    return pl.pallas_call(
        functools.partial(_prompt_attn_kernel, tq=tq),
        grid=(batch, nq),
        in_specs=[pl.BlockSpec((tq, QK_WIDTH), lambda b, i: (b * nq + i, 0)),
                  pl.BlockSpec((seq, QK_WIDTH), lambda b, i: (b, 0)),
                  pl.BlockSpec((seq, WIDTH_A), lambda b, i: (b, 0)),
                  _resident(bias_tail.shape), *small, _resident(subln_g.shape)],
        out_specs=pl.BlockSpec((tq, WIDTH_A), lambda b, i: (b * nq + i, 0)),
        out_shape=jax.ShapeDtypeStruct((batch * seq, WIDTH_A), BF16),
        scratch_shapes=[pltpu.VMEM((N_HEADS, 2 * tq, V_DIM), BF16), pltpu.VMEM((N_HEADS, 2 * tq, LANES), F32),
                        pltpu.VMEM((N_HEADS, 2 * tq, 2 * V_DIM), F32)],
        compiler_params=_params("parallel", "arbitrary"),
        name="prompt_attn",
    )(q, kb, vb, bias_tail, *lam_params, subln_g)


def _sample_attn_kernel(q_ref, ck_ref, cv_ref, kn_ref, vn_ref, bp_ref, bn_ref, lq1_ref, lk1_ref, lq2_ref, lk2_ref,
                        g_ref, o_ref, *, past_len):
    t = q_ref.shape[0]
    lam = _lambda(lq1_ref, lk1_ref, lq2_ref, lk2_ref)
    for h in range(N_HEADS):
        hs = _head_cols(h)
        q2 = _stack_maps(q_ref[:, hs])
        bias_past, bias_new = bp_ref[h], bn_ref[h]
        k_past = ck_ref[0, _head_rows(h, past_len), :].astype(BF16)
        s_past = lax.dot_general(q2, k_past, _NT, preferred_element_type=F32)
        s_past = s_past + jnp.concatenate([bias_past, bias_past], axis=0)
        s_new = lax.dot_general(q2, kn_ref[:, hs], _NT, preferred_element_type=F32)
        s_new = s_new + jnp.concatenate([bias_new, bias_new], axis=0)
        m = jnp.maximum(jnp.max(s_past, axis=1, keepdims=True), jnp.max(s_new, axis=1, keepdims=True))
        p_past = jnp.exp(s_past - m).astype(BF16)
        p_new = jnp.exp(s_new - m).astype(BF16)
        v_past = cv_ref[0, _head_rows(h, past_len), :].astype(BF16)
        acc = (jnp.dot(p_past, _with_ones(v_past), preferred_element_type=F32)
               + jnp.dot(p_new, _with_ones(vn_ref[:, hs]), preferred_element_type=F32))
        o_ref[:, hs] = _finish_head(acc, lam, g_ref[...], t).astype(o_ref.dtype)


def _sample_attn(q, cache_k, cache_v, kb, vb, bias_past, bias_new, lam_params, subln_g, past_len):
    batch = cache_k.shape[0]
    t = q.shape[0] // batch
    small = [_resident(p.shape) for p in lam_params]
    new_rows = pl.BlockSpec((t, QK_WIDTH), lambda b: (b, 0))
    cached = pl.BlockSpec((1, past_len * N_HEADS, V_DIM), lambda b: (b, 0, 0))
    return pl.pallas_call(
        functools.partial(_sample_attn_kernel, past_len=past_len),
        grid=(batch,),
        in_specs=[new_rows, cached, cached, new_rows, new_rows,
                  _resident(bias_past.shape), _resident(bias_new.shape), *small, _resident(subln_g.shape)],
        out_specs=new_rows,
        out_shape=jax.ShapeDtypeStruct(q.shape, BF16),
        compiler_params=_params("parallel"),
        name="sample_attn",
    )(q, cache_k, cache_v, kb, vb, bias_past, bias_new, *lam_params, subln_g)


def _conv_kernel(pre_ref, halo_ref, u_ref, wb_ref, b_ref, g_ref, beta_ref, c_ref, ubuf, ybuf, *, tt, rc):
    n_groups = WIDTH_B // LANES
    n_streams = u_ref.shape[0]
    n_chunks = n_streams * (tt // rc)
    for s in range(n_streams):
        history = jnp.where(pl.program_id(1) == 0, pre_ref[s], halo_ref[s])
        for k in range(tt // rc):
            for lg in range(n_groups):
                ls = slice(lg * LANES, (lg + 1) * LANES)
                slab = (s * (tt // rc) + k) * n_groups + lg
                ubuf[slab, 0:HALO, :] = history[:, ls] if k == 0 else u_ref[s, k * rc - HALO:k * rc, ls]
                ubuf[slab, HALO:, :] = u_ref[s, k * rc:(k + 1) * rc, ls]

    def conv_slab(slab, carry):
        lg = slab % n_groups
        acc = None
        for j in range(CONV_WIDTH):
            first = HALO - CONV_PAD + j
            window = ubuf[slab, first:first + rc, :].reshape(rc // SUBLANES, SUBLANES, LANES)
            term = window * wb_ref[lg, j]
            acc = term if acc is None else acc + term
        ybuf[slab] = acc.reshape(rc, LANES)
        return carry

    lax.fori_loop(0, n_chunks * n_groups, conv_slab, 0)

    for c in range(n_chunks):
        s, k = divmod(c, tt // rc)
        y = jnp.concatenate([ybuf[c * n_groups + lg] for lg in range(n_groups)], axis=1) + b_ref[...]
        mu = jnp.mean(y, axis=-1, keepdims=True)
        var = jnp.mean(jnp.square(y - mu), axis=-1, keepdims=True)
        yn = (y - mu) * lax.rsqrt(var + EPS) * g_ref[...] + beta_ref[...]
        c_ref[s, k * rc:(k + 1) * rc, :] = (yn * _sigmoid(yn)).astype(c_ref.dtype)


def _conv(prefix, u, w_rows, b_dw, ln_g, ln_b, streams, tt, rc):
    batch, seq, _ = u.shape
    halo_blocks = tt // HALO
    slabs = streams * (tt // rc) * (WIDTH_B // LANES)
    small = [_resident(p.shape) for p in (w_rows, b_dw, ln_g, ln_b)]
    return pl.pallas_call(
        functools.partial(_conv_kernel, tt=tt, rc=rc),
        grid=(batch // streams, seq // tt),
        in_specs=[pl.BlockSpec((streams, HALO, WIDTH_B), lambda b, t: (b, 0, 0)),
                  pl.BlockSpec((streams, HALO, WIDTH_B), lambda b, t: (b, jnp.maximum(t * halo_blocks - 1, 0), 0)),
                  pl.BlockSpec((streams, tt, WIDTH_B), lambda b, t: (b, t, 0)), *small],
        out_specs=pl.BlockSpec((streams, tt, WIDTH_B), lambda b, t: (b, t, 0)),
        out_shape=jax.ShapeDtypeStruct(u.shape, BF16),
        scratch_shapes=[pltpu.VMEM((slabs, HALO + rc, LANES), F32), pltpu.VMEM((slabs, rc, LANES), F32)],
        compiler_params=_params("parallel", "parallel"),
        name="conv_ln_swish",
    )(prefix, u, u, w_rows, b_dw, ln_g, ln_b)


def _out_ffn_kernel(x_ref, a_ref, c_ref, wo_ref, g2_ref, wu_ref, wd_ref, gf_ref, y_ref, *, ff_chunk):
    x = (x_ref[...]
         + jnp.dot(a_ref[...], wo_ref[:WIDTH_A, :], preferred_element_type=F32)
         + jnp.dot(c_ref[...], wo_ref[WIDTH_A:, :], preferred_element_type=F32))
    h2 = _rms_norm_rows(x, g2_ref[...]).astype(BF16)
    ffn = None
    for lo in range(0, D_FF, ff_chunk):
        hid = jnp.dot(h2, wu_ref[:, lo:lo + ff_chunk], preferred_element_type=F32)
        hid = jnp.square(jnp.maximum(hid, 0.0)).astype(BF16)
        down = jnp.dot(hid, wd_ref[lo:lo + ff_chunk, :], preferred_element_type=F32)
        ffn = down if ffn is None else ffn + down
    y_ref[...] = _rms_norm_rows(x + ffn, gf_ref[...])


def _out_ffn(x, a, c, w_out_bf, ln2_g, w_up_bf, w_down_bf, ln_f_g, tm, ff_chunk):
    n = x.shape[0]
    rows = lambda width: pl.BlockSpec((tm, width), lambda i: (i, 0))
    return pl.pallas_call(
        functools.partial(_out_ffn_kernel, ff_chunk=ff_chunk),
        grid=(n // tm,),
        in_specs=[rows(D_MODEL), rows(WIDTH_A), rows(WIDTH_B), _resident(w_out_bf.shape), _resident(ln2_g.shape),
                  _resident(w_up_bf.shape), _resident(w_down_bf.shape), _resident(ln_f_g.shape)],
        out_specs=rows(D_MODEL),
        out_shape=jax.ShapeDtypeStruct(x.shape, F32),
        compiler_params=_params("parallel"),
        name="out_ffn",
    )(x, a, c, w_out_bf, ln2_g, w_up_bf, w_down_bf, ln_f_g)


def _bucket_of(rel):
    half = N_BUCKETS // 2
    max_exact = half // 2
    n = -rel
    ret = jnp.where(n < 0, half, 0)
    n = jnp.abs(n)
    nf = jnp.maximum(n, 1).astype(F32)
    large = max_exact + (jnp.log(nf / max_exact) / math.log(MAX_DISTANCE / max_exact)
                         * (half - max_exact)).astype(jnp.int32)
    large = jnp.minimum(large, half - 1)
    return ret + jnp.where(n < max_exact, n, large)


def _bias_table(rel_bias, q0, nq, k0, nk, far_rel):
    span = nq + nk - 1
    shifted = (rel_bias - rel_bias[_bucket_of(far_rel)]).T
    hit = _bucket_of(k0 - (q0 + nq - 1) + jnp.arange(span))[:, None] == jnp.arange(N_BUCKETS)
    by_rel = jnp.sum(jnp.where(hit[None], shifted[:, None, :], 0.0), axis=-1)
    rows = jnp.pad(by_rel, ((0, 0), (0, 1)))
    rows = jnp.broadcast_to(rows[:, None, :], (N_HEADS, nq, span + 1)).reshape(N_HEADS, nq * (span + 1))
    table = rows[:, :nq * span].reshape(N_HEADS, nq, span)[:, :, nq - 1:]
    q_pos, k_pos = q0 + jnp.arange(nq), k0 + jnp.arange(nk)
    visible = (k_pos[None, :] // CHUNK) <= (q_pos[:, None] // CHUNK)
    return jnp.where(visible[None], table, NEG_INF).astype(F32)


def kernel(x_prompt, x_sample, cache_k, cache_v, state_conv, rel_bias, ln1_g, w_in, lambda_q1, lambda_k1, lambda_q2,
           lambda_k2, subln_g, w_dw, b_dw, conv_ln_g, conv_ln_b, w_out, ln2_g, w_up, w_down, ln_f_g):
    batch, seq, _ = x_prompt.shape
    dec_batch, dec_seq, _ = x_sample.shape
    past_len = cache_k.shape[2]
    tq = 256
    assert w_in.shape[0] == 1 and seq % (2 * tq) == 0 and seq >= CONV_PAD and dec_seq >= CONV_PAD

    w_in_bf, w_out_bf = w_in[0].astype(BF16), w_out[0].astype(BF16)
    w_up_bf, w_down_bf = w_up[0].astype(BF16), w_down[0].astype(BF16)
    w_taps = w_dw[0].reshape(CONV_WIDTH, WIDTH_B // LANES, 1, LANES).transpose(1, 0, 2, 3)
    w_rows = jnp.broadcast_to(w_taps, (WIDTH_B // LANES, CONV_WIDTH, SUBLANES, LANES))
    lam_params = (lambda_q1, lambda_k1, lambda_q2, lambda_k2)

    far_rel = jnp.int32(-(2 * tq))
    bias_tail = _bias_table(rel_bias, tq, tq, 0, 2 * tq, far_rel)
    bias_past = _bias_table(rel_bias, past_len, dec_seq, 0, past_len, far_rel)
    bias_new = _bias_table(rel_bias, past_len, dec_seq, past_len, dec_seq, far_rel)

    xp = x_prompt.reshape(batch * seq, D_MODEL)
    q_p, k_p, v_p, kb_p, vb_p, u_p = _in_proj(xp, ln1_g, w_in_bf, tm=512)
    a_p = _prompt_attn(q_p, kb_p, vb_p, bias_tail, lam_params, subln_g, batch, seq, tq)
    u_p3 = u_p.reshape(batch, seq, WIDTH_B)
    c_p = _conv(jnp.zeros((batch, HALO, WIDTH_B), F32), u_p3, w_rows, b_dw, conv_ln_g, conv_ln_b,
                streams=1, tt=512, rc=128)
    y_p = _out_ffn(xp, a_p, c_p.reshape(batch * seq, WIDTH_B), w_out_bf, ln2_g, w_up_bf, w_down_bf,
                   ln_f_g[None, :], tm=512, ff_chunk=1024)

    xs = x_sample.reshape(dec_batch * dec_seq, D_MODEL)
    q_s, k_s, v_s, kb_s, vb_s, u_s = _in_proj(xs, ln1_g, w_in_bf, tm=512)
    by_head = lambda cache: cache[0].reshape(dec_batch, past_len * N_HEADS, V_DIM)
    a_s = _sample_attn(q_s, by_head(cache_k), by_head(cache_v), kb_s, vb_s, bias_past, bias_new,
                       lam_params, subln_g, past_len)
    u_s3 = u_s.reshape(dec_batch, dec_seq, WIDTH_B)
    prefix_s = jnp.pad(state_conv[0], ((0, 0), (HALO - CONV_PAD, 0), (0, 0)))
    c_s = _conv(prefix_s, u_s3, w_rows, b_dw, conv_ln_g, conv_ln_b, streams=8, tt=dec_seq, rc=dec_seq)
    y_s = _out_ffn(xs, a_s, c_s.reshape(dec_batch * dec_seq, WIDTH_B), w_out_bf, ln2_g, w_up_bf, w_down_bf,
                   ln_f_g[None, :], tm=512, ff_chunk=1024)

    heads = lambda t, b, s: t.reshape(1, b, s, N_HEADS, V_DIM)
    return (y_p.reshape(batch, seq, D_MODEL), y_s.reshape(dec_batch, dec_seq, D_MODEL),
            heads(k_p, batch, seq), heads(v_p, batch, seq), u_p3[None, :, seq - CONV_PAD:],
            heads(k_s, dec_batch, dec_seq), heads(v_s, dec_batch, dec_seq), u_s3[None, :, dec_seq - CONV_PAD:])
```

```python
import functools
import math

import jax
import jax.numpy as jnp
from jax import lax
from jax.experimental import pallas as pl
from jax.experimental.pallas import tpu as pltpu

D_MODEL = 1024
CHUNK = 64
N_HEADS = 4
HEAD_DIM = 64
V_DIM = 2 * HEAD_DIM
QK_WIDTH = N_HEADS * 2 * HEAD_DIM
WIDTH_A = N_HEADS * V_DIM
WIDTH_B = D_MODEL - WIDTH_A
CONV_WIDTH = 31
CONV_PAD = CONV_WIDTH - 1
D_FF = 4 * D_MODEL
N_BUCKETS = 32
MAX_DISTANCE = 128
EPS = 1e-6
NEG_INF = -1e30
LAM_INIT = 0.8 - 0.6 * math.exp(-0.3 * 0)

LANES = 128
SUBLANES = 8
HALO = 32
VMEM_LIMIT_BYTES = 56 * 1024 * 1024

F32 = jnp.float32
BF16 = jnp.bfloat16
_NT = (((1,), (1,)), ((), ()))


def _sigmoid(x):
    return 1.0 / (1.0 + jnp.exp(-x))


def _rms_norm_rows(x, g):
    return x * lax.rsqrt(jnp.mean(x * x, axis=-1, keepdims=True) + EPS) * g


def _params(*semantics):
    return pltpu.CompilerParams(dimension_semantics=semantics, vmem_limit_bytes=VMEM_LIMIT_BYTES)


def _resident(shape):
    zeros = (0,) * len(shape)
    return pl.BlockSpec(shape, lambda *_: zeros, pipeline_mode=pl.Buffered(1))


def _head_cols(h):
    return slice(h * V_DIM, (h + 1) * V_DIM)


def _head_rows(h, n):
    return pl.ds(h, n, stride=N_HEADS)


def _in_proj_kernel(x_ref, g_ref, w_ref, q_ref, k_ref, v_ref, kb_ref, vb_ref, u_ref):
    tm = x_ref.shape[0]
    h = _rms_norm_rows(x_ref[...], g_ref[...]).astype(BF16)

    def cols(lo, hi):
        return jnp.dot(h, w_ref[:, lo:hi], preferred_element_type=F32)

    a_cols = 2 * QK_WIDTH + WIDTH_A
    q_ref[...] = (cols(0, QK_WIDTH) * HEAD_DIM ** -0.5).astype(BF16)
    k, v = cols(QK_WIDTH, 2 * QK_WIDTH), cols(2 * QK_WIDTH, a_cols)
    for full, out_ref, bf_ref in ((k, k_ref, kb_ref), (v, v_ref, vb_ref)):
        bf_ref[...] = full.astype(BF16)
        for hd in range(N_HEADS):
            out_ref[_head_rows(hd, tm), :] = full[:, _head_cols(hd)]
    u_ref[...] = cols(a_cols, a_cols + WIDTH_B) * _sigmoid(cols(a_cols + WIDTH_B, a_cols + 2 * WIDTH_B))


def _in_proj(x, ln_g, w_in_bf, tm):
    n = x.shape[0]
    rows = lambda width: pl.BlockSpec((tm, width), lambda i: (i, 0))
    by_head = pl.BlockSpec((tm * N_HEADS, V_DIM), lambda i: (i, 0))
    wide = lambda dt: jax.ShapeDtypeStruct((n, QK_WIDTH), dt)
    tall = jax.ShapeDtypeStruct((n * N_HEADS, V_DIM), F32)
    return pl.pallas_call(
        _in_proj_kernel,
        grid=(n // tm,),
        in_specs=[rows(D_MODEL), _resident((1, D_MODEL)), _resident(w_in_bf.shape)],
        out_specs=[rows(QK_WIDTH), by_head, by_head, rows(QK_WIDTH), rows(WIDTH_A), rows(WIDTH_B)],
        out_shape=[wide(BF16), tall, tall, wide(BF16), wide(BF16), wide(F32)],
        compiler_params=_params("parallel"),
        name="in_proj",
    )(x, ln_g, w_in_bf)


def _lambda(lq1_ref, lk1_ref, lq2_ref, lk2_ref):
    d1 = jnp.sum(lq1_ref[...] * lk1_ref[...], axis=-1, keepdims=True)
    d2 = jnp.sum(lq2_ref[...] * lk2_ref[...], axis=-1, keepdims=True)
    return jnp.exp(d1) - jnp.exp(d2) + LAM_INIT


def _stack_maps(qh):
    lane = lax.broadcasted_iota(jnp.int32, qh.shape, 1)
    zero = jnp.zeros_like(qh)
    return jnp.concatenate([jnp.where(lane < HEAD_DIM, qh, zero), jnp.where(lane >= HEAD_DIM, qh, zero)], axis=0)


def _with_ones(v):
    return jnp.concatenate([v, jnp.ones_like(v)], axis=1)


def _finish_head(acc, lam, g, tq):
    o1 = acc[:tq, :V_DIM] / acc[:tq, V_DIM:]
    o2 = acc[tq:, :V_DIM] / acc[tq:, V_DIM:]
    o = o1 - lam * o2
    return _rms_norm_rows(o, g) * (1.0 - LAM_INIT)


def _prompt_attn_kernel(q_ref, k_ref, v_ref, bt_ref, lq1_ref, lk1_ref, lq2_ref, lk2_ref, g_ref,
                        o_ref, q2_sc, m_sc, acc_sc, *, tq):
    tiles_per_step = q_ref.shape[0] // tq

    def key_tile(slot, start, tk, bias_of_head):
        for h in range(N_HEADS):
            hs = _head_cols(h)
            k_tile, v_tile = k_ref[start:start + tk, hs], _with_ones(v_ref[start:start + tk, hs])
            for rows in (slice(0, tq), slice(tq, 2 * tq)):
                s = lax.dot_general(q2_sc[slot, h, rows], k_tile, _NT, preferred_element_type=F32)
                if bias_of_head is not None:
                    s = s + bias_of_head(h)
                m_prev = m_sc[slot, h, rows]
                m_new = jnp.maximum(m_prev, jnp.max(s, axis=1, keepdims=True))
                alpha = jnp.exp(m_prev - m_new)
                p = jnp.exp(s - jnp.concatenate([m_new] * (tk // LANES), axis=1))
                pv = jnp.dot(p.astype(BF16), v_tile, preferred_element_type=F32)
                acc_sc[slot, h, rows] = jnp.concatenate([alpha, alpha], axis=1) * acc_sc[slot, h, rows] + pv
                m_sc[slot, h, rows] = m_new

    def query_tile(i):
        slot = i % 2
        q_rows = slice((i % tiles_per_step) * tq, (i % tiles_per_step + 1) * tq)
        for h in range(N_HEADS):
            q2_sc[slot, h] = _stack_maps(q_ref[q_rows, _head_cols(h)])
        m_sc[slot] = jnp.full(m_sc.shape[1:], NEG_INF, F32)
        acc_sc[slot] = jnp.zeros(acc_sc.shape[1:], F32)
        n_far = max(i - 1, 0)
        for j in range(n_far // 2):
            key_tile(slot, j * 2 * tq, 2 * tq, None)
        if n_far % 2:
            key_tile(slot, (n_far - 1) * tq, tq, None)
        if i >= 1:
            key_tile(slot, (i - 1) * tq, 2 * tq, lambda h: bt_ref[h])
        else:
            key_tile(slot, 0, tq, lambda h: bt_ref[h, :, tq:])
        lam = _lambda(lq1_ref, lk1_ref, lq2_ref, lk2_ref)
        for h in range(N_HEADS):
            o = _finish_head(acc_sc[slot, h], lam, g_ref[...], tq)
            o_ref[q_rows, _head_cols(h)] = o.astype(o_ref.dtype)

    def step(first_tile):
        for i in range(first_tile, first_tile + tiles_per_step):
            query_tile(i)

    for c in range(k_ref.shape[0] // q_ref.shape[0]):
        pl.when(pl.program_id(1) == c)(functools.partial(step, c * tiles_per_step))


def _prompt_attn(q, kb, vb, bias_tail, lam_params, subln_g, batch, seq, tq, tiles_per_step):
    steps = seq // (tq * tiles_per_step)
    rows = tq * tiles_per_step
    small = [_resident(p.shape) for p in lam_params]
    return pl.pallas_call(
        functools.partial(_prompt_attn_kernel, tq=tq),
        grid=(batch, steps),
        in_specs=[pl.BlockSpec((rows, QK_WIDTH), lambda b, i: (b * steps + i, 0)),
                  pl.BlockSpec((seq, QK_WIDTH), lambda b, i: (b, 0)),
                  pl.BlockSpec((seq, WIDTH_A), lambda b, i: (b, 0)),
                  _resident(bias_tail.shape), *small, _resident(subln_g.shape)],
        out_specs=pl.BlockSpec((rows, WIDTH_A), lambda b, i: (b * steps + i, 0)),
        out_shape=jax.ShapeDtypeStruct((batch * seq, WIDTH_A), BF16),
        scratch_shapes=[pltpu.VMEM((2, N_HEADS, 2 * tq, V_DIM), BF16), pltpu.VMEM((2, N_HEADS, 2 * tq, LANES), F32),
                        pltpu.VMEM((2, N_HEADS, 2 * tq, 2 * V_DIM), F32)],
        compiler_params=_params("parallel", "arbitrary"),
        name="prompt_attn",
    )(q, kb, vb, bias_tail, *lam_params, subln_g)


def _sample_attn_kernel(q_ref, ck_ref, cv_ref, kn_ref, vn_ref, bp_ref, bn_ref, lq1_ref, lk1_ref, lq2_ref, lk2_ref,
                        g_ref, o_ref, *, past_len):
    t = q_ref.shape[0]
    lam = _lambda(lq1_ref, lk1_ref, lq2_ref, lk2_ref)
    for h in range(N_HEADS):
        hs = _head_cols(h)
        q2 = _stack_maps(q_ref[:, hs])
        bias_past, bias_new = bp_ref[h], bn_ref[h]
        k_past = ck_ref[0, _head_rows(h, past_len), :].astype(BF16)
        s_past = lax.dot_general(q2, k_past, _NT, preferred_element_type=F32)
        s_past = s_past + jnp.concatenate([bias_past, bias_past], axis=0)
        s_new = lax.dot_general(q2, kn_ref[:, hs], _NT, preferred_element_type=F32)
        s_new = s_new + jnp.concatenate([bias_new, bias_new], axis=0)
        m = jnp.maximum(jnp.max(s_past, axis=1, keepdims=True), jnp.max(s_new, axis=1, keepdims=True))
        p_past = jnp.exp(s_past - m).astype(BF16)
        p_new = jnp.exp(s_new - m).astype(BF16)
        v_past = cv_ref[0, _head_rows(h, past_len), :].astype(BF16)
        acc = (jnp.dot(p_past, _with_ones(v_past), preferred_element_type=F32)
               + jnp.dot(p_new, _with_ones(vn_ref[:, hs]), preferred_element_type=F32))
        o_ref[:, hs] = _finish_head(acc, lam, g_ref[...], t).astype(o_ref.dtype)


def _sample_attn(q, cache_k, cache_v, kb, vb, bias_past, bias_new, lam_params, subln_g, past_len):
    batch = cache_k.shape[0]
    t = q.shape[0] // batch
    small = [_resident(p.shape) for p in lam_params]
    new_rows = pl.BlockSpec((t, QK_WIDTH), lambda b: (b, 0))
    cached = pl.BlockSpec((1, past_len * N_HEADS, V_DIM), lambda b: (b, 0, 0))
    return pl.pallas_call(
        functools.partial(_sample_attn_kernel, past_len=past_len),
        grid=(batch,),
        in_specs=[new_rows, cached, cached, new_rows, new_rows,
                  _resident(bias_past.shape), _resident(bias_new.shape), *small, _resident(subln_g.shape)],
        out_specs=new_rows,
        out_shape=jax.ShapeDtypeStruct(q.shape, BF16),
        compiler_params=_params("parallel"),
        name="sample_attn",
    )(q, cache_k, cache_v, kb, vb, bias_past, bias_new, *lam_params, subln_g)


def _conv_kernel(pre_ref, halo_ref, u_ref, wb_ref, b_ref, g_ref, beta_ref, c_ref, ubuf, ybuf, *, tt, rc):
    n_groups = WIDTH_B // LANES
    n_streams = u_ref.shape[0]
    n_chunks = n_streams * (tt // rc)
    for s in range(n_streams):
        history = jnp.where(pl.program_id(1) == 0, pre_ref[s], halo_ref[s])
        for k in range(tt // rc):
            for lg in range(n_groups):
                ls = slice(lg * LANES, (lg + 1) * LANES)
                slab = (s * (tt // rc) + k) * n_groups + lg
                ubuf[slab, 0:HALO, :] = history[:, ls] if k == 0 else u_ref[s, k * rc - HALO:k * rc, ls]
                ubuf[slab, HALO:, :] = u_ref[s, k * rc:(k + 1) * rc, ls]

    def conv_slab(slab, carry):
        lg = slab % n_groups
        acc = None
        for j in range(CONV_WIDTH):
            first = HALO - CONV_PAD + j
            window = ubuf[slab, first:first + rc, :].reshape(rc // SUBLANES, SUBLANES, LANES)
            term = window * wb_ref[lg, j]
            acc = term if acc is None else acc + term
        ybuf[slab] = acc.reshape(rc, LANES)
        return carry

    lax.fori_loop(0, n_chunks * n_groups, conv_slab, 0)

    for c in range(n_chunks):
        s, k = divmod(c, tt // rc)
        y = jnp.concatenate([ybuf[c * n_groups + lg] for lg in range(n_groups)], axis=1) + b_ref[...]
        mu = jnp.mean(y, axis=-1, keepdims=True)
        var = jnp.mean(jnp.square(y - mu), axis=-1, keepdims=True)
        yn = (y - mu) * lax.rsqrt(var + EPS) * g_ref[...] + beta_ref[...]
        c_ref[s, k * rc:(k + 1) * rc, :] = (yn * _sigmoid(yn)).astype(c_ref.dtype)


def _conv(prefix, u, w_rows, b_dw, ln_g, ln_b, streams, tt, rc):
    batch, seq, _ = u.shape
    halo_blocks = tt // HALO
    slabs = streams * (tt // rc) * (WIDTH_B // LANES)
    small = [_resident(p.shape) for p in (w_rows, b_dw, ln_g, ln_b)]
    return pl.pallas_call(
        functools.partial(_conv_kernel, tt=tt, rc=rc),
        grid=(batch // streams, seq // tt),
        in_specs=[pl.BlockSpec((streams, HALO, WIDTH_B), lambda b, t: (b, 0, 0)),
                  pl.BlockSpec((streams, HALO, WIDTH_B), lambda b, t: (b, jnp.maximum(t * halo_blocks - 1, 0), 0)),
                  pl.BlockSpec((streams, tt, WIDTH_B), lambda b, t: (b, t, 0)), *small],
        out_specs=pl.BlockSpec((streams, tt, WIDTH_B), lambda b, t: (b, t, 0)),
        out_shape=jax.ShapeDtypeStruct(u.shape, BF16),
        scratch_shapes=[pltpu.VMEM((slabs, HALO + rc, LANES), F32), pltpu.VMEM((slabs, rc, LANES), F32)],
        compiler_params=_params("parallel", "parallel"),
        name="conv_ln_swish",
    )(prefix, u, u, w_rows, b_dw, ln_g, ln_b)


def _out_ffn_kernel(x_ref, a_ref, c_ref, wo_ref, g2_ref, wu_ref, wd_ref, gf_ref, y_ref, *, ff_chunk):
    x = (x_ref[...]
         + jnp.dot(a_ref[...], wo_ref[:WIDTH_A, :], preferred_element_type=F32)
         + jnp.dot(c_ref[...], wo_ref[WIDTH_A:, :], preferred_element_type=F32))
    h2 = _rms_norm_rows(x, g2_ref[...]).astype(BF16)
    ffn = None
    for lo in range(0, D_FF, ff_chunk):
        hid = jnp.dot(h2, wu_ref[:, lo:lo + ff_chunk], preferred_element_type=F32)
        hid = jnp.square(jnp.maximum(hid, 0.0)).astype(BF16)
        down = jnp.dot(hid, wd_ref[lo:lo + ff_chunk, :], preferred_element_type=F32)
        ffn = down if ffn is None else ffn + down
    y_ref[...] = _rms_norm_rows(x + ffn, gf_ref[...])


def _out_ffn(x, a, c, w_out_bf, ln2_g, w_up_bf, w_down_bf, ln_f_g, tm, ff_chunk):
    n = x.shape[0]
    rows = lambda width: pl.BlockSpec((tm, width), lambda i: (i, 0))
    return pl.pallas_call(
        functools.partial(_out_ffn_kernel, ff_chunk=ff_chunk),
        grid=(n // tm,),
        in_specs=[rows(D_MODEL), rows(WIDTH_A), rows(WIDTH_B), _resident(w_out_bf.shape), _resident(ln2_g.shape),
                  _resident(w_up_bf.shape), _resident(w_down_bf.shape), _resident(ln_f_g.shape)],
        out_specs=rows(D_MODEL),
        out_shape=jax.ShapeDtypeStruct(x.shape, F32),
        compiler_params=_params("parallel"),
        name="out_ffn",
    )(x, a, c, w_out_bf, ln2_g, w_up_bf, w_down_bf, ln_f_g)


def _bucket_of(rel):
    half = N_BUCKETS // 2
    max_exact = half // 2
    n = -rel
    ret = jnp.where(n < 0, half, 0)
    n = jnp.abs(n)
    nf = jnp.maximum(n, 1).astype(F32)
    large = max_exact + (jnp.log(nf / max_exact) / math.log(MAX_DISTANCE / max_exact)
                         * (half - max_exact)).astype(jnp.int32)
    large = jnp.minimum(large, half - 1)
    return ret + jnp.where(n < max_exact, n, large)


def _bias_table(rel_bias, q0, nq, k0, nk, far_rel):
    span = nq + nk - 1
    shifted = (rel_bias - rel_bias[_bucket_of(far_rel)]).T
    hit = _bucket_of(k0 - (q0 + nq - 1) + jnp.arange(span))[:, None] == jnp.arange(N_BUCKETS)
    by_rel = jnp.sum(jnp.where(hit[None], shifted[:, None, :], 0.0), axis=-1)
    rows = jnp.pad(by_rel, ((0, 0), (0, 1)))
    rows = jnp.broadcast_to(rows[:, None, :], (N_HEADS, nq, span + 1)).reshape(N_HEADS, nq * (span + 1))
    table = rows[:, :nq * span].reshape(N_HEADS, nq, span)[:, :, nq - 1:]
    q_pos, k_pos = q0 + jnp.arange(nq), k0 + jnp.arange(nk)
    visible = (k_pos[None, :] // CHUNK) <= (q_pos[:, None] // CHUNK)
    return jnp.where(visible[None], table, NEG_INF).astype(F32)


def kernel(x_prompt, x_sample, cache_k, cache_v, state_conv, rel_bias, ln1_g, w_in, lambda_q1, lambda_k1, lambda_q2,
           lambda_k2, subln_g, w_dw, b_dw, conv_ln_g, conv_ln_b, w_out, ln2_g, w_up, w_down, ln_f_g):
    batch, seq, _ = x_prompt.shape
    dec_batch, dec_seq, _ = x_sample.shape
    past_len = cache_k.shape[2]
    tq = 256
    assert w_in.shape[0] == 1 and seq % (2 * tq) == 0 and seq >= CONV_PAD and dec_seq >= CONV_PAD

    w_in_bf, w_out_bf = w_in[0].astype(BF16), w_out[0].astype(BF16)
    w_up_bf, w_down_bf = w_up[0].astype(BF16), w_down[0].astype(BF16)
    w_taps = w_dw[0].reshape(CONV_WIDTH, WIDTH_B // LANES, 1, LANES).transpose(1, 0, 2, 3)
    w_rows = jnp.broadcast_to(w_taps, (WIDTH_B // LANES, CONV_WIDTH, SUBLANES, LANES))
    lam_params = (lambda_q1, lambda_k1, lambda_q2, lambda_k2)

    far_rel = jnp.int32(-(2 * tq))
    bias_tail = _bias_table(rel_bias, tq, tq, 0, 2 * tq, far_rel)
    bias_past = _bias_table(rel_bias, past_len, dec_seq, 0, past_len, far_rel)
    bias_new = _bias_table(rel_bias, past_len, dec_seq, past_len, dec_seq, far_rel)

    xp = x_prompt.reshape(batch * seq, D_MODEL)
    q_p, k_p, v_p, kb_p, vb_p, u_p = _in_proj(xp, ln1_g, w_in_bf, tm=1024)
    a_p = _prompt_attn(q_p, kb_p, vb_p, bias_tail, lam_params, subln_g, batch, seq, tq, tiles_per_step=2)
    u_p3 = u_p.reshape(batch, seq, WIDTH_B)
    c_p = _conv(jnp.zeros((batch, HALO, WIDTH_B), F32), u_p3, w_rows, b_dw, conv_ln_g, conv_ln_b,
                streams=1, tt=512, rc=128)
    y_p = _out_ffn(xp, a_p, c_p.reshape(batch * seq, WIDTH_B), w_out_bf, ln2_g, w_up_bf, w_down_bf,
                   ln_f_g[None, :], tm=1024, ff_chunk=1024)

    xs = x_sample.reshape(dec_batch * dec_seq, D_MODEL)
    q_s, k_s, v_s, kb_s, vb_s, u_s = _in_proj(xs, ln1_g, w_in_bf, tm=512)
    by_head = lambda cache: cache[0].reshape(dec_batch, past_len * N_HEADS, V_DIM)
    a_s = _sample_attn(q_s, by_head(cache_k), by_head(cache_v), kb_s, vb_s, bias_past, bias_new,
                       lam_params, subln_g, past_len)
    u_s3 = u_s.reshape(dec_batch, dec_seq, WIDTH_B)
    prefix_s = jnp.pad(state_conv[0], ((0, 0), (HALO - CONV_PAD, 0), (0, 0)))
    c_s = _conv(prefix_s, u_s3, w_rows, b_dw, conv_ln_g, conv_ln_b, streams=8, tt=dec_seq, rc=dec_seq)
    y_s = _out_ffn(xs, a_s, c_s.reshape(dec_batch * dec_seq, WIDTH_B), w_out_bf, ln2_g, w_up_bf, w_down_bf,
                   ln_f_g[None, :], tm=512, ff_chunk=1024)

    heads = lambda t, b, s: t.reshape(1, b, s, N_HEADS, V_DIM)
    return (y_p.reshape(batch, seq, D_MODEL), y_s.reshape(dec_batch, dec_seq, D_MODEL),
            heads(k_p, batch, seq), heads(v_p, batch, seq), u_p3[None, :, seq - CONV_PAD:],
            heads(k_s, dec_batch, dec_seq), heads(v_s, dec_batch, dec_seq), u_s3[None, :, dec_seq - CONV_PAD:])
```

```python
import functools
import math

import jax
import jax.numpy as jnp
from jax import lax
from jax.experimental import pallas as pl
from jax.experimental.pallas import tpu as pltpu

D_MODEL = 1024
CHUNK = 64
N_HEADS = 4
HEAD_DIM = 64
V_DIM = 2 * HEAD_DIM
QK_WIDTH = N_HEADS * 2 * HEAD_DIM
WIDTH_A = N_HEADS * V_DIM
WIDTH_B = D_MODEL - WIDTH_A
CONV_WIDTH = 31
CONV_PAD = CONV_WIDTH - 1
D_FF = 4 * D_MODEL
N_BUCKETS = 32
MAX_DISTANCE = 128
EPS = 1e-6
NEG_INF = -1e30
LAM_INIT = 0.8 - 0.6 * math.exp(-0.3 * 0)
LOG2E = math.log2(math.e)

LANES = 128
SUBLANES = 8
HALO = 32
VMEM_LIMIT_BYTES = 56 * 1024 * 1024

F32 = jnp.float32
BF16 = jnp.bfloat16
_NT = (((1,), (1,)), ((), ()))


def _sigmoid(x):
    return 1.0 / (1.0 + jnp.exp(-x))


def _rms_norm_rows(x, g):
    return x * lax.rsqrt(jnp.mean(x * x, axis=-1, keepdims=True) + EPS) * g


def _params(*semantics):
    return pltpu.CompilerParams(dimension_semantics=semantics, vmem_limit_bytes=VMEM_LIMIT_BYTES)


def _resident(shape):
    zeros = (0,) * len(shape)
    return pl.BlockSpec(shape, lambda *_: zeros, pipeline_mode=pl.Buffered(1))


def _head_cols(h):
    return slice(h * V_DIM, (h + 1) * V_DIM)


def _head_rows(h, n):
    return pl.ds(h, n, stride=N_HEADS)


def _in_proj_kernel(x_ref, g_ref, w_ref, q_ref, k_ref, v_ref, kb_ref, vb_ref, u_ref):
    tm = x_ref.shape[0]
    h = _rms_norm_rows(x_ref[...], g_ref[...]).astype(BF16)

    def cols(lo, hi):
        return jnp.dot(h, w_ref[:, lo:hi], preferred_element_type=F32)

    a_cols = 2 * QK_WIDTH + WIDTH_A
    q_ref[...] = (cols(0, QK_WIDTH) * (HEAD_DIM ** -0.5 * LOG2E)).astype(BF16)
    k, v = cols(QK_WIDTH, 2 * QK_WIDTH), cols(2 * QK_WIDTH, a_cols)
    for full, out_ref, bf_ref in ((k, k_ref, kb_ref), (v, v_ref, vb_ref)):
        bf_ref[...] = full.astype(BF16)
        for hd in range(N_HEADS):
            out_ref[_head_rows(hd, tm), :] = full[:, _head_cols(hd)]
    u_ref[...] = cols(a_cols, a_cols + WIDTH_B) * _sigmoid(cols(a_cols + WIDTH_B, a_cols + 2 * WIDTH_B))


def _in_proj(x, ln_g, w_in_bf, tm):
    n = x.shape[0]
    rows = lambda width: pl.BlockSpec((tm, width), lambda i: (i, 0))
    by_head = pl.BlockSpec((tm * N_HEADS, V_DIM), lambda i: (i, 0))
    wide = lambda dt: jax.ShapeDtypeStruct((n, QK_WIDTH), dt)
    tall = jax.ShapeDtypeStruct((n * N_HEADS, V_DIM), F32)
    return pl.pallas_call(
        _in_proj_kernel,
        grid=(n // tm,),
        in_specs=[rows(D_MODEL), _resident((1, D_MODEL)), _resident(w_in_bf.shape)],
        out_specs=[rows(QK_WIDTH), by_head, by_head, rows(QK_WIDTH), rows(WIDTH_A), rows(WIDTH_B)],
        out_shape=[wide(BF16), tall, tall, wide(BF16), wide(BF16), wide(F32)],
        compiler_params=_params("parallel"),
        name="in_proj",
    )(x, ln_g, w_in_bf)


def _lambda(lq1_ref, lk1_ref, lq2_ref, lk2_ref):
    d1 = jnp.sum(lq1_ref[...] * lk1_ref[...], axis=-1, keepdims=True)
    d2 = jnp.sum(lq2_ref[...] * lk2_ref[...], axis=-1, keepdims=True)
    return jnp.exp(d1) - jnp.exp(d2) + LAM_INIT


def _stack_maps(qh):
    lane = lax.broadcasted_iota(jnp.int32, qh.shape, 1)
    zero = jnp.zeros_like(qh)
    return jnp.concatenate([jnp.where(lane < HEAD_DIM, qh, zero), jnp.where(lane >= HEAD_DIM, qh, zero)], axis=0)


def _with_ones(v):
    return jnp.concatenate([v, jnp.ones_like(v)], axis=1)


def _finish_head(acc, lam, g, tq):
    o1 = acc[:tq, :V_DIM] / acc[:tq, V_DIM:]
    o2 = acc[tq:, :V_DIM] / acc[tq:, V_DIM:]
    o = o1 - lam * o2
    return _rms_norm_rows(o, g) * (1.0 - LAM_INIT)


def _prompt_attn_kernel(q_ref, k_ref, v_ref, bt_ref, lq1_ref, lk1_ref, lq2_ref, lk2_ref, g_ref,
                        o_ref, q2_sc, m_sc, acc_sc, *, tq):
    tiles_per_step = q_ref.shape[0] // tq

    def key_tile(slot, start, tk, bias_of_head):
        for h in range(N_HEADS):
            hs = _head_cols(h)
            k_tile, v_tile = k_ref[start:start + tk, hs], _with_ones(v_ref[start:start + tk, hs])
            for rows in (slice(0, tq), slice(tq, 2 * tq)):
                s = lax.dot_general(q2_sc[slot, h, rows], k_tile, _NT, preferred_element_type=F32)
                if bias_of_head is not None:
                    s = s + bias_of_head(h)
                m_prev = m_sc[slot, h, rows]
                m_new = jnp.maximum(m_prev, jnp.max(s, axis=1, keepdims=True))
                alpha = jnp.exp2(m_prev - m_new)
                p = jnp.exp2(s - jnp.concatenate([m_new] * (tk // LANES), axis=1))
                pv = jnp.dot(p.astype(BF16), v_tile, preferred_element_type=F32)
                acc_sc[slot, h, rows] = jnp.concatenate([alpha, alpha], axis=1) * acc_sc[slot, h, rows] + pv
                m_sc[slot, h, rows] = m_new

    def query_tile(i):
        slot = i % 2
        q_rows = slice((i % tiles_per_step) * tq, (i % tiles_per_step + 1) * tq)
        for h in range(N_HEADS):
            q2_sc[slot, h] = _stack_maps(q_ref[q_rows, _head_cols(h)])
        m_sc[slot] = jnp.full(m_sc.shape[1:], NEG_INF, F32)
        acc_sc[slot] = jnp.zeros(acc_sc.shape[1:], F32)
        n_far = max(i - 1, 0)
        for j in range(n_far // 2):
            key_tile(slot, j * 2 * tq, 2 * tq, None)
        if n_far % 2:
            key_tile(slot, (n_far - 1) * tq, tq, None)
        if i >= 1:
            key_tile(slot, (i - 1) * tq, 2 * tq, lambda h: bt_ref[h])
        else:
            key_tile(slot, 0, tq, lambda h: bt_ref[h, :, tq:])
        lam = _lambda(lq1_ref, lk1_ref, lq2_ref, lk2_ref)
        for h in range(N_HEADS):
            o = _finish_head(acc_sc[slot, h], lam, g_ref[...], tq)
            o_ref[q_rows, _head_cols(h)] = o.astype(o_ref.dtype)

    def step(first_tile):
        for i in range(first_tile, first_tile + tiles_per_step):
            query_tile(i)

    for c in range(k_ref.shape[0] // q_ref.shape[0]):
        pl.when(pl.program_id(1) == c)(functools.partial(step, c * tiles_per_step))


def _prompt_attn(q, kb, vb, bias_tail, lam_params, subln_g, batch, seq, tq, tiles_per_step):
    steps = seq // (tq * tiles_per_step)
    rows = tq * tiles_per_step
    small = [_resident(p.shape) for p in lam_params]
    return pl.pallas_call(
        functools.partial(_prompt_attn_kernel, tq=tq),
        grid=(batch, steps),
        in_specs=[pl.BlockSpec((rows, QK_WIDTH), lambda b, i: (b * steps + i, 0)),
                  pl.BlockSpec((seq, QK_WIDTH), lambda b, i: (b, 0)),
                  pl.BlockSpec((seq, WIDTH_A), lambda b, i: (b, 0)),
                  _resident(bias_tail.shape), *small, _resident(subln_g.shape)],
        out_specs=pl.BlockSpec((rows, WIDTH_A), lambda b, i: (b * steps + i, 0)),
        out_shape=jax.ShapeDtypeStruct((batch * seq, WIDTH_A), BF16),
        scratch_shapes=[pltpu.VMEM((2, N_HEADS, 2 * tq, V_DIM), BF16), pltpu.VMEM((2, N_HEADS, 2 * tq, LANES), F32),
                        pltpu.VMEM((2, N_HEADS, 2 * tq, 2 * V_DIM), F32)],
        compiler_params=_params("parallel", "arbitrary"),
        name="prompt_attn",
    )(q, kb, vb, bias_tail, *lam_params, subln_g)


def _sample_attn_kernel(q_ref, ck_ref, cv_ref, kn_ref, vn_ref, bp_ref, bn_ref, lq1_ref, lk1_ref, lq2_ref, lk2_ref,
                        g_ref, o_ref, *, past_len):
    t = q_ref.shape[0]
    lam = _lambda(lq1_ref, lk1_ref, lq2_ref, lk2_ref)
    for h in range(N_HEADS):
        hs = _head_cols(h)
        q2 = _stack_maps(q_ref[:, hs])
        bias_past, bias_new = bp_ref[h], bn_ref[h]
        k_past = ck_ref[0, _head_rows(h, past_len), :].astype(BF16)
        s_past = lax.dot_general(q2, k_past, _NT, preferred_element_type=F32)
        s_past = s_past + jnp.concatenate([bias_past, bias_past], axis=0)
        s_new = lax.dot_general(q2, kn_ref[:, hs], _NT, preferred_element_type=F32)
        s_new = s_new + jnp.concatenate([bias_new, bias_new], axis=0)
        m = jnp.maximum(jnp.max(s_past, axis=1, keepdims=True), jnp.max(s_new, axis=1, keepdims=True))
        p_past = jnp.exp2(s_past - m).astype(BF16)
        p_new = jnp.exp2(s_new - m).astype(BF16)
        v_past = cv_ref[0, _head_rows(h, past_len), :].astype(BF16)
        acc = (jnp.dot(p_past, _with_ones(v_past), preferred_element_type=F32)
               + jnp.dot(p_new, _with_ones(vn_ref[:, hs]), preferred_element_type=F32))
        o_ref[:, hs] = _finish_head(acc, lam, g_ref[...], t).astype(o_ref.dtype)


def _sample_attn(q, cache_k, cache_v, kb, vb, bias_past, bias_new, lam_params, subln_g, past_len):
    batch = cache_k.shape[0]
    t = q.shape[0] // batch
    small = [_resident(p.shape) for p in lam_params]
    new_rows = pl.BlockSpec((t, QK_WIDTH), lambda b: (b, 0))
    cached = pl.BlockSpec((1, past_len * N_HEADS, V_DIM), lambda b: (b, 0, 0))
    return pl.pallas_call(
        functools.partial(_sample_attn_kernel, past_len=past_len),
        grid=(batch,),
        in_specs=[new_rows, cached, cached, new_rows, new_rows,
                  _resident(bias_past.shape), _resident(bias_new.shape), *small, _resident(subln_g.shape)],
        out_specs=new_rows,
        out_shape=jax.ShapeDtypeStruct(q.shape, BF16),
        compiler_params=_params("parallel"),
        name="sample_attn",
    )(q, cache_k, cache_v, kb, vb, bias_past, bias_new, *lam_params, subln_g)


def _conv_kernel(pre_ref, halo_ref, u_ref, wb_ref, b_ref, g_ref, beta_ref, c_ref, ubuf, ybuf, *, tt, rc):
    n_groups = WIDTH_B // LANES
    n_streams = u_ref.shape[0]
    n_chunks = n_streams * (tt // rc)
    for s in range(n_streams):
        history = jnp.where(pl.program_id(1) == 0, pre_ref[s], halo_ref[s])
        for k in range(tt // rc):
            for lg in range(n_groups):
                ls = slice(lg * LANES, (lg + 1) * LANES)
                slab = (s * (tt // rc) + k) * n_groups + lg
                ubuf[slab, 0:HALO, :] = history[:, ls] if k == 0 else u_ref[s, k * rc - HALO:k * rc, ls]
                ubuf[slab, HALO:, :] = u_ref[s, k * rc:(k + 1) * rc, ls]

    def conv_slab(slab, carry):
        lg = slab % n_groups
        acc = None
        for j in range(CONV_WIDTH):
            first = HALO - CONV_PAD + j
            window = ubuf[slab, first:first + rc, :].reshape(rc // SUBLANES, SUBLANES, LANES)
            term = window * wb_ref[lg, j]
            acc = term if acc is None else acc + term
        ybuf[slab] = acc.reshape(rc, LANES)
        return carry

    lax.fori_loop(0, n_chunks * n_groups, conv_slab, 0)

    for c in range(n_chunks):
        s, k = divmod(c, tt // rc)
        y = jnp.concatenate([ybuf[c * n_groups + lg] for lg in range(n_groups)], axis=1) + b_ref[...]
        mu = jnp.mean(y, axis=-1, keepdims=True)
        var = jnp.mean(jnp.square(y - mu), axis=-1, keepdims=True)
        yn = (y - mu) * lax.rsqrt(var + EPS) * g_ref[...] + beta_ref[...]
        c_ref[s, k * rc:(k + 1) * rc, :] = (yn * _sigmoid(yn)).astype(c_ref.dtype)


def _conv(prefix, u, w_rows, b_dw, ln_g, ln_b, streams, tt, rc):
    batch, seq, _ = u.shape
    halo_blocks = tt // HALO
    slabs = streams * (tt // rc) * (WIDTH_B // LANES)
    small = [_resident(p.shape) for p in (w_rows, b_dw, ln_g, ln_b)]
    return pl.pallas_call(
        functools.partial(_conv_kernel, tt=tt, rc=rc),
        grid=(batch // streams, seq // tt),
        in_specs=[pl.BlockSpec((streams, HALO, WIDTH_B), lambda b, t: (b, 0, 0)),
                  pl.BlockSpec((streams, HALO, WIDTH_B), lambda b, t: (b, jnp.maximum(t * halo_blocks - 1, 0), 0)),
                  pl.BlockSpec((streams, tt, WIDTH_B), lambda b, t: (b, t, 0)), *small],
        out_specs=pl.BlockSpec((streams, tt, WIDTH_B), lambda b, t: (b, t, 0)),
        out_shape=jax.ShapeDtypeStruct(u.shape, BF16),
        scratch_shapes=[pltpu.VMEM((slabs, HALO + rc, LANES), F32), pltpu.VMEM((slabs, rc, LANES), F32)],
        compiler_params=_params("parallel", "parallel"),
        name="conv_ln_swish",
    )(prefix, u, u, w_rows, b_dw, ln_g, ln_b)


def _out_ffn_kernel(x_ref, a_ref, c_ref, wo_ref, g2_ref, wu_ref, wd_ref, gf_ref, y_ref, *, ff_chunk):
    x = (x_ref[...]
         + jnp.dot(a_ref[...], wo_ref[:WIDTH_A, :], preferred_element_type=F32)
         + jnp.dot(c_ref[...], wo_ref[WIDTH_A:, :], preferred_element_type=F32))
    h2 = _rms_norm_rows(x, g2_ref[...]).astype(BF16)
    ffn = None
    for lo in range(0, D_FF, ff_chunk):
        hid = jnp.dot(h2, wu_ref[:, lo:lo + ff_chunk], preferred_element_type=F32)
        hid = jnp.square(jnp.maximum(hid, 0.0)).astype(BF16)
        down = jnp.dot(hid, wd_ref[lo:lo + ff_chunk, :], preferred_element_type=F32)
        ffn = down if ffn is None else ffn + down
    y_ref[...] = _rms_norm_rows(x + ffn, gf_ref[...])


def _out_ffn(x, a, c, w_out_bf, ln2_g, w_up_bf, w_down_bf, ln_f_g, tm, ff_chunk):
    n = x.shape[0]
    rows = lambda width: pl.BlockSpec((tm, width), lambda i: (i, 0))
    return pl.pallas_call(
        functools.partial(_out_ffn_kernel, ff_chunk=ff_chunk),
        grid=(n // tm,),
        in_specs=[rows(D_MODEL), rows(WIDTH_A), rows(WIDTH_B), _resident(w_out_bf.shape), _resident(ln2_g.shape),
                  _resident(w_up_bf.shape), _resident(w_down_bf.shape), _resident(ln_f_g.shape)],
        out_specs=rows(D_MODEL),
        out_shape=jax.ShapeDtypeStruct(x.shape, F32),
        compiler_params=_params("parallel"),
        name="out_ffn",
    )(x, a, c, w_out_bf, ln2_g, w_up_bf, w_down_bf, ln_f_g)


def _bucket_of(rel):
    half = N_BUCKETS // 2
    max_exact = half // 2
    n = -rel
    ret = jnp.where(n < 0, half, 0)
    n = jnp.abs(n)
    nf = jnp.maximum(n, 1).astype(F32)
    large = max_exact + (jnp.log(nf / max_exact) / math.log(MAX_DISTANCE / max_exact)
                         * (half - max_exact)).astype(jnp.int32)
    large = jnp.minimum(large, half - 1)
    return ret + jnp.where(n < max_exact, n, large)


def _bias_table(rel_bias, q0, nq, k0, nk, far_rel):
    span = nq + nk - 1
    shifted = (rel_bias - rel_bias[_bucket_of(far_rel)]).T
    hit = _bucket_of(k0 - (q0 + nq - 1) + jnp.arange(span))[:, None] == jnp.arange(N_BUCKETS)
    by_rel = jnp.sum(jnp.where(hit[None], shifted[:, None, :], 0.0), axis=-1)
    rows = jnp.pad(by_rel, ((0, 0), (0, 1)))
    rows = jnp.broadcast_to(rows[:, None, :], (N_HEADS, nq, span + 1)).reshape(N_HEADS, nq * (span + 1))
    table = rows[:, :nq * span].reshape(N_HEADS, nq, span)[:, :, nq - 1:]
    q_pos, k_pos = q0 + jnp.arange(nq), k0 + jnp.arange(nk)
    visible = (k_pos[None, :] // CHUNK) <= (q_pos[:, None] // CHUNK)
    return jnp.where(visible[None], table * LOG2E, NEG_INF).astype(F32)


def kernel(x_prompt, x_sample, cache_k, cache_v, state_conv, rel_bias, ln1_g, w_in, lambda_q1, lambda_k1, lambda_q2,
           lambda_k2, subln_g, w_dw, b_dw, conv_ln_g, conv_ln_b, w_out, ln2_g, w_up, w_down, ln_f_g):
    batch, seq, _ = x_prompt.shape
    dec_batch, dec_seq, _ = x_sample.shape
    past_len = cache_k.shape[2]
    tq = 256
    assert w_in.shape[0] == 1 and seq % (2 * tq) == 0 and seq >= CONV_PAD and dec_seq >= CONV_PAD

    w_in_bf, w_out_bf = w_in[0].astype(BF16), w_out[0].astype(BF16)
    w_up_bf, w_down_bf = w_up[0].astype(BF16), w_down[0].astype(BF16)
    w_taps = w_dw[0].reshape(CONV_WIDTH, WIDTH_B // LANES, 1, LANES).transpose(1, 0, 2, 3)
    w_rows = jnp.broadcast_to(w_taps, (WIDTH_B // LANES, CONV_WIDTH, SUBLANES, LANES))
    lam_params = (lambda_q1, lambda_k1, lambda_q2, lambda_k2)

    far_rel = jnp.int32(-(2 * tq))
    bias_tail = _bias_table(rel_bias, tq, tq, 0, 2 * tq, far_rel)
    bias_past = _bias_table(rel_bias, past_len, dec_seq, 0, past_len, far_rel)
    bias_new = _bias_table(rel_bias, past_len, dec_seq, past_len, dec_seq, far_rel)

    xp = x_prompt.reshape(batch * seq, D_MODEL)
    q_p, k_p, v_p, kb_p, vb_p, u_p = _in_proj(xp, ln1_g, w_in_bf, tm=1024)
    a_p = _prompt_attn(q_p, kb_p, vb_p, bias_tail, lam_params, subln_g, batch, seq, tq, tiles_per_step=2)
    u_p3 = u_p.reshape(batch, seq, WIDTH_B)
    c_p = _conv(jnp.zeros((batch, HALO, WIDTH_B), F32), u_p3, w_rows, b_dw, conv_ln_g, conv_ln_b,
                streams=1, tt=512, rc=128)
    y_p = _out_ffn(xp, a_p, c_p.reshape(batch * seq, WIDTH_B), w_out_bf, ln2_g, w_up_bf, w_down_bf,
                   ln_f_g[None, :], tm=1024, ff_chunk=1024)

    xs = x_sample.reshape(dec_batch * dec_seq, D_MODEL)
    q_s, k_s, v_s, kb_s, vb_s, u_s = _in_proj(xs, ln1_g, w_in_bf, tm=512)
    by_head = lambda cache: cache[0].reshape(dec_batch, past_len * N_HEADS, V_DIM)
    a_s = _sample_attn(q_s, by_head(cache_k), by_head(cache_v), kb_s, vb_s, bias_past, bias_new,
                       lam_params, subln_g, past_len)
    u_s3 = u_s.reshape(dec_batch, dec_seq, WIDTH_B)
    prefix_s = jnp.pad(state_conv[0], ((0, 0), (HALO - CONV_PAD, 0), (0, 0)))
    c_s = _conv(prefix_s, u_s3, w_rows, b_dw, conv_ln_g, conv_ln_b, streams=8, tt=dec_seq, rc=dec_seq)
    y_s = _out_ffn(xs, a_s, c_s.reshape(dec_batch * dec_seq, WIDTH_B), w_out_bf, ln2_g, w_up_bf, w_down_bf,
                   ln_f_g[None, :], tm=512, ff_chunk=1024)

    heads = lambda t, b, s: t.reshape(1, b, s, N_HEADS, V_DIM)
    return (y_p.reshape(batch, seq, D_MODEL), y_s.reshape(dec_batch, dec_seq, D_MODEL),
            heads(k_p, batch, seq), heads(v_p, batch, seq), u_p3[None, :, seq - CONV_PAD:],
            heads(k_s, dec_batch, dec_seq), heads(v_s, dec_batch, dec_seq), u_s3[None, :, dec_seq - CONV_PAD:])
```

```python
import functools
import math

import jax
import jax.numpy as jnp
from jax import lax
from jax.experimental import pallas as pl
from jax.experimental.pallas import tpu as pltpu

D_MODEL = 1024
CHUNK = 64
N_HEADS = 4
HEAD_DIM = 64
V_DIM = 2 * HEAD_DIM
QK_WIDTH = N_HEADS * 2 * HEAD_DIM
WIDTH_A = N_HEADS * V_DIM
WIDTH_B = D_MODEL - WIDTH_A
CONV_WIDTH = 31
CONV_PAD = CONV_WIDTH - 1
D_FF = 4 * D_MODEL
N_BUCKETS = 32
MAX_DISTANCE = 128
EPS = 1e-6
NEG_INF = -1e30
LAM_INIT = 0.8 - 0.6 * math.exp(-0.3 * 0)

LANES = 128
SUBLANES = 8
HALO = 32
VMEM_LIMIT_BYTES = 56 * 1024 * 1024

F32 = jnp.float32
BF16 = jnp.bfloat16
_NT = (((1,), (1,)), ((), ()))


def _sigmoid(x):
    return 1.0 / (1.0 + jnp.exp(-x))


def _rms_norm_rows(x, g):
    return x * lax.rsqrt(jnp.mean(x * x, axis=-1, keepdims=True) + EPS) * g


def _params(*semantics):
    return pltpu.CompilerParams(dimension_semantics=semantics, vmem_limit_bytes=VMEM_LIMIT_BYTES)


def _resident(shape):
    zeros = (0,) * len(shape)
    return pl.BlockSpec(shape, lambda *_: zeros, pipeline_mode=pl.Buffered(1))


def _head_cols(h):
    return slice(h * V_DIM, (h + 1) * V_DIM)


def _head_rows(h, n):
    return pl.ds(h, n, stride=N_HEADS)


def _in_proj_kernel(x_ref, g_ref, w_ref, q_ref, k_ref, v_ref, kb_ref, vb_ref, u_ref):
    tm = x_ref.shape[0]
    h = _rms_norm_rows(x_ref[...], g_ref[...]).astype(BF16)

    def cols(lo, hi):
        return jnp.dot(h, w_ref[:, lo:hi], preferred_element_type=F32)

    a_cols = 2 * QK_WIDTH + WIDTH_A
    u_ref[...] = cols(a_cols, a_cols + WIDTH_B) * _sigmoid(cols(a_cols + WIDTH_B, a_cols + 2 * WIDTH_B))
    k, v = cols(QK_WIDTH, 2 * QK_WIDTH), cols(2 * QK_WIDTH, a_cols)
    for full, out_ref, bf_ref in ((k, k_ref, kb_ref), (v, v_ref, vb_ref)):
        bf_ref[...] = full.astype(BF16)
        for hd in range(N_HEADS):
            out_ref[_head_rows(hd, tm), :] = full[:, _head_cols(hd)]
    q_ref[...] = (cols(0, QK_WIDTH) * HEAD_DIM ** -0.5).astype(BF16)


def _in_proj(x, ln_g, w_in_bf, tm):
    n = x.shape[0]
    rows = lambda width: pl.BlockSpec((tm, width), lambda i: (i, 0))
    by_head = pl.BlockSpec((tm * N_HEADS, V_DIM), lambda i: (i, 0))
    wide = lambda dt: jax.ShapeDtypeStruct((n, QK_WIDTH), dt)
    tall = jax.ShapeDtypeStruct((n * N_HEADS, V_DIM), F32)
    return pl.pallas_call(
        _in_proj_kernel,
        grid=(n // tm,),
        in_specs=[rows(D_MODEL), _resident((1, D_MODEL)), _resident(w_in_bf.shape)],
        out_specs=[rows(QK_WIDTH), by_head, by_head, rows(QK_WIDTH), rows(WIDTH_A), rows(WIDTH_B)],
        out_shape=[wide(BF16), tall, tall, wide(BF16), wide(BF16), wide(F32)],
        compiler_params=_params("parallel"),
        name="in_proj",
    )(x, ln_g, w_in_bf)


def _lambda(lq1_ref, lk1_ref, lq2_ref, lk2_ref):
    d1 = jnp.sum(lq1_ref[...] * lk1_ref[...], axis=-1, keepdims=True)
    d2 = jnp.sum(lq2_ref[...] * lk2_ref[...], axis=-1, keepdims=True)
    return jnp.exp(d1) - jnp.exp(d2) + LAM_INIT


def _stack_maps(qh):
    lane = lax.broadcasted_iota(jnp.int32, qh.shape, 1)
    zero = jnp.zeros_like(qh)
    return jnp.concatenate([jnp.where(lane < HEAD_DIM, qh, zero), jnp.where(lane >= HEAD_DIM, qh, zero)], axis=0)


def _with_ones(v):
    return jnp.concatenate([v, jnp.ones_like(v)], axis=1)


def _finish_head(acc, lam, g, tq):
    o1 = acc[:tq, :V_DIM] / acc[:tq, V_DIM:]
    o2 = acc[tq:, :V_DIM] / acc[tq:, V_DIM:]
    o = o1 - lam * o2
    return _rms_norm_rows(o, g) * (1.0 - LAM_INIT)


def _prompt_attn_kernel(q_ref, k_ref, v_ref, bt_ref, lq1_ref, lk1_ref, lq2_ref, lk2_ref, g_ref,
                        o_ref, q2_sc, m_sc, acc_sc, *, tq):
    tiles_per_step = q_ref.shape[0] // tq

    def key_tile(slot, start, tk, bias_of_head):
        for h in range(N_HEADS):
            hs = _head_cols(h)
            k_tile, v_tile = k_ref[start:start + tk, hs], _with_ones(v_ref[start:start + tk, hs])
            for rows in (slice(0, tq), slice(tq, 2 * tq)):
                s = lax.dot_general(q2_sc[slot, h, rows], k_tile, _NT, preferred_element_type=F32)
                if bias_of_head is not None:
                    s = s + bias_of_head(h)
                m_prev = m_sc[slot, h, rows]
                m_new = jnp.maximum(m_prev, jnp.max(s, axis=1, keepdims=True))
                alpha = jnp.exp(m_prev - m_new)
                p = jnp.exp(s - jnp.concatenate([m_new] * (tk // LANES), axis=1))
                pv = jnp.dot(p.astype(BF16), v_tile, preferred_element_type=F32)
                acc_sc[slot, h, rows] = jnp.concatenate([alpha, alpha], axis=1) * acc_sc[slot, h, rows] + pv
                m_sc[slot, h, rows] = m_new

    def query_tile(i):
        slot = i % 2
        q_rows = slice((i % tiles_per_step) * tq, (i % tiles_per_step + 1) * tq)
        for h in range(N_HEADS):
            q2_sc[slot, h] = _stack_maps(q_ref[q_rows, _head_cols(h)])
        m_sc[slot] = jnp.full(m_sc.shape[1:], NEG_INF, F32)
        acc_sc[slot] = jnp.zeros(acc_sc.shape[1:], F32)
        n_far = max(i - 1, 0)
        for j in range(n_far // 2):
            key_tile(slot, j * 2 * tq, 2 * tq, None)
        if n_far % 2:
            key_tile(slot, (n_far - 1) * tq, tq, None)
        if i >= 1:
            key_tile(slot, (i - 1) * tq, 2 * tq, lambda h: bt_ref[h])
        else:
            key_tile(slot, 0, tq, lambda h: bt_ref[h, :, tq:])
        lam = _lambda(lq1_ref, lk1_ref, lq2_ref, lk2_ref)
        for h in range(N_HEADS):
            o = _finish_head(acc_sc[slot, h], lam, g_ref[...], tq)
            o_ref[q_rows, _head_cols(h)] = o.astype(o_ref.dtype)

    def step(first_tile):
        for i in range(first_tile, first_tile + tiles_per_step):
            query_tile(i)

    for c in range(k_ref.shape[0] // q_ref.shape[0]):
        pl.when(pl.program_id(1) == c)(functools.partial(step, c * tiles_per_step))


def _prompt_attn(q, kb, vb, bias_tail, lam_params, subln_g, batch, seq, tq, tiles_per_step):
    steps = seq // (tq * tiles_per_step)
    rows = tq * tiles_per_step
    small = [_resident(p.shape) for p in lam_params]
    return pl.pallas_call(
        functools.partial(_prompt_attn_kernel, tq=tq),
        grid=(batch, steps),
        in_specs=[pl.BlockSpec((rows, QK_WIDTH), lambda b, i: (b * steps + i, 0)),
                  pl.BlockSpec((seq, QK_WIDTH), lambda b, i: (b, 0)),
                  pl.BlockSpec((seq, WIDTH_A), lambda b, i: (b, 0)),
                  _resident(bias_tail.shape), *small, _resident(subln_g.shape)],
        out_specs=pl.BlockSpec((rows, WIDTH_A), lambda b, i: (b * steps + i, 0)),
        out_shape=jax.ShapeDtypeStruct((batch * seq, WIDTH_A), BF16),
        scratch_shapes=[pltpu.VMEM((2, N_HEADS, 2 * tq, V_DIM), BF16), pltpu.VMEM((2, N_HEADS, 2 * tq, LANES), F32),
                        pltpu.VMEM((2, N_HEADS, 2 * tq, 2 * V_DIM), F32)],
        compiler_params=_params("parallel", "arbitrary"),
        name="prompt_attn",
    )(q, kb, vb, bias_tail, *lam_params, subln_g)


def _sample_attn_kernel(q_ref, ck_ref, cv_ref, kn_ref, vn_ref, bp_ref, bn_ref, lq1_ref, lk1_ref, lq2_ref, lk2_ref,
                        g_ref, o_ref, *, past_len):
    n_streams = ck_ref.shape[0]
    t = q_ref.shape[0] // n_streams
    lam = _lambda(lq1_ref, lk1_ref, lq2_ref, lk2_ref)
    for s in range(n_streams):
        rows = slice(s * t, (s + 1) * t)
        for h in range(N_HEADS):
            hs = _head_cols(h)
            q2 = _stack_maps(q_ref[rows, hs])
            bias_past, bias_new = bp_ref[h], bn_ref[h]
            k_past = ck_ref[s, _head_rows(h, past_len), :].astype(BF16)
            s_past = lax.dot_general(q2, k_past, _NT, preferred_element_type=F32)
            s_past = s_past + jnp.concatenate([bias_past, bias_past], axis=0)
            s_new = lax.dot_general(q2, kn_ref[rows, hs], _NT, preferred_element_type=F32)
            s_new = s_new + jnp.concatenate([bias_new, bias_new], axis=0)
            m = jnp.maximum(jnp.max(s_past, axis=1, keepdims=True), jnp.max(s_new, axis=1, keepdims=True))
            p_past = jnp.exp(s_past - m).astype(BF16)
            p_new = jnp.exp(s_new - m).astype(BF16)
            v_past = cv_ref[s, _head_rows(h, past_len), :].astype(BF16)
            acc = (jnp.dot(p_past, _with_ones(v_past), preferred_element_type=F32)
                   + jnp.dot(p_new, _with_ones(vn_ref[rows, hs]), preferred_element_type=F32))
            o_ref[rows, hs] = _finish_head(acc, lam, g_ref[...], t).astype(o_ref.dtype)


def _sample_attn(q, cache_k, cache_v, kb, vb, bias_past, bias_new, lam_params, subln_g, past_len, streams):
    batch = cache_k.shape[0]
    t = q.shape[0] // batch
    small = [_resident(p.shape) for p in lam_params]
    new_rows = pl.BlockSpec((streams * t, QK_WIDTH), lambda b: (b, 0))
    cached = pl.BlockSpec((streams, past_len * N_HEADS, V_DIM), lambda b: (b, 0, 0))
    return pl.pallas_call(
        functools.partial(_sample_attn_kernel, past_len=past_len),
        grid=(batch // streams,),
        in_specs=[new_rows, cached, cached, new_rows, new_rows,
                  _resident(bias_past.shape), _resident(bias_new.shape), *small, _resident(subln_g.shape)],
        out_specs=new_rows,
        out_shape=jax.ShapeDtypeStruct(q.shape, BF16),
        compiler_params=_params("parallel"),
        name="sample_attn",
    )(q, cache_k, cache_v, kb, vb, bias_past, bias_new, *lam_params, subln_g)


def _conv_kernel(pre_ref, halo_ref, u_ref, wb_ref, b_ref, g_ref, beta_ref, c_ref, ubuf, ybuf, *, tt, rc):
    n_groups = WIDTH_B // LANES
    n_streams = u_ref.shape[0]
    n_chunks = n_streams * (tt // rc)
    for s in range(n_streams):
        history = jnp.where(pl.program_id(1) == 0, pre_ref[s], halo_ref[s])
        for k in range(tt // rc):
            for lg in range(n_groups):
                ls = slice(lg * LANES, (lg + 1) * LANES)
                slab = (s * (tt // rc) + k) * n_groups + lg
                ubuf[slab, 0:HALO, :] = history[:, ls] if k == 0 else u_ref[s, k * rc - HALO:k * rc, ls]
                ubuf[slab, HALO:, :] = u_ref[s, k * rc:(k + 1) * rc, ls]

    def conv_slab(slab, carry):
        lg = slab % n_groups
        acc = None
        for j in range(CONV_WIDTH):
            first = HALO - CONV_PAD + j
            window = ubuf[slab, first:first + rc, :].reshape(rc // SUBLANES, SUBLANES, LANES)
            term = window * wb_ref[lg, j]
            acc = term if acc is None else acc + term
        ybuf[slab] = acc.reshape(rc, LANES)
        return carry

    lax.fori_loop(0, n_chunks * n_groups, conv_slab, 0)

    for c in range(n_chunks):
        s, k = divmod(c, tt // rc)
        y = jnp.concatenate([ybuf[c * n_groups + lg] for lg in range(n_groups)], axis=1) + b_ref[...]
        mu = jnp.mean(y, axis=-1, keepdims=True)
        var = jnp.mean(jnp.square(y - mu), axis=-1, keepdims=True)
        yn = (y - mu) * lax.rsqrt(var + EPS) * g_ref[...] + beta_ref[...]
        c_ref[s, k * rc:(k + 1) * rc, :] = (yn * _sigmoid(yn)).astype(c_ref.dtype)


def _conv(prefix, u, w_rows, b_dw, ln_g, ln_b, streams, tt, rc):
    batch, seq, _ = u.shape
    halo_blocks = tt // HALO
    slabs = streams * (tt // rc) * (WIDTH_B // LANES)
    small = [_resident(p.shape) for p in (w_rows, b_dw, ln_g, ln_b)]
    return pl.pallas_call(
        functools.partial(_conv_kernel, tt=tt, rc=rc),
        grid=(batch // streams, seq // tt),
        in_specs=[pl.BlockSpec((streams, HALO, WIDTH_B), lambda b, t: (b, 0, 0)),
                  pl.BlockSpec((streams, HALO, WIDTH_B), lambda b, t: (b, jnp.maximum(t * halo_blocks - 1, 0), 0)),
                  pl.BlockSpec((streams, tt, WIDTH_B), lambda b, t: (b, t, 0)), *small],
        out_specs=pl.BlockSpec((streams, tt, WIDTH_B), lambda b, t: (b, t, 0)),
        out_shape=jax.ShapeDtypeStruct(u.shape, BF16),
        scratch_shapes=[pltpu.VMEM((slabs, HALO + rc, LANES), F32), pltpu.VMEM((slabs, rc, LANES), F32)],
        compiler_params=_params("parallel", "parallel"),
        name="conv_ln_swish",
    )(prefix, u, u, w_rows, b_dw, ln_g, ln_b)


def _out_ffn_kernel(x_ref, a_ref, c_ref, wo_ref, g2_ref, wu_ref, wd_ref, gf_ref, y_ref, *, ff_chunk):
    x = (x_ref[...]
         + jnp.dot(a_ref[...], wo_ref[:WIDTH_A, :], preferred_element_type=F32)
         + jnp.dot(c_ref[...], wo_ref[WIDTH_A:, :], preferred_element_type=F32))
    h2 = _rms_norm_rows(x, g2_ref[...]).astype(BF16)
    ffn = None
    for lo in range(0, D_FF, ff_chunk):
        hid = jnp.dot(h2, wu_ref[:, lo:lo + ff_chunk], preferred_element_type=F32)
        hid = jnp.square(jnp.maximum(hid, 0.0)).astype(BF16)
        down = jnp.dot(hid, wd_ref[lo:lo + ff_chunk, :], preferred_element_type=F32)
        ffn = down if ffn is None else ffn + down
    y_ref[...] = _rms_norm_rows(x + ffn, gf_ref[...])


def _out_ffn(x, a, c, w_out_bf, ln2_g, w_up_bf, w_down_bf, ln_f_g, tm, ff_chunk):
    n = x.shape[0]
    rows = lambda width: pl.BlockSpec((tm, width), lambda i: (i, 0))
    return pl.pallas_call(
        functools.partial(_out_ffn_kernel, ff_chunk=ff_chunk),
        grid=(n // tm,),
        in_specs=[rows(D_MODEL), rows(WIDTH_A), rows(WIDTH_B), _resident(w_out_bf.shape), _resident(ln2_g.shape),
                  _resident(w_up_bf.shape), _resident(w_down_bf.shape), _resident(ln_f_g.shape)],
        out_specs=rows(D_MODEL),
        out_shape=jax.ShapeDtypeStruct(x.shape, F32),
        compiler_params=_params("parallel"),
        name="out_ffn",
    )(x, a, c, w_out_bf, ln2_g, w_up_bf, w_down_bf, ln_f_g)


def _bucket_of(rel):
    half = N_BUCKETS // 2
    max_exact = half // 2
    n = -rel
    ret = jnp.where(n < 0, half, 0)
    n = jnp.abs(n)
    nf = jnp.maximum(n, 1).astype(F32)
    large = max_exact + (jnp.log(nf / max_exact) / math.log(MAX_DISTANCE / max_exact)
                         * (half - max_exact)).astype(jnp.int32)
    large = jnp.minimum(large, half - 1)
    return ret + jnp.where(n < max_exact, n, large)


def _bias_table(rel_bias, q0, nq, k0, nk, far_rel):
    span = nq + nk - 1
    shifted = (rel_bias - rel_bias[_bucket_of(far_rel)]).T
    hit = _bucket_of(k0 - (q0 + nq - 1) + jnp.arange(span))[:, None] == jnp.arange(N_BUCKETS)
    by_rel = jnp.sum(jnp.where(hit[None], shifted[:, None, :], 0.0), axis=-1)
    rows = jnp.pad(by_rel, ((0, 0), (0, 1)))
    rows = jnp.broadcast_to(rows[:, None, :], (N_HEADS, nq, span + 1)).reshape(N_HEADS, nq * (span + 1))
    table = rows[:, :nq * span].reshape(N_HEADS, nq, span)[:, :, nq - 1:]
    q_pos, k_pos = q0 + jnp.arange(nq), k0 + jnp.arange(nk)
    visible = (k_pos[None, :] // CHUNK) <= (q_pos[:, None] // CHUNK)
    return jnp.where(visible[None], table, NEG_INF).astype(F32)


def kernel(x_prompt, x_sample, cache_k, cache_v, state_conv, rel_bias, ln1_g, w_in, lambda_q1, lambda_k1, lambda_q2,
           lambda_k2, subln_g, w_dw, b_dw, conv_ln_g, conv_ln_b, w_out, ln2_g, w_up, w_down, ln_f_g):
    batch, seq, _ = x_prompt.shape
    dec_batch, dec_seq, _ = x_sample.shape
    past_len = cache_k.shape[2]
    tq = 256
    assert w_in.shape[0] == 1 and seq % (2 * tq) == 0 and seq >= CONV_PAD and dec_seq >= CONV_PAD

    w_in_bf, w_out_bf = w_in[0].astype(BF16), w_out[0].astype(BF16)
    w_up_bf, w_down_bf = w_up[0].astype(BF16), w_down[0].astype(BF16)
    w_taps = w_dw[0].reshape(CONV_WIDTH, WIDTH_B // LANES, 1, LANES).transpose(1, 0, 2, 3)
    w_rows = jnp.broadcast_to(w_taps, (WIDTH_B // LANES, CONV_WIDTH, SUBLANES, LANES))
    lam_params = (lambda_q1, lambda_k1, lambda_q2, lambda_k2)

    far_rel = jnp.int32(-(2 * tq))
    bias_tail = _bias_table(rel_bias, tq, tq, 0, 2 * tq, far_rel)
    bias_past = _bias_table(rel_bias, past_len, dec_seq, 0, past_len, far_rel)
    bias_new = _bias_table(rel_bias, past_len, dec_seq, past_len, dec_seq, far_rel)

    xp = x_prompt.reshape(batch * seq, D_MODEL)
    q_p, k_p, v_p, kb_p, vb_p, u_p = _in_proj(xp, ln1_g, w_in_bf, tm=1024)
    a_p = _prompt_attn(q_p, kb_p, vb_p, bias_tail, lam_params, subln_g, batch, seq, tq, tiles_per_step=2)
    u_p3 = u_p.reshape(batch, seq, WIDTH_B)
    c_p = _conv(jnp.zeros((batch, HALO, WIDTH_B), F32), u_p3, w_rows, b_dw, conv_ln_g, conv_ln_b,
                streams=1, tt=512, rc=128)
    y_p = _out_ffn(xp, a_p, c_p.reshape(batch * seq, WIDTH_B), w_out_bf, ln2_g, w_up_bf, w_down_bf,
                   ln_f_g[None, :], tm=1024, ff_chunk=1024)

    xs = x_sample.reshape(dec_batch * dec_seq, D_MODEL)
    q_s, k_s, v_s, kb_s, vb_s, u_s = _in_proj(xs, ln1_g, w_in_bf, tm=512)
    by_head = lambda cache: cache[0].reshape(dec_batch, past_len * N_HEADS, V_DIM)
    a_s = _sample_attn(q_s, by_head(cache_k), by_head(cache_v), kb_s, vb_s, bias_past, bias_new,
                       lam_params, subln_g, past_len, streams=2)
    u_s3 = u_s.reshape(dec_batch, dec_seq, WIDTH_B)
    prefix_s = jnp.pad(state_conv[0], ((0, 0), (HALO - CONV_PAD, 0), (0, 0)))
    c_s = _conv(prefix_s, u_s3, w_rows, b_dw, conv_ln_g, conv_ln_b, streams=8, tt=dec_seq, rc=dec_seq)
    y_s = _out_ffn(xs, a_s, c_s.reshape(dec_batch * dec_seq, WIDTH_B), w_out_bf, ln2_g, w_up_bf, w_down_bf,
                   ln_f_g[None, :], tm=512, ff_chunk=1024)

    heads = lambda t, b, s: t.reshape(1, b, s, N_HEADS, V_DIM)
    return (y_p.reshape(batch, seq, D_MODEL), y_s.reshape(dec_batch, dec_seq, D_MODEL),
            heads(k_p, batch, seq), heads(v_p, batch, seq), u_p3[None, :, seq - CONV_PAD:],
            heads(k_s, dec_batch, dec_seq), heads(v_s, dec_batch, dec_seq), u_s3[None, :, dec_seq - CONV_PAD:])
```

```python
import functools
import math
from typing import NamedTuple

import jax
import jax.numpy as jnp
from jax import lax
from jax.experimental import pallas as pl
from jax.experimental.pallas import tpu as pltpu

D_MODEL = 1024
CHUNK = 64
N_HEADS = 4
HEAD_DIM = 64
V_DIM = 2 * HEAD_DIM
QK_WIDTH = N_HEADS * 2 * HEAD_DIM
WIDTH_A = N_HEADS * V_DIM
WIDTH_B = D_MODEL - WIDTH_A
CONV_WIDTH = 31
CONV_PAD = CONV_WIDTH - 1
D_FF = 4 * D_MODEL
N_BUCKETS = 32
MAX_DISTANCE = 128
EPS = 1e-6
NEG_INF = -1e30
LAM_INIT = 0.8 - 0.6 * math.exp(-0.3 * 0)

LANES = 128
SUBLANES = 8
HALO = 32
VMEM_LIMIT_BYTES = 56 * 1024 * 1024

F32 = jnp.float32
BF16 = jnp.bfloat16
_NT = (((1,), (1,)), ((), ()))


def _sigmoid(x):
    return 1.0 / (1.0 + jnp.exp(-x))


def _rms_norm_rows(x, g):
    return x * lax.rsqrt(jnp.mean(x * x, axis=-1, keepdims=True) + EPS) * g


def _params(*semantics):
    return pltpu.CompilerParams(dimension_semantics=semantics, vmem_limit_bytes=VMEM_LIMIT_BYTES)


def _resident(shape):
    zeros = (0,) * len(shape)
    return pl.BlockSpec(shape, lambda *_: zeros, pipeline_mode=pl.Buffered(1))


def _head_cols(h):
    return slice(h * V_DIM, (h + 1) * V_DIM)


def _head_rows(h, n):
    return pl.ds(h, n, stride=N_HEADS)


def _in_proj_kernel(x_ref, g_ref, w_ref, q_ref, k_ref, v_ref, kb_ref, vb_ref, u_ref):
    tm = x_ref.shape[0]
    h = _rms_norm_rows(x_ref[...], g_ref[...]).astype(BF16)

    def cols(lo, hi):
        return jnp.dot(h, w_ref[:, lo:hi], preferred_element_type=F32)

    a_cols = 2 * QK_WIDTH + WIDTH_A
    u_ref[...] = cols(a_cols, a_cols + WIDTH_B) * _sigmoid(cols(a_cols + WIDTH_B, a_cols + 2 * WIDTH_B))
    k, v = cols(QK_WIDTH, 2 * QK_WIDTH), cols(2 * QK_WIDTH, a_cols)
    for full, out_ref, bf_ref in ((k, k_ref, kb_ref), (v, v_ref, vb_ref)):
        bf_ref[...] = full.astype(BF16)
        for hd in range(N_HEADS):
            out_ref[_head_rows(hd, tm), :] = full[:, _head_cols(hd)]
    q_ref[...] = (cols(0, QK_WIDTH) * HEAD_DIM ** -0.5).astype(BF16)


def _in_proj(x, ln_g, w_in_bf, tm):
    n = x.shape[0]
    rows = lambda width: pl.BlockSpec((tm, width), lambda i: (i, 0))
    by_head = pl.BlockSpec((tm * N_HEADS, V_DIM), lambda i: (i, 0))
    wide = lambda dt: jax.ShapeDtypeStruct((n, QK_WIDTH), dt)
    tall = jax.ShapeDtypeStruct((n * N_HEADS, V_DIM), F32)
    return pl.pallas_call(
        _in_proj_kernel,
        grid=(n // tm,),
        in_specs=[rows(D_MODEL), _resident((1, D_MODEL)), _resident(w_in_bf.shape)],
        out_specs=[rows(QK_WIDTH), by_head, by_head, rows(QK_WIDTH), rows(WIDTH_A), rows(WIDTH_B)],
        out_shape=[wide(BF16), tall, tall, wide(BF16), wide(BF16), wide(F32)],
        compiler_params=_params("parallel"),
        name="in_proj",
    )(x, ln_g, w_in_bf)


def _lambda(lq1_ref, lk1_ref, lq2_ref, lk2_ref):
    d1 = jnp.sum(lq1_ref[...] * lk1_ref[...], axis=-1, keepdims=True)
    d2 = jnp.sum(lq2_ref[...] * lk2_ref[...], axis=-1, keepdims=True)
    return jnp.exp(d1) - jnp.exp(d2) + LAM_INIT


def _stack_maps(qh):
    lane = lax.broadcasted_iota(jnp.int32, qh.shape, 1)
    zero = jnp.zeros_like(qh)
    return jnp.concatenate([jnp.where(lane < HEAD_DIM, qh, zero), jnp.where(lane >= HEAD_DIM, qh, zero)], axis=0)


def _with_ones(v):
    return jnp.concatenate([v, jnp.ones_like(v)], axis=1)


def _finish_head(acc, lam, g, tq):
    o1 = acc[:tq, :V_DIM] / acc[:tq, V_DIM:]
    o2 = acc[tq:, :V_DIM] / acc[tq:, V_DIM:]
    o = o1 - lam * o2
    return _rms_norm_rows(o, g) * (1.0 - LAM_INIT)


def _prompt_attn_kernel(q_ref, k_ref, v_ref, bt_ref, lq1_ref, lk1_ref, lq2_ref, lk2_ref, g_ref,
                        o_ref, q2_sc, m_sc, acc_sc, *, tq):
    tiles_per_step = q_ref.shape[0] // tq

    def key_tile(slot, start, tk, bias_of_head):
        for h in range(N_HEADS):
            hs = _head_cols(h)
            k_tile, v_tile = k_ref[start:start + tk, hs], _with_ones(v_ref[start:start + tk, hs])
            for rows in (slice(0, tq), slice(tq, 2 * tq)):
                s = lax.dot_general(q2_sc[slot, h, rows], k_tile, _NT, preferred_element_type=F32)
                if bias_of_head is not None:
                    s = s + bias_of_head(h)
                m_prev = m_sc[slot, h, rows]
                m_new = jnp.maximum(m_prev, jnp.max(s, axis=1, keepdims=True))
                alpha = jnp.exp(m_prev - m_new)
                p = jnp.exp(s - jnp.concatenate([m_new] * (tk // LANES), axis=1))
                pv = jnp.dot(p.astype(BF16), v_tile, preferred_element_type=F32)
                acc_sc[slot, h, rows] = jnp.concatenate([alpha, alpha], axis=1) * acc_sc[slot, h, rows] + pv
                m_sc[slot, h, rows] = m_new

    def query_tile(i):
        slot = i % 2
        q_rows = slice((i % tiles_per_step) * tq, (i % tiles_per_step + 1) * tq)
        for h in range(N_HEADS):
            q2_sc[slot, h] = _stack_maps(q_ref[q_rows, _head_cols(h)])
        m_sc[slot] = jnp.full(m_sc.shape[1:], NEG_INF, F32)
        acc_sc[slot] = jnp.zeros(acc_sc.shape[1:], F32)
        n_far = max(i - 1, 0)
        for j in range(n_far // 2):
            key_tile(slot, j * 2 * tq, 2 * tq, None)
        if n_far % 2:
            key_tile(slot, (n_far - 1) * tq, tq, None)
        if i >= 1:
            key_tile(slot, (i - 1) * tq, 2 * tq, lambda h: bt_ref[h])
        else:
            key_tile(slot, 0, tq, lambda h: bt_ref[h, :, tq:])
        lam = _lambda(lq1_ref, lk1_ref, lq2_ref, lk2_ref)
        for h in range(N_HEADS):
            o = _finish_head(acc_sc[slot, h], lam, g_ref[...], tq)
            o_ref[q_rows, _head_cols(h)] = o.astype(o_ref.dtype)

    def step(first_tile):
        for i in range(first_tile, first_tile + tiles_per_step):
            query_tile(i)

    for c in range(k_ref.shape[0] // q_ref.shape[0]):
        pl.when(pl.program_id(1) == c)(functools.partial(step, c * tiles_per_step))


def _prompt_attn(q, kb, vb, bias_tail, lam_params, subln_g, batch, seq, tq, tiles_per_step):
    steps = seq // (tq * tiles_per_step)
    rows = tq * tiles_per_step
    small = [_resident(p.shape) for p in lam_params]
    return pl.pallas_call(
        functools.partial(_prompt_attn_kernel, tq=tq),
        grid=(batch, steps),
        in_specs=[pl.BlockSpec((rows, QK_WIDTH), lambda b, i: (b * steps + i, 0)),
                  pl.BlockSpec((seq, QK_WIDTH), lambda b, i: (b, 0)),
                  pl.BlockSpec((seq, WIDTH_A), lambda b, i: (b, 0)),
                  _resident(bias_tail.shape), *small, _resident(subln_g.shape)],
        out_specs=pl.BlockSpec((rows, WIDTH_A), lambda b, i: (b * steps + i, 0)),
        out_shape=jax.ShapeDtypeStruct((batch * seq, WIDTH_A), BF16),
        scratch_shapes=[pltpu.VMEM((2, N_HEADS, 2 * tq, V_DIM), BF16), pltpu.VMEM((2, N_HEADS, 2 * tq, LANES), F32),
                        pltpu.VMEM((2, N_HEADS, 2 * tq, 2 * V_DIM), F32)],
        compiler_params=_params("parallel", "arbitrary"),
        name="prompt_attn",
    )(q, kb, vb, bias_tail, *lam_params, subln_g)


def _sample_attn_kernel(q_ref, ck_ref, cv_ref, kn_ref, vn_ref, bp_ref, bn_ref, lq1_ref, lk1_ref, lq2_ref, lk2_ref,
                        g_ref, o_ref, *, past_len):
    n_streams = ck_ref.shape[0]
    t = q_ref.shape[0] // n_streams
    lam = _lambda(lq1_ref, lk1_ref, lq2_ref, lk2_ref)
    for s in range(n_streams):
        rows = slice(s * t, (s + 1) * t)
        for h in range(N_HEADS):
            hs = _head_cols(h)
            q2 = _stack_maps(q_ref[rows, hs])
            bias_past, bias_new = bp_ref[h], bn_ref[h]
            k_past = ck_ref[s, _head_rows(h, past_len), :].astype(BF16)
            s_past = lax.dot_general(q2, k_past, _NT, preferred_element_type=F32)
            s_past = s_past + jnp.concatenate([bias_past, bias_past], axis=0)
            s_new = lax.dot_general(q2, kn_ref[rows, hs], _NT, preferred_element_type=F32)
            s_new = s_new + jnp.concatenate([bias_new, bias_new], axis=0)
            m = jnp.maximum(jnp.max(s_past, axis=1, keepdims=True), jnp.max(s_new, axis=1, keepdims=True))
            p_past = jnp.exp(s_past - m).astype(BF16)
            p_new = jnp.exp(s_new - m).astype(BF16)
            v_past = cv_ref[s, _head_rows(h, past_len), :].astype(BF16)
            acc = (jnp.dot(p_past, _with_ones(v_past), preferred_element_type=F32)
                   + jnp.dot(p_new, _with_ones(vn_ref[rows, hs]), preferred_element_type=F32))
            o_ref[rows, hs] = _finish_head(acc, lam, g_ref[...], t).astype(o_ref.dtype)


def _sample_attn(q, cache_k, cache_v, kb, vb, bias_past, bias_new, lam_params, subln_g, past_len, streams):
    batch = cache_k.shape[0]
    t = q.shape[0] // batch
    small = [_resident(p.shape) for p in lam_params]
    new_rows = pl.BlockSpec((streams * t, QK_WIDTH), lambda b: (b, 0))
    cached = pl.BlockSpec((streams, past_len * N_HEADS, V_DIM), lambda b: (b, 0, 0))
    return pl.pallas_call(
        functools.partial(_sample_attn_kernel, past_len=past_len),
        grid=(batch // streams,),
        in_specs=[new_rows, cached, cached, new_rows, new_rows,
                  _resident(bias_past.shape), _resident(bias_new.shape), *small, _resident(subln_g.shape)],
        out_specs=new_rows,
        out_shape=jax.ShapeDtypeStruct(q.shape, BF16),
        compiler_params=_params("parallel"),
        name="sample_attn",
    )(q, cache_k, cache_v, kb, vb, bias_past, bias_new, *lam_params, subln_g)


def _conv_kernel(pre_ref, halo_ref, u_ref, wb_ref, b_ref, g_ref, beta_ref, c_ref, ubuf, ybuf, *, tt, rc):
    n_groups = WIDTH_B // LANES
    n_streams = u_ref.shape[0]
    n_chunks = n_streams * (tt // rc)
    for s in range(n_streams):
        history = jnp.where(pl.program_id(1) == 0, pre_ref[s], halo_ref[s])
        for k in range(tt // rc):
            for lg in range(n_groups):
                ls = slice(lg * LANES, (lg + 1) * LANES)
                slab = (s * (tt // rc) + k) * n_groups + lg
                ubuf[slab, 0:HALO, :] = history[:, ls] if k == 0 else u_ref[s, k * rc - HALO:k * rc, ls]
                ubuf[slab, HALO:, :] = u_ref[s, k * rc:(k + 1) * rc, ls]

    def conv_slab(slab, carry):
        lg = slab % n_groups
        acc = None
        for j in range(CONV_WIDTH):
            first = HALO - CONV_PAD + j
            window = ubuf[slab, first:first + rc, :].reshape(rc // SUBLANES, SUBLANES, LANES)
            term = window * wb_ref[lg, j]
            acc = term if acc is None else acc + term
        ybuf[slab] = acc.reshape(rc, LANES)
        return carry

    lax.fori_loop(0, n_chunks * n_groups, conv_slab, 0)

    for c in range(n_chunks):
        s, k = divmod(c, tt // rc)
        y = jnp.concatenate([ybuf[c * n_groups + lg] for lg in range(n_groups)], axis=1) + b_ref[...]
        mu = jnp.mean(y, axis=-1, keepdims=True)
        var = jnp.mean(jnp.square(y - mu), axis=-1, keepdims=True)
        yn = (y - mu) * lax.rsqrt(var + EPS) * g_ref[...] + beta_ref[...]
        c_ref[s, k * rc:(k + 1) * rc, :] = (yn * _sigmoid(yn)).astype(c_ref.dtype)


def _conv(prefix, u, w_rows, b_dw, ln_g, ln_b, streams, tt, rc):
    batch, seq, _ = u.shape
    halo_blocks = tt // HALO
    slabs = streams * (tt // rc) * (WIDTH_B // LANES)
    small = [_resident(p.shape) for p in (w_rows, b_dw, ln_g, ln_b)]
    return pl.pallas_call(
        functools.partial(_conv_kernel, tt=tt, rc=rc),
        grid=(batch // streams, seq // tt),
        in_specs=[pl.BlockSpec((streams, HALO, WIDTH_B), lambda b, t: (b, 0, 0)),
                  pl.BlockSpec((streams, HALO, WIDTH_B), lambda b, t: (b, jnp.maximum(t * halo_blocks - 1, 0), 0)),
                  pl.BlockSpec((streams, tt, WIDTH_B), lambda b, t: (b, t, 0)), *small],
        out_specs=pl.BlockSpec((streams, tt, WIDTH_B), lambda b, t: (b, t, 0)),
        out_shape=jax.ShapeDtypeStruct(u.shape, BF16),
        scratch_shapes=[pltpu.VMEM((slabs, HALO + rc, LANES), F32), pltpu.VMEM((slabs, rc, LANES), F32)],
        compiler_params=_params("parallel", "parallel"),
        name="conv_ln_swish",
    )(prefix, u, u, w_rows, b_dw, ln_g, ln_b)


def _out_ffn_kernel(x_ref, a_ref, c_ref, wo_ref, g2_ref, wu_ref, wd_ref, gf_ref, y_ref, *, ff_chunk):
    x = (x_ref[...]
         + jnp.dot(a_ref[...], wo_ref[:WIDTH_A, :], preferred_element_type=F32)
         + jnp.dot(c_ref[...], wo_ref[WIDTH_A:, :], preferred_element_type=F32))
    h2 = _rms_norm_rows(x, g2_ref[...]).astype(BF16)
    ffn = None
    for lo in range(0, D_FF, ff_chunk):
        hid = jnp.dot(h2, wu_ref[:, lo:lo + ff_chunk], preferred_element_type=F32)
        hid = jnp.square(jnp.maximum(hid, 0.0)).astype(BF16)
        down = jnp.dot(hid, wd_ref[lo:lo + ff_chunk, :], preferred_element_type=F32)
        ffn = down if ffn is None else ffn + down
    y_ref[...] = _rms_norm_rows(x + ffn, gf_ref[...])


def _out_ffn(x, a, c, w_out_bf, ln2_g, w_up_bf, w_down_bf, ln_f_g, tm, ff_chunk):
    n = x.shape[0]
    rows = lambda width: pl.BlockSpec((tm, width), lambda i: (i, 0))
    return pl.pallas_call(
        functools.partial(_out_ffn_kernel, ff_chunk=ff_chunk),
        grid=(n // tm,),
        in_specs=[rows(D_MODEL), rows(WIDTH_A), rows(WIDTH_B), _resident(w_out_bf.shape), _resident(ln2_g.shape),
                  _resident(w_up_bf.shape), _resident(w_down_bf.shape), _resident(ln_f_g.shape)],
        out_specs=rows(D_MODEL),
        out_shape=jax.ShapeDtypeStruct(x.shape, F32),
        compiler_params=_params("parallel"),
        name="out_ffn",
    )(x, a, c, w_out_bf, ln2_g, w_up_bf, w_down_bf, ln_f_g)


def _bucket_of(rel):
    half = N_BUCKETS // 2
    max_exact = half // 2
    n = -rel
    ret = jnp.where(n < 0, half, 0)
    n = jnp.abs(n)
    nf = jnp.maximum(n, 1).astype(F32)
    large = max_exact + (jnp.log(nf / max_exact) / math.log(MAX_DISTANCE / max_exact)
                         * (half - max_exact)).astype(jnp.int32)
    large = jnp.minimum(large, half - 1)
    return ret + jnp.where(n < max_exact, n, large)


def _bias_table(rel_bias, q0, nq, k0, nk, far_rel):
    span = nq + nk - 1
    shifted = (rel_bias - rel_bias[_bucket_of(far_rel)]).T
    hit = _bucket_of(k0 - (q0 + nq - 1) + jnp.arange(span))[:, None] == jnp.arange(N_BUCKETS)
    by_rel = jnp.sum(jnp.where(hit[None], shifted[:, None, :], 0.0), axis=-1)
    rows = jnp.pad(by_rel, ((0, 0), (0, 1)))
    rows = jnp.broadcast_to(rows[:, None, :], (N_HEADS, nq, span + 1)).reshape(N_HEADS, nq * (span + 1))
    table = rows[:, :nq * span].reshape(N_HEADS, nq, span)[:, :, nq - 1:]
    q_pos, k_pos = q0 + jnp.arange(nq), k0 + jnp.arange(nk)
    visible = (k_pos[None, :] // CHUNK) <= (q_pos[:, None] // CHUNK)
    return jnp.where(visible[None], table, NEG_INF).astype(F32)


class _Tiles(NamedTuple):
    prompt_tokens: int
    sample_tokens: int
    ff_chunk: int
    tq: int
    attn_tiles_per_step: int
    conv_rows: int
    conv_chunk: int
    sample_conv_streams: int
    sample_attn_streams: int


def _tiles(batch, seq, dec_batch, dec_seq, past_len):
    t = _Tiles(prompt_tokens=1024, sample_tokens=512, ff_chunk=1024, tq=256, attn_tiles_per_step=2,
               conv_rows=1024, conv_chunk=128, sample_conv_streams=8, sample_attn_streams=2)
    assert (batch * seq) % t.prompt_tokens == 0 and (dec_batch * dec_seq) % t.sample_tokens == 0
    assert D_FF % t.ff_chunk == 0 and seq % (t.tq * t.attn_tiles_per_step) == 0 and t.tq % LANES == 0
    assert seq % t.conv_rows == 0 and t.conv_rows % t.conv_chunk == 0 and t.conv_chunk % HALO == 0
    assert dec_batch % t.sample_conv_streams == 0 and dec_batch % t.sample_attn_streams == 0
    assert dec_seq % HALO == 0 and seq >= CONV_PAD and dec_seq >= CONV_PAD and past_len % CHUNK == 0
    return t


def kernel(x_prompt, x_sample, cache_k, cache_v, state_conv, rel_bias, ln1_g, w_in, lambda_q1, lambda_k1, lambda_q2,
           lambda_k2, subln_g, w_dw, b_dw, conv_ln_g, conv_ln_b, w_out, ln2_g, w_up, w_down, ln_f_g):
    batch, seq, _ = x_prompt.shape
    dec_batch, dec_seq, _ = x_sample.shape
    past_len = cache_k.shape[2]
    assert w_in.shape[0] == 1, "one layer"
    tiles = _tiles(batch, seq, dec_batch, dec_seq, past_len)
    tq = tiles.tq

    w_in_bf, w_out_bf = w_in[0].astype(BF16), w_out[0].astype(BF16)
    w_up_bf, w_down_bf = w_up[0].astype(BF16), w_down[0].astype(BF16)
    w_taps = w_dw[0].reshape(CONV_WIDTH, WIDTH_B // LANES, 1, LANES).transpose(1, 0, 2, 3)
    w_rows = jnp.broadcast_to(w_taps, (WIDTH_B // LANES, CONV_WIDTH, SUBLANES, LANES))
    lam_params = (lambda_q1, lambda_k1, lambda_q2, lambda_k2)

    far_rel = jnp.int32(-(2 * tq))
    bias_tail = _bias_table(rel_bias, tq, tq, 0, 2 * tq, far_rel)
    bias_past = _bias_table(rel_bias, past_len, dec_seq, 0, past_len, far_rel)
    bias_new = _bias_table(rel_bias, past_len, dec_seq, past_len, dec_seq, far_rel)

    xp = x_prompt.reshape(batch * seq, D_MODEL)
    q_p, k_p, v_p, kb_p, vb_p, u_p = _in_proj(xp, ln1_g, w_in_bf, tm=tiles.prompt_tokens)
    a_p = _prompt_attn(q_p, kb_p, vb_p, bias_tail, lam_params, subln_g, batch, seq, tq, tiles.attn_tiles_per_step)
    u_p3 = u_p.reshape(batch, seq, WIDTH_B)
    c_p = _conv(jnp.zeros((batch, HALO, WIDTH_B), F32), u_p3, w_rows, b_dw, conv_ln_g, conv_ln_b,
                streams=1, tt=tiles.conv_rows, rc=tiles.conv_chunk)
    y_p = _out_ffn(xp, a_p, c_p.reshape(batch * seq, WIDTH_B), w_out_bf, ln2_g, w_up_bf, w_down_bf,
                   ln_f_g[None, :], tm=tiles.prompt_tokens, ff_chunk=tiles.ff_chunk)

    xs = x_sample.reshape(dec_batch * dec_seq, D_MODEL)
    q_s, k_s, v_s, kb_s, vb_s, u_s = _in_proj(xs, ln1_g, w_in_bf, tm=tiles.sample_tokens)
    by_head = lambda cache: cache[0].reshape(dec_batch, past_len * N_HEADS, V_DIM)
    a_s = _sample_attn(q_s, by_head(cache_k), by_head(cache_v), kb_s, vb_s, bias_past, bias_new,
                       lam_params, subln_g, past_len, streams=tiles.sample_attn_streams)
    u_s3 = u_s.reshape(dec_batch, dec_seq, WIDTH_B)
    prefix_s = jnp.pad(state_conv[0], ((0, 0), (HALO - CONV_PAD, 0), (0, 0)))
    c_s = _conv(prefix_s, u_s3, w_rows, b_dw, conv_ln_g, conv_ln_b,
                streams=tiles.sample_conv_streams, tt=dec_seq, rc=dec_seq)
    y_s = _out_ffn(xs, a_s, c_s.reshape(dec_batch * dec_seq, WIDTH_B), w_out_bf, ln2_g, w_up_bf, w_down_bf,
                   ln_f_g[None, :], tm=tiles.sample_tokens, ff_chunk=tiles.ff_chunk)

    heads = lambda t, b, s: t.reshape(1, b, s, N_HEADS, V_DIM)
    return (y_p.reshape(batch, seq, D_MODEL), y_s.reshape(dec_batch, dec_seq, D_MODEL),
            heads(k_p, batch, seq), heads(v_p, batch, seq), u_p3[None, :, seq - CONV_PAD:],
            heads(k_s, dec_batch, dec_seq), heads(v_s, dec_batch, dec_seq), u_s3[None, :, dec_seq - CONV_PAD:])
```

```python
import functools
import math
from typing import NamedTuple

import jax
import jax.numpy as jnp
from jax import lax
from jax.experimental import pallas as pl
from jax.experimental.pallas import tpu as pltpu

D_MODEL = 1024
CHUNK = 64
N_HEADS = 4
HEAD_DIM = 64
V_DIM = 2 * HEAD_DIM
QK_WIDTH = N_HEADS * 2 * HEAD_DIM
WIDTH_A = N_HEADS * V_DIM
WIDTH_B = D_MODEL - WIDTH_A
CONV_WIDTH = 31
CONV_PAD = CONV_WIDTH - 1
D_FF = 4 * D_MODEL
N_BUCKETS = 32
MAX_DISTANCE = 128
EPS = 1e-6
NEG_INF = -1e30
LAM_INIT = 0.8 - 0.6 * math.exp(-0.3 * 0)

LANES = 128
SUBLANES = 8
HALO = 32
VMEM_LIMIT_BYTES = 56 * 1024 * 1024

F32 = jnp.float32
BF16 = jnp.bfloat16
_NT = (((1,), (1,)), ((), ()))


def _sigmoid(x):
    return 1.0 / (1.0 + jnp.exp(-x))


def _rms_norm_rows(x, g):
    return x * lax.rsqrt(jnp.mean(x * x, axis=-1, keepdims=True) + EPS) * g


def _params(*semantics):
    return pltpu.CompilerParams(dimension_semantics=semantics, vmem_limit_bytes=VMEM_LIMIT_BYTES)


def _resident(shape):
    zeros = (0,) * len(shape)
    return pl.BlockSpec(shape, lambda *_: zeros, pipeline_mode=pl.Buffered(1))


def _head_cols(h):
    return slice(h * V_DIM, (h + 1) * V_DIM)


def _head_rows(h, n):
    return pl.ds(h, n, stride=N_HEADS)


def _in_proj_kernel(x_ref, g_ref, w_ref, q_ref, k_ref, v_ref, kb_ref, vb_ref, u_ref):
    tm = x_ref.shape[0]
    h = _rms_norm_rows(x_ref[...], g_ref[...]).astype(BF16)

    def cols(lo, hi):
        return jnp.dot(h, w_ref[:, lo:hi], preferred_element_type=F32)

    a_cols = 2 * QK_WIDTH + WIDTH_A
    u_ref[...] = cols(a_cols, a_cols + WIDTH_B) * _sigmoid(cols(a_cols + WIDTH_B, a_cols + 2 * WIDTH_B))
    k, v = cols(QK_WIDTH, 2 * QK_WIDTH), cols(2 * QK_WIDTH, a_cols)
    for full, out_ref, bf_ref in ((k, k_ref, kb_ref), (v, v_ref, vb_ref)):
        bf_ref[...] = full.astype(BF16)
        for hd in range(N_HEADS):
            out_ref[_head_rows(hd, tm), :] = full[:, _head_cols(hd)]
    q_ref[...] = (cols(0, QK_WIDTH) * HEAD_DIM ** -0.5).astype(BF16)


def _in_proj(x, ln_g, w_in_bf, tm):
    n = x.shape[0]
    rows = lambda width: pl.BlockSpec((tm, width), lambda i: (i, 0))
    by_head = pl.BlockSpec((tm * N_HEADS, V_DIM), lambda i: (i, 0))
    wide = lambda dt: jax.ShapeDtypeStruct((n, QK_WIDTH), dt)
    tall = jax.ShapeDtypeStruct((n * N_HEADS, V_DIM), F32)
    return pl.pallas_call(
        _in_proj_kernel,
        grid=(n // tm,),
        in_specs=[rows(D_MODEL), _resident((1, D_MODEL)), _resident(w_in_bf.shape)],
        out_specs=[rows(QK_WIDTH), by_head, by_head, rows(QK_WIDTH), rows(WIDTH_A), rows(WIDTH_B)],
        out_shape=[wide(BF16), tall, tall, wide(BF16), wide(BF16), wide(F32)],
        compiler_params=_params("parallel"),
        name="in_proj",
    )(x, ln_g, w_in_bf)


def _lambda(lq1_ref, lk1_ref, lq2_ref, lk2_ref):
    d1 = jnp.sum(lq1_ref[...] * lk1_ref[...], axis=-1, keepdims=True)
    d2 = jnp.sum(lq2_ref[...] * lk2_ref[...], axis=-1, keepdims=True)
    return jnp.exp(d1) - jnp.exp(d2) + LAM_INIT


def _stack_maps(qh):
    lane = lax.broadcasted_iota(jnp.int32, qh.shape, 1)
    zero = jnp.zeros_like(qh)
    return jnp.concatenate([jnp.where(lane < HEAD_DIM, qh, zero), jnp.where(lane >= HEAD_DIM, qh, zero)], axis=0)


def _with_ones(v):
    return jnp.concatenate([v, jnp.ones_like(v)], axis=1)


def _finish_head(acc, lam, g, tq):
    o1 = acc[:tq, :V_DIM] / acc[:tq, V_DIM:]
    o2 = acc[tq:, :V_DIM] / acc[tq:, V_DIM:]
    o = o1 - lam * o2
    return _rms_norm_rows(o, g) * (1.0 - LAM_INIT)


def _prompt_attn_kernel(q_ref, k_ref, v_ref, bt_ref, lq1_ref, lk1_ref, lq2_ref, lk2_ref, g_ref,
                        o_ref, q2_sc, m_sc, acc_sc, *, tq):
    tiles_per_step = q_ref.shape[0] // tq

    def key_tile(slot, start, tk, bias_of_head):
        for h in range(N_HEADS):
            hs = _head_cols(h)
            k_tile, v_tile = k_ref[start:start + tk, hs], _with_ones(v_ref[start:start + tk, hs])
            for rows in (slice(0, tq), slice(tq, 2 * tq)):
                s = lax.dot_general(q2_sc[slot, h, rows], k_tile, _NT, preferred_element_type=F32)
                if bias_of_head is not None:
                    s = s + bias_of_head(h)
                m_new = jnp.broadcast_to(jnp.max(s, axis=1, keepdims=True), (tq, LANES))
                if start > 0:
                    m_prev = m_sc[slot, h, rows]
                    m_new = jnp.maximum(m_prev, m_new)
                    alpha = jnp.exp(m_prev - m_new)
                p = jnp.exp(s - jnp.concatenate([m_new] * (tk // LANES), axis=1))
                pv = jnp.dot(p.astype(BF16), v_tile, preferred_element_type=F32)
                if start > 0:
                    pv = jnp.concatenate([alpha, alpha], axis=1) * acc_sc[slot, h, rows] + pv
                acc_sc[slot, h, rows] = pv
                m_sc[slot, h, rows] = m_new

    def query_tile(i):
        slot = i % 2
        q_rows = slice((i % tiles_per_step) * tq, (i % tiles_per_step + 1) * tq)
        for h in range(N_HEADS):
            q2_sc[slot, h] = _stack_maps(q_ref[q_rows, _head_cols(h)])
        n_far = max(i - 1, 0)
        for j in range(n_far // 2):
            key_tile(slot, j * 2 * tq, 2 * tq, None)
        if n_far % 2:
            key_tile(slot, (n_far - 1) * tq, tq, None)
        if i >= 1:
            key_tile(slot, (i - 1) * tq, 2 * tq, lambda h: bt_ref[h])
        else:
            key_tile(slot, 0, tq, lambda h: bt_ref[h, :, tq:])
        lam = _lambda(lq1_ref, lk1_ref, lq2_ref, lk2_ref)
        for h in range(N_HEADS):
            o = _finish_head(acc_sc[slot, h], lam, g_ref[...], tq)
            o_ref[q_rows, _head_cols(h)] = o.astype(o_ref.dtype)

    def step(first_tile):
        for i in range(first_tile, first_tile + tiles_per_step):
            query_tile(i)

    for c in range(k_ref.shape[0] // q_ref.shape[0]):
        pl.when(pl.program_id(1) == c)(functools.partial(step, c * tiles_per_step))


def _prompt_attn(q, kb, vb, bias_tail, lam_params, subln_g, batch, seq, tq, tiles_per_step):
    steps = seq // (tq * tiles_per_step)
    rows = tq * tiles_per_step
    small = [_resident(p.shape) for p in lam_params]
    return pl.pallas_call(
        functools.partial(_prompt_attn_kernel, tq=tq),
        grid=(batch, steps),
        in_specs=[pl.BlockSpec((rows, QK_WIDTH), lambda b, i: (b * steps + i, 0)),
                  pl.BlockSpec((seq, QK_WIDTH), lambda b, i: (b, 0)),
                  pl.BlockSpec((seq, WIDTH_A), lambda b, i: (b, 0)),
                  _resident(bias_tail.shape), *small, _resident(subln_g.shape)],
        out_specs=pl.BlockSpec((rows, WIDTH_A), lambda b, i: (b * steps + i, 0)),
        out_shape=jax.ShapeDtypeStruct((batch * seq, WIDTH_A), BF16),
        scratch_shapes=[pltpu.VMEM((2, N_HEADS, 2 * tq, V_DIM), BF16), pltpu.VMEM((2, N_HEADS, 2 * tq, LANES), F32),
                        pltpu.VMEM((2, N_HEADS, 2 * tq, 2 * V_DIM), F32)],
        compiler_params=_params("parallel", "arbitrary"),
        name="prompt_attn",
    )(q, kb, vb, bias_tail, *lam_params, subln_g)


def _sample_attn_kernel(q_ref, ck_ref, cv_ref, kn_ref, vn_ref, bp_ref, bn_ref, lq1_ref, lk1_ref, lq2_ref, lk2_ref,
                        g_ref, o_ref, *, past_len):
    n_streams = ck_ref.shape[0]
    t = q_ref.shape[0] // n_streams
    lam = _lambda(lq1_ref, lk1_ref, lq2_ref, lk2_ref)
    for s in range(n_streams):
        rows = slice(s * t, (s + 1) * t)
        for h in range(N_HEADS):
            hs = _head_cols(h)
            q2 = _stack_maps(q_ref[rows, hs])
            bias_past, bias_new = bp_ref[h], bn_ref[h]
            k_past = ck_ref[s, _head_rows(h, past_len), :].astype(BF16)
            s_past = lax.dot_general(q2, k_past, _NT, preferred_element_type=F32)
            s_past = s_past + jnp.concatenate([bias_past, bias_past], axis=0)
            s_new = lax.dot_general(q2, kn_ref[rows, hs], _NT, preferred_element_type=F32)
            s_new = s_new + jnp.concatenate([bias_new, bias_new], axis=0)
            m = jnp.maximum(jnp.max(s_past, axis=1, keepdims=True), jnp.max(s_new, axis=1, keepdims=True))
            p_past = jnp.exp(s_past - m).astype(BF16)
            p_new = jnp.exp(s_new - m).astype(BF16)
            v_past = cv_ref[s, _head_rows(h, past_len), :].astype(BF16)
            acc = (jnp.dot(p_past, _with_ones(v_past), preferred_element_type=F32)
                   + jnp.dot(p_new, _with_ones(vn_ref[rows, hs]), preferred_element_type=F32))
            o_ref[rows, hs] = _finish_head(acc, lam, g_ref[...], t).astype(o_ref.dtype)


def _sample_attn(q, cache_k, cache_v, kb, vb, bias_past, bias_new, lam_params, subln_g, past_len, streams):
    batch = cache_k.shape[0]
    t = q.shape[0] // batch
    small = [_resident(p.shape) for p in lam_params]
    new_rows = pl.BlockSpec((streams * t, QK_WIDTH), lambda b: (b, 0))
    cached = pl.BlockSpec((streams, past_len * N_HEADS, V_DIM), lambda b: (b, 0, 0))
    return pl.pallas_call(
        functools.partial(_sample_attn_kernel, past_len=past_len),
        grid=(batch // streams,),
        in_specs=[new_rows, cached, cached, new_rows, new_rows,
                  _resident(bias_past.shape), _resident(bias_new.shape), *small, _resident(subln_g.shape)],
        out_specs=new_rows,
        out_shape=jax.ShapeDtypeStruct(q.shape, BF16),
        compiler_params=_params("parallel"),
        name="sample_attn",
    )(q, cache_k, cache_v, kb, vb, bias_past, bias_new, *lam_params, subln_g)


def _conv_kernel(pre_ref, halo_ref, u_ref, wb_ref, b_ref, g_ref, beta_ref, c_ref, ubuf, ybuf, *, tt, rc):
    n_groups = WIDTH_B // LANES
    n_streams = u_ref.shape[0]
    n_chunks = n_streams * (tt // rc)
    for s in range(n_streams):
        history = jnp.where(pl.program_id(1) == 0, pre_ref[s], halo_ref[s])
        for k in range(tt // rc):
            for lg in range(n_groups):
                ls = slice(lg * LANES, (lg + 1) * LANES)
                slab = (s * (tt // rc) + k) * n_groups + lg
                ubuf[slab, 0:HALO, :] = history[:, ls] if k == 0 else u_ref[s, k * rc - HALO:k * rc, ls]
                ubuf[slab, HALO:, :] = u_ref[s, k * rc:(k + 1) * rc, ls]

    def conv_slab(slab, carry):
        lg = slab % n_groups
        acc = None
        for j in range(CONV_WIDTH):
            first = HALO - CONV_PAD + j
            window = ubuf[slab, first:first + rc, :].reshape(rc // SUBLANES, SUBLANES, LANES)
            term = window * wb_ref[lg, j]
            acc = term if acc is None else acc + term
        ybuf[slab] = acc.reshape(rc, LANES)
        return carry

    lax.fori_loop(0, n_chunks * n_groups, conv_slab, 0)

    for c in range(n_chunks):
        s, k = divmod(c, tt // rc)
        y = jnp.concatenate([ybuf[c * n_groups + lg] for lg in range(n_groups)], axis=1) + b_ref[...]
        mu = jnp.mean(y, axis=-1, keepdims=True)
        var = jnp.mean(jnp.square(y - mu), axis=-1, keepdims=True)
        yn = (y - mu) * lax.rsqrt(var + EPS) * g_ref[...] + beta_ref[...]
        c_ref[s, k * rc:(k + 1) * rc, :] = (yn * _sigmoid(yn)).astype(c_ref.dtype)


def _conv(prefix, u, w_rows, b_dw, ln_g, ln_b, streams, tt, rc):
    batch, seq, _ = u.shape
    halo_blocks = tt // HALO
    slabs = streams * (tt // rc) * (WIDTH_B // LANES)
    small = [_resident(p.shape) for p in (w_rows, b_dw, ln_g, ln_b)]
    return pl.pallas_call(
        functools.partial(_conv_kernel, tt=tt, rc=rc),
        grid=(batch // streams, seq // tt),
        in_specs=[pl.BlockSpec((streams, HALO, WIDTH_B), lambda b, t: (b, 0, 0)),
                  pl.BlockSpec((streams, HALO, WIDTH_B), lambda b, t: (b, jnp.maximum(t * halo_blocks - 1, 0), 0)),
                  pl.BlockSpec((streams, tt, WIDTH_B), lambda b, t: (b, t, 0)), *small],
        out_specs=pl.BlockSpec((streams, tt, WIDTH_B), lambda b, t: (b, t, 0)),
        out_shape=jax.ShapeDtypeStruct(u.shape, BF16),
        scratch_shapes=[pltpu.VMEM((slabs, HALO + rc, LANES), F32), pltpu.VMEM((slabs, rc, LANES), F32)],
        compiler_params=_params("parallel", "parallel"),
        name="conv_ln_swish",
    )(prefix, u, u, w_rows, b_dw, ln_g, ln_b)


def _out_ffn_kernel(x_ref, a_ref, c_ref, wo_ref, g2_ref, wu_ref, wd_ref, gf_ref, y_ref, *, ff_chunk):
    x = (x_ref[...]
         + jnp.dot(a_ref[...], wo_ref[:WIDTH_A, :], preferred_element_type=F32)
         + jnp.dot(c_ref[...], wo_ref[WIDTH_A:, :], preferred_element_type=F32))
    h2 = _rms_norm_rows(x, g2_ref[...]).astype(BF16)
    ffn = None
    for lo in range(0, D_FF, ff_chunk):
        hid = jnp.dot(h2, wu_ref[:, lo:lo + ff_chunk], preferred_element_type=F32)
        hid = jnp.square(jnp.maximum(hid, 0.0)).astype(BF16)
        down = jnp.dot(hid, wd_ref[lo:lo + ff_chunk, :], preferred_element_type=F32)
        ffn = down if ffn is None else ffn + down
    y_ref[...] = _rms_norm_rows(x + ffn, gf_ref[...])


def _out_ffn(x, a, c, w_out_bf, ln2_g, w_up_bf, w_down_bf, ln_f_g, tm, ff_chunk):
    n = x.shape[0]
    rows = lambda width: pl.BlockSpec((tm, width), lambda i: (i, 0))
    return pl.pallas_call(
        functools.partial(_out_ffn_kernel, ff_chunk=ff_chunk),
        grid=(n // tm,),
        in_specs=[rows(D_MODEL), rows(WIDTH_A), rows(WIDTH_B), _resident(w_out_bf.shape), _resident(ln2_g.shape),
                  _resident(w_up_bf.shape), _resident(w_down_bf.shape), _resident(ln_f_g.shape)],
        out_specs=rows(D_MODEL),
        out_shape=jax.ShapeDtypeStruct(x.shape, F32),
        compiler_params=_params("parallel"),
        name="out_ffn",
    )(x, a, c, w_out_bf, ln2_g, w_up_bf, w_down_bf, ln_f_g)


def _bucket_of(rel):
    half = N_BUCKETS // 2
    max_exact = half // 2
    n = -rel
    ret = jnp.where(n < 0, half, 0)
    n = jnp.abs(n)
    nf = jnp.maximum(n, 1).astype(F32)
    large = max_exact + (jnp.log(nf / max_exact) / math.log(MAX_DISTANCE / max_exact)
                         * (half - max_exact)).astype(jnp.int32)
    large = jnp.minimum(large, half - 1)
    return ret + jnp.where(n < max_exact, n, large)


def _bias_table(rel_bias, q0, nq, k0, nk, far_rel):
    span = nq + nk - 1
    shifted = (rel_bias - rel_bias[_bucket_of(far_rel)]).T
    hit = _bucket_of(k0 - (q0 + nq - 1) + jnp.arange(span))[:, None] == jnp.arange(N_BUCKETS)
    by_rel = jnp.sum(jnp.where(hit[None], shifted[:, None, :], 0.0), axis=-1)
    rows = jnp.pad(by_rel, ((0, 0), (0, 1)))
    rows = jnp.broadcast_to(rows[:, None, :], (N_HEADS, nq, span + 1)).reshape(N_HEADS, nq * (span + 1))
    table = rows[:, :nq * span].reshape(N_HEADS, nq, span)[:, :, nq - 1:]
    q_pos, k_pos = q0 + jnp.arange(nq), k0 + jnp.arange(nk)
    visible = (k_pos[None, :] // CHUNK) <= (q_pos[:, None] // CHUNK)
    return jnp.where(visible[None], table, NEG_INF).astype(F32)


class _Tiles(NamedTuple):
    prompt_tokens: int
    sample_tokens: int
    ff_chunk: int
    tq: int
    attn_tiles_per_step: int
    conv_rows: int
    conv_chunk: int
    sample_conv_streams: int
    sample_attn_streams: int


def _tiles(batch, seq, dec_batch, dec_seq, past_len):
    t = _Tiles(prompt_tokens=1024, sample_tokens=512, ff_chunk=1024, tq=256, attn_tiles_per_step=2,
               conv_rows=1024, conv_chunk=128, sample_conv_streams=8, sample_attn_streams=2)
    assert (batch * seq) % t.prompt_tokens == 0 and (dec_batch * dec_seq) % t.sample_tokens == 0
    assert D_FF % t.ff_chunk == 0 and seq % (t.tq * t.attn_tiles_per_step) == 0 and t.tq % LANES == 0
    assert seq % t.conv_rows == 0 and t.conv_rows % t.conv_chunk == 0 and t.conv_chunk % HALO == 0
    assert dec_batch % t.sample_conv_streams == 0 and dec_batch % t.sample_attn_streams == 0
    assert dec_seq % HALO == 0 and seq >= CONV_PAD and dec_seq >= CONV_PAD and past_len % CHUNK == 0
    return t


def kernel(x_prompt, x_sample, cache_k, cache_v, state_conv, rel_bias, ln1_g, w_in, lambda_q1, lambda_k1, lambda_q2,
           lambda_k2, subln_g, w_dw, b_dw, conv_ln_g, conv_ln_b, w_out, ln2_g, w_up, w_down, ln_f_g):
    batch, seq, _ = x_prompt.shape
    dec_batch, dec_seq, _ = x_sample.shape
    past_len = cache_k.shape[2]
    assert w_in.shape[0] == 1, "one layer"
    tiles = _tiles(batch, seq, dec_batch, dec_seq, past_len)
    tq = tiles.tq

    w_in_bf, w_out_bf = w_in[0].astype(BF16), w_out[0].astype(BF16)
    w_up_bf, w_down_bf = w_up[0].astype(BF16), w_down[0].astype(BF16)
    w_taps = w_dw[0].reshape(CONV_WIDTH, WIDTH_B // LANES, 1, LANES).transpose(1, 0, 2, 3)
    w_rows = jnp.broadcast_to(w_taps, (WIDTH_B // LANES, CONV_WIDTH, SUBLANES, LANES))
    lam_params = (lambda_q1, lambda_k1, lambda_q2, lambda_k2)

    far_rel = jnp.int32(-(2 * tq))
    bias_tail = _bias_table(rel_bias, tq, tq, 0, 2 * tq, far_rel)
    bias_past = _bias_table(rel_bias, past_len, dec_seq, 0, past_len, far_rel)
    bias_new = _bias_table(rel_bias, past_len, dec_seq, past_len, dec_seq, far_rel)

    xp = x_prompt.reshape(batch * seq, D_MODEL)
    q_p, k_p, v_p, kb_p, vb_p, u_p = _in_proj(xp, ln1_g, w_in_bf, tm=tiles.prompt_tokens)
    a_p = _prompt_attn(q_p, kb_p, vb_p, bias_tail, lam_params, subln_g, batch, seq, tq, tiles.attn_tiles_per_step)
    u_p3 = u_p.reshape(batch, seq, WIDTH_B)
    c_p = _conv(jnp.zeros((batch, HALO, WIDTH_B), F32), u_p3, w_rows, b_dw, conv_ln_g, conv_ln_b,
                streams=1, tt=tiles.conv_rows, rc=tiles.conv_chunk)
    y_p = _out_ffn(xp, a_p, c_p.reshape(batch * seq, WIDTH_B), w_out_bf, ln2_g, w_up_bf, w_down_bf,
                   ln_f_g[None, :], tm=tiles.prompt_tokens, ff_chunk=tiles.ff_chunk)

    xs = x_sample.reshape(dec_batch * dec_seq, D_MODEL)
    q_s, k_s, v_s, kb_s, vb_s, u_s = _in_proj(xs, ln1_g, w_in_bf, tm=tiles.sample_tokens)
    by_head = lambda cache: cache[0].reshape(dec_batch, past_len * N_HEADS, V_DIM)
    a_s = _sample_attn(q_s, by_head(cache_k), by_head(cache_v), kb_s, vb_s, bias_past, bias_new,
                       lam_params, subln_g, past_len, streams=tiles.sample_attn_streams)
    u_s3 = u_s.reshape(dec_batch, dec_seq, WIDTH_B)
    prefix_s = jnp.pad(state_conv[0], ((0, 0), (HALO - CONV_PAD, 0), (0, 0)))
    c_s = _conv(prefix_s, u_s3, w_rows, b_dw, conv_ln_g, conv_ln_b,
                streams=tiles.sample_conv_streams, tt=dec_seq, rc=dec_seq)
    y_s = _out_ffn(xs, a_s, c_s.reshape(dec_batch * dec_seq, WIDTH_B), w_out_bf, ln2_g, w_up_bf, w_down_bf,
                   ln_f_g[None, :], tm=tiles.sample_tokens, ff_chunk=tiles.ff_chunk)

    heads = lambda t, b, s: t.reshape(1, b, s, N_HEADS, V_DIM)
    return (y_p.reshape(batch, seq, D_MODEL), y_s.reshape(dec_batch, dec_seq, D_MODEL),
            heads(k_p, batch, seq), heads(v_p, batch, seq), u_p3[None, :, seq - CONV_PAD:],
            heads(k_s, dec_batch, dec_seq), heads(v_s, dec_batch, dec_seq), u_s3[None, :, dec_seq - CONV_PAD:])
```

```python
import functools
import math
from typing import NamedTuple

import jax
import jax.numpy as jnp
from jax import lax
from jax.experimental import pallas as pl
from jax.experimental.pallas import tpu as pltpu

D_MODEL = 1024
CHUNK = 64
N_HEADS = 4
HEAD_DIM = 64
V_DIM = 2 * HEAD_DIM
QK_WIDTH = N_HEADS * 2 * HEAD_DIM
WIDTH_A = N_HEADS * V_DIM
WIDTH_B = D_MODEL - WIDTH_A
CONV_WIDTH = 31
CONV_PAD = CONV_WIDTH - 1
D_FF = 4 * D_MODEL
N_BUCKETS = 32
MAX_DISTANCE = 128
EPS = 1e-6
NEG_INF = -1e30
LAM_INIT = 0.8 - 0.6 * math.exp(-0.3 * 0)

LANES = 128
SUBLANES = 8
HALO = 32
VMEM_LIMIT_BYTES = 56 * 1024 * 1024

F32 = jnp.float32
BF16 = jnp.bfloat16
_NT = (((1,), (1,)), ((), ()))


def _sigmoid(x):
    return 1.0 / (1.0 + jnp.exp(-x))


def _rms_norm_rows(x, g):
    return x * lax.rsqrt(jnp.mean(x * x, axis=-1, keepdims=True) + EPS) * g


def _params(*semantics):
    return pltpu.CompilerParams(dimension_semantics=semantics, vmem_limit_bytes=VMEM_LIMIT_BYTES)


def _resident(shape):
    zeros = (0,) * len(shape)
    return pl.BlockSpec(shape, lambda *_: zeros, pipeline_mode=pl.Buffered(1))


def _head_cols(h):
    return slice(h * V_DIM, (h + 1) * V_DIM)


def _head_rows(h, n):
    return pl.ds(h, n, stride=N_HEADS)


def _in_proj_kernel(x_ref, g_ref, w_ref, q_ref, k_ref, v_ref, kb_ref, vb_ref, u_ref):
    tm = x_ref.shape[0]
    h = _rms_norm_rows(x_ref[...], g_ref[...]).astype(BF16)

    def cols(lo, hi):
        return jnp.dot(h, w_ref[:, lo:hi], preferred_element_type=F32)

    a_cols = 2 * QK_WIDTH + WIDTH_A
    u_ref[...] = cols(a_cols, a_cols + WIDTH_B) * _sigmoid(cols(a_cols + WIDTH_B, a_cols + 2 * WIDTH_B))
    k, v = cols(QK_WIDTH, 2 * QK_WIDTH), cols(2 * QK_WIDTH, a_cols)
    for full, out_ref, bf_ref in ((k, k_ref, kb_ref), (v, v_ref, vb_ref)):
        bf_ref[...] = full.astype(BF16)
        for hd in range(N_HEADS):
            out_ref[_head_rows(hd, tm), :] = full[:, _head_cols(hd)]
    q_ref[...] = (cols(0, QK_WIDTH) * HEAD_DIM ** -0.5).astype(BF16)


def _in_proj(x, ln_g, w_in_bf, tm):
    n = x.shape[0]
    rows = lambda width: pl.BlockSpec((tm, width), lambda i: (i, 0))
    by_head = pl.BlockSpec((tm * N_HEADS, V_DIM), lambda i: (i, 0))
    wide = lambda dt: jax.ShapeDtypeStruct((n, QK_WIDTH), dt)
    tall = jax.ShapeDtypeStruct((n * N_HEADS, V_DIM), F32)
    return pl.pallas_call(
        _in_proj_kernel,
        grid=(n // tm,),
        in_specs=[rows(D_MODEL), _resident((1, D_MODEL)), _resident(w_in_bf.shape)],
        out_specs=[rows(QK_WIDTH), by_head, by_head, rows(QK_WIDTH), rows(WIDTH_A), rows(WIDTH_B)],
        out_shape=[wide(BF16), tall, tall, wide(BF16), wide(BF16), wide(F32)],
        compiler_params=_params("parallel"),
        name="in_proj",
    )(x, ln_g, w_in_bf)


def _lambda(lq1_ref, lk1_ref, lq2_ref, lk2_ref):
    d1 = jnp.sum(lq1_ref[...] * lk1_ref[...], axis=-1, keepdims=True)
    d2 = jnp.sum(lq2_ref[...] * lk2_ref[...], axis=-1, keepdims=True)
    return jnp.exp(d1) - jnp.exp(d2) + LAM_INIT


def _stack_maps(qh):
    lane = lax.broadcasted_iota(jnp.int32, qh.shape, 1)
    zero = jnp.zeros_like(qh)
    return jnp.concatenate([jnp.where(lane < HEAD_DIM, qh, zero), jnp.where(lane >= HEAD_DIM, qh, zero)], axis=0)


def _with_ones(v):
    return jnp.concatenate([v, jnp.ones_like(v)], axis=1)


def _finish_head(acc, lam, g, tq):
    o1 = acc[:tq, :V_DIM] / acc[:tq, V_DIM:]
    o2 = acc[tq:, :V_DIM] / acc[tq:, V_DIM:]
    o = o1 - lam * o2
    return _rms_norm_rows(o, g) * (1.0 - LAM_INIT)


def _prompt_attn_kernel(q_ref, k_ref, v_ref, bt_ref, lq1_ref, lk1_ref, lq2_ref, lk2_ref, g_ref,
                        o_ref, q2_sc, m_sc, acc_sc, *, tq):
    tiles_per_step = q_ref.shape[0] // tq

    def key_tile(slot, start, tk, bias_of_head):
        for h in range(N_HEADS):
            hs = _head_cols(h)
            s = lax.dot_general(q2_sc[slot, h], k_ref[start:start + tk, hs], _NT,
                                preferred_element_type=F32)
            if bias_of_head is not None:
                bias = bias_of_head(h)
                s = s + jnp.concatenate([bias, bias], axis=0)
            m_new = jnp.broadcast_to(jnp.max(s, axis=1, keepdims=True), (2 * tq, LANES))
            if start > 0:
                m_prev = m_sc[slot, h]
                m_new = jnp.maximum(m_prev, m_new)
                alpha = jnp.exp(m_prev - m_new)
            p = jnp.exp(s - jnp.concatenate([m_new] * (tk // LANES), axis=1))
            pv = jnp.dot(p.astype(BF16), _with_ones(v_ref[start:start + tk, hs]), preferred_element_type=F32)
            if start > 0:
                pv = jnp.concatenate([alpha, alpha], axis=1) * acc_sc[slot, h] + pv
            acc_sc[slot, h] = pv
            m_sc[slot, h] = m_new

    def query_tile(i):
        slot = i % 2
        q_rows = slice((i % tiles_per_step) * tq, (i % tiles_per_step + 1) * tq)
        for h in range(N_HEADS):
            q2_sc[slot, h] = _stack_maps(q_ref[q_rows, _head_cols(h)])
        n_far = max(i - 1, 0)
        for j in range(n_far // 2):
            key_tile(slot, j * 2 * tq, 2 * tq, None)
        if n_far % 2:
            key_tile(slot, (n_far - 1) * tq, tq, None)
        if i >= 1:
            key_tile(slot, (i - 1) * tq, 2 * tq, lambda h: bt_ref[h])
        else:
            key_tile(slot, 0, tq, lambda h: bt_ref[h, :, tq:])
        lam = _lambda(lq1_ref, lk1_ref, lq2_ref, lk2_ref)
        for h in range(N_HEADS):
            o = _finish_head(acc_sc[slot, h], lam, g_ref[...], tq)
            o_ref[q_rows, _head_cols(h)] = o.astype(o_ref.dtype)

    def step(first_tile):
        for i in range(first_tile, first_tile + tiles_per_step):
            query_tile(i)

    for c in range(k_ref.shape[0] // q_ref.shape[0]):
        pl.when(pl.program_id(1) == c)(functools.partial(step, c * tiles_per_step))


def _prompt_attn(q, kb, vb, bias_tail, lam_params, subln_g, batch, seq, tq, tiles_per_step):
    steps = seq // (tq * tiles_per_step)
    rows = tq * tiles_per_step
    small = [_resident(p.shape) for p in lam_params]
    return pl.pallas_call(
        functools.partial(_prompt_attn_kernel, tq=tq),
        grid=(batch, steps),
        in_specs=[pl.BlockSpec((rows, QK_WIDTH), lambda b, i: (b * steps + i, 0)),
                  pl.BlockSpec((seq, QK_WIDTH), lambda b, i: (b, 0)),
                  pl.BlockSpec((seq, WIDTH_A), lambda b, i: (b, 0)),
                  _resident(bias_tail.shape), *small, _resident(subln_g.shape)],
        out_specs=pl.BlockSpec((rows, WIDTH_A), lambda b, i: (b * steps + i, 0)),
        out_shape=jax.ShapeDtypeStruct((batch * seq, WIDTH_A), BF16),
        scratch_shapes=[pltpu.VMEM((2, N_HEADS, 2 * tq, V_DIM), BF16), pltpu.VMEM((2, N_HEADS, 2 * tq, LANES), F32),
                        pltpu.VMEM((2, N_HEADS, 2 * tq, 2 * V_DIM), F32)],
        compiler_params=_params("parallel", "arbitrary"),
        name="prompt_attn",
    )(q, kb, vb, bias_tail, *lam_params, subln_g)


def _sample_attn_kernel(q_ref, ck_ref, cv_ref, kn_ref, vn_ref, bp_ref, bn_ref, lq1_ref, lk1_ref, lq2_ref, lk2_ref,
                        g_ref, o_ref, *, past_len):
    n_streams = ck_ref.shape[0]
    t = q_ref.shape[0] // n_streams
    lam = _lambda(lq1_ref, lk1_ref, lq2_ref, lk2_ref)
    for s in range(n_streams):
        rows = slice(s * t, (s + 1) * t)
        for h in range(N_HEADS):
            hs = _head_cols(h)
            q2 = _stack_maps(q_ref[rows, hs])
            bias_past, bias_new = bp_ref[h], bn_ref[h]
            k_past = ck_ref[s, _head_rows(h, past_len), :].astype(BF16)
            s_past = lax.dot_general(q2, k_past, _NT, preferred_element_type=F32)
            s_past = s_past + jnp.concatenate([bias_past, bias_past], axis=0)
            s_new = lax.dot_general(q2, kn_ref[rows, hs], _NT, preferred_element_type=F32)
            s_new = s_new + jnp.concatenate([bias_new, bias_new], axis=0)
            m = jnp.maximum(jnp.max(s_past, axis=1, keepdims=True), jnp.max(s_new, axis=1, keepdims=True))
            p_past = jnp.exp(s_past - m).astype(BF16)
            p_new = jnp.exp(s_new - m).astype(BF16)
            v_past = cv_ref[s, _head_rows(h, past_len), :].astype(BF16)
            acc = (jnp.dot(p_past, _with_ones(v_past), preferred_element_type=F32)
                   + jnp.dot(p_new, _with_ones(vn_ref[rows, hs]), preferred_element_type=F32))
            o_ref[rows, hs] = _finish_head(acc, lam, g_ref[...], t).astype(o_ref.dtype)


def _sample_attn(q, cache_k, cache_v, kb, vb, bias_past, bias_new, lam_params, subln_g, past_len, streams):
    batch = cache_k.shape[0]
    t = q.shape[0] // batch
    small = [_resident(p.shape) for p in lam_params]
    new_rows = pl.BlockSpec((streams * t, QK_WIDTH), lambda b: (b, 0))
    cached = pl.BlockSpec((streams, past_len * N_HEADS, V_DIM), lambda b: (b, 0, 0))
    return pl.pallas_call(
        functools.partial(_sample_attn_kernel, past_len=past_len),
        grid=(batch // streams,),
        in_specs=[new_rows, cached, cached, new_rows, new_rows,
                  _resident(bias_past.shape), _resident(bias_new.shape), *small, _resident(subln_g.shape)],
        out_specs=new_rows,
        out_shape=jax.ShapeDtypeStruct(q.shape, BF16),
        compiler_params=_params("parallel"),
        name="sample_attn",
    )(q, cache_k, cache_v, kb, vb, bias_past, bias_new, *lam_params, subln_g)


def _conv_kernel(pre_ref, halo_ref, u_ref, wb_ref, b_ref, g_ref, beta_ref, c_ref, ubuf, ybuf, *, tt, rc):
    n_groups = WIDTH_B // LANES
    n_streams = u_ref.shape[0]
    n_chunks = n_streams * (tt // rc)
    for s in range(n_streams):
        history = jnp.where(pl.program_id(1) == 0, pre_ref[s], halo_ref[s])
        for k in range(tt // rc):
            for lg in range(n_groups):
                ls = slice(lg * LANES, (lg + 1) * LANES)
                slab = (s * (tt // rc) + k) * n_groups + lg
                ubuf[slab, 0:HALO, :] = history[:, ls] if k == 0 else u_ref[s, k * rc - HALO:k * rc, ls]
                ubuf[slab, HALO:, :] = u_ref[s, k * rc:(k + 1) * rc, ls]

    def conv_slab(slab, carry):
        lg = slab % n_groups
        acc = None
        for j in range(CONV_WIDTH):
            first = HALO - CONV_PAD + j
            window = ubuf[slab, first:first + rc, :].reshape(rc // SUBLANES, SUBLANES, LANES)
            term = window * wb_ref[lg, j]
            acc = term if acc is None else acc + term
        ybuf[slab] = acc.reshape(rc, LANES)
        return carry

    lax.fori_loop(0, n_chunks * n_groups, conv_slab, 0)

    for c in range(n_chunks):
        s, k = divmod(c, tt // rc)
        y = jnp.concatenate([ybuf[c * n_groups + lg] for lg in range(n_groups)], axis=1) + b_ref[...]
        mu = jnp.mean(y, axis=-1, keepdims=True)
        var = jnp.mean(jnp.square(y - mu), axis=-1, keepdims=True)
        yn = (y - mu) * lax.rsqrt(var + EPS) * g_ref[...] + beta_ref[...]
        c_ref[s, k * rc:(k + 1) * rc, :] = (yn * _sigmoid(yn)).astype(c_ref.dtype)


def _conv(prefix, u, w_rows, b_dw, ln_g, ln_b, streams, tt, rc):
    batch, seq, _ = u.shape
    halo_blocks = tt // HALO
    slabs = streams * (tt // rc) * (WIDTH_B // LANES)
    small = [_resident(p.shape) for p in (w_rows, b_dw, ln_g, ln_b)]
    return pl.pallas_call(
        functools.partial(_conv_kernel, tt=tt, rc=rc),
        grid=(batch // streams, seq // tt),
        in_specs=[pl.BlockSpec((streams, HALO, WIDTH_B), lambda b, t: (b, 0, 0)),
                  pl.BlockSpec((streams, HALO, WIDTH_B), lambda b, t: (b, jnp.maximum(t * halo_blocks - 1, 0), 0)),
                  pl.BlockSpec((streams, tt, WIDTH_B), lambda b, t: (b, t, 0)), *small],
        out_specs=pl.BlockSpec((streams, tt, WIDTH_B), lambda b, t: (b, t, 0)),
        out_shape=jax.ShapeDtypeStruct(u.shape, BF16),
        scratch_shapes=[pltpu.VMEM((slabs, HALO + rc, LANES), F32), pltpu.VMEM((slabs, rc, LANES), F32)],
        compiler_params=_params("parallel", "parallel"),
        name="conv_ln_swish",
    )(prefix, u, u, w_rows, b_dw, ln_g, ln_b)


def _out_ffn_kernel(x_ref, a_ref, c_ref, wo_ref, g2_ref, wu_ref, wd_ref, gf_ref, y_ref, *, ff_chunk):
    x = (x_ref[...]
         + jnp.dot(a_ref[...], wo_ref[:WIDTH_A, :], preferred_element_type=F32)
         + jnp.dot(c_ref[...], wo_ref[WIDTH_A:, :], preferred_element_type=F32))
    h2 = _rms_norm_rows(x, g2_ref[...]).astype(BF16)
    ffn = None
    for lo in range(0, D_FF, ff_chunk):
        hid = jnp.dot(h2, wu_ref[:, lo:lo + ff_chunk], preferred_element_type=F32)
        hid = jnp.square(jnp.maximum(hid, 0.0)).astype(BF16)
        down = jnp.dot(hid, wd_ref[lo:lo + ff_chunk, :], preferred_element_type=F32)
        ffn = down if ffn is None else ffn + down
    y_ref[...] = _rms_norm_rows(x + ffn, gf_ref[...])


def _out_ffn(x, a, c, w_out_bf, ln2_g, w_up_bf, w_down_bf, ln_f_g, tm, ff_chunk):
    n = x.shape[0]
    rows = lambda width: pl.BlockSpec((tm, width), lambda i: (i, 0))
    return pl.pallas_call(
        functools.partial(_out_ffn_kernel, ff_chunk=ff_chunk),
        grid=(n // tm,),
        in_specs=[rows(D_MODEL), rows(WIDTH_A), rows(WIDTH_B), _resident(w_out_bf.shape), _resident(ln2_g.shape),
                  _resident(w_up_bf.shape), _resident(w_down_bf.shape), _resident(ln_f_g.shape)],
        out_specs=rows(D_MODEL),
        out_shape=jax.ShapeDtypeStruct(x.shape, F32),
        compiler_params=_params("parallel"),
        name="out_ffn",
    )(x, a, c, w_out_bf, ln2_g, w_up_bf, w_down_bf, ln_f_g)


def _bucket_of(rel):
    half = N_BUCKETS // 2
    max_exact = half // 2
    n = -rel
    ret = jnp.where(n < 0, half, 0)
    n = jnp.abs(n)
    nf = jnp.maximum(n, 1).astype(F32)
    large = max_exact + (jnp.log(nf / max_exact) / math.log(MAX_DISTANCE / max_exact)
                         * (half - max_exact)).astype(jnp.int32)
    large = jnp.minimum(large, half - 1)
    return ret + jnp.where(n < max_exact, n, large)


def _bias_table(rel_bias, q0, nq, k0, nk, far_rel):
    span = nq + nk - 1
    shifted = (rel_bias - rel_bias[_bucket_of(far_rel)]).T
    hit = _bucket_of(k0 - (q0 + nq - 1) + jnp.arange(span))[:, None] == jnp.arange(N_BUCKETS)
    by_rel = jnp.sum(jnp.where(hit[None], shifted[:, None, :], 0.0), axis=-1)
    rows = jnp.pad(by_rel, ((0, 0), (0, 1)))
    rows = jnp.broadcast_to(rows[:, None, :], (N_HEADS, nq, span + 1)).reshape(N_HEADS, nq * (span + 1))
    table = rows[:, :nq * span].reshape(N_HEADS, nq, span)[:, :, nq - 1:]
    q_pos, k_pos = q0 + jnp.arange(nq), k0 + jnp.arange(nk)
    visible = (k_pos[None, :] // CHUNK) <= (q_pos[:, None] // CHUNK)
    return jnp.where(visible[None], table, NEG_INF).astype(F32)


class _Tiles(NamedTuple):
    prompt_tokens: int
    sample_tokens: int
    ff_chunk: int
    tq: int
    attn_tiles_per_step: int
    conv_rows: int
    conv_chunk: int
    sample_conv_streams: int
    sample_attn_streams: int


def _tiles(batch, seq, dec_batch, dec_seq, past_len):
    t = _Tiles(prompt_tokens=1024, sample_tokens=512, ff_chunk=1024, tq=256, attn_tiles_per_step=2,
               conv_rows=1024, conv_chunk=128, sample_conv_streams=8, sample_attn_streams=2)
    assert (batch * seq) % t.prompt_tokens == 0 and (dec_batch * dec_seq) % t.sample_tokens == 0
    assert D_FF % t.ff_chunk == 0 and seq % (t.tq * t.attn_tiles_per_step) == 0 and t.tq % LANES == 0
    assert seq % t.conv_rows == 0 and t.conv_rows % t.conv_chunk == 0 and t.conv_chunk % HALO == 0
    assert dec_batch % t.sample_conv_streams == 0 and dec_batch % t.sample_attn_streams == 0
    assert dec_seq % HALO == 0 and seq >= CONV_PAD and dec_seq >= CONV_PAD and past_len % CHUNK == 0
    return t


def kernel(x_prompt, x_sample, cache_k, cache_v, state_conv, rel_bias, ln1_g, w_in, lambda_q1, lambda_k1, lambda_q2,
           lambda_k2, subln_g, w_dw, b_dw, conv_ln_g, conv_ln_b, w_out, ln2_g, w_up, w_down, ln_f_g):
    batch, seq, _ = x_prompt.shape
    dec_batch, dec_seq, _ = x_sample.shape
    past_len = cache_k.shape[2]
    assert w_in.shape[0] == 1, "one layer"
    tiles = _tiles(batch, seq, dec_batch, dec_seq, past_len)
    tq = tiles.tq

    w_in_bf, w_out_bf = w_in[0].astype(BF16), w_out[0].astype(BF16)
    w_up_bf, w_down_bf = w_up[0].astype(BF16), w_down[0].astype(BF16)
    w_taps = w_dw[0].reshape(CONV_WIDTH, WIDTH_B // LANES, 1, LANES).transpose(1, 0, 2, 3)
    w_rows = jnp.broadcast_to(w_taps, (WIDTH_B // LANES, CONV_WIDTH, SUBLANES, LANES))
    lam_params = (lambda_q1, lambda_k1, lambda_q2, lambda_k2)

    far_rel = jnp.int32(-(2 * tq))
    bias_tail = _bias_table(rel_bias, tq, tq, 0, 2 * tq, far_rel)
    bias_past = _bias_table(rel_bias, past_len, dec_seq, 0, past_len, far_rel)
    bias_new = _bias_table(rel_bias, past_len, dec_seq, past_len, dec_seq, far_rel)

    xp = x_prompt.reshape(batch * seq, D_MODEL)
    q_p, k_p, v_p, kb_p, vb_p, u_p = _in_proj(xp, ln1_g, w_in_bf, tm=tiles.prompt_tokens)
    a_p = _prompt_attn(q_p, kb_p, vb_p, bias_tail, lam_params, subln_g, batch, seq, tq, tiles.attn_tiles_per_step)
    u_p3 = u_p.reshape(batch, seq, WIDTH_B)
    c_p = _conv(jnp.zeros((batch, HALO, WIDTH_B), F32), u_p3, w_rows, b_dw, conv_ln_g, conv_ln_b,
                streams=1, tt=tiles.conv_rows, rc=tiles.conv_chunk)
    y_p = _out_ffn(xp, a_p, c_p.reshape(batch * seq, WIDTH_B), w_out_bf, ln2_g, w_up_bf, w_down_bf,
                   ln_f_g[None, :], tm=tiles.prompt_tokens, ff_chunk=tiles.ff_chunk)

    xs = x_sample.reshape(dec_batch * dec_seq, D_MODEL)
    q_s, k_s, v_s, kb_s, vb_s, u_s = _in_proj(xs, ln1_g, w_in_bf, tm=tiles.sample_tokens)
    by_head = lambda cache: cache[0].reshape(dec_batch, past_len * N_HEADS, V_DIM)
    a_s = _sample_attn(q_s, by_head(cache_k), by_head(cache_v), kb_s, vb_s, bias_past, bias_new,
                       lam_params, subln_g, past_len, streams=tiles.sample_attn_streams)
    u_s3 = u_s.reshape(dec_batch, dec_seq, WIDTH_B)
    prefix_s = jnp.pad(state_conv[0], ((0, 0), (HALO - CONV_PAD, 0), (0, 0)))
    c_s = _conv(prefix_s, u_s3, w_rows, b_dw, conv_ln_g, conv_ln_b,
                streams=tiles.sample_conv_streams, tt=dec_seq, rc=dec_seq)
    y_s = _out_ffn(xs, a_s, c_s.reshape(dec_batch * dec_seq, WIDTH_B), w_out_bf, ln2_g, w_up_bf, w_down_bf,
                   ln_f_g[None, :], tm=tiles.sample_tokens, ff_chunk=tiles.ff_chunk)

    heads = lambda t, b, s: t.reshape(1, b, s, N_HEADS, V_DIM)
    return (y_p.reshape(batch, seq, D_MODEL), y_s.reshape(dec_batch, dec_seq, D_MODEL),
            heads(k_p, batch, seq), heads(v_p, batch, seq), u_p3[None, :, seq - CONV_PAD:],
            heads(k_s, dec_batch, dec_seq), heads(v_s, dec_batch, dec_seq), u_s3[None, :, dec_seq - CONV_PAD:])
```

```python
import functools
import math
from typing import NamedTuple

import jax
import jax.numpy as jnp
from jax import lax
from jax.experimental import pallas as pl
from jax.experimental.pallas import tpu as pltpu

D_MODEL = 1024
CHUNK = 64
N_HEADS = 4
HEAD_DIM = 64
V_DIM = 2 * HEAD_DIM
QK_WIDTH = N_HEADS * 2 * HEAD_DIM
WIDTH_A = N_HEADS * V_DIM
WIDTH_B = D_MODEL - WIDTH_A
CONV_WIDTH = 31
CONV_PAD = CONV_WIDTH - 1
D_FF = 4 * D_MODEL
N_BUCKETS = 32
MAX_DISTANCE = 128
EPS = 1e-6
NEG_INF = -1e30
LAM_INIT = 0.8 - 0.6 * math.exp(-0.3 * 0)

LANES = 128
SUBLANES = 8
HALO = 32
VMEM_LIMIT_BYTES = 56 * 1024 * 1024

F32 = jnp.float32
BF16 = jnp.bfloat16
_NT = (((1,), (1,)), ((), ()))


def _sigmoid(x):
    return 1.0 / (1.0 + jnp.exp(-x))


def _rms_norm_rows(x, g):
    return x * lax.rsqrt(jnp.mean(x * x, axis=-1, keepdims=True) + EPS) * g


def _params(*semantics):
    return pltpu.CompilerParams(dimension_semantics=semantics, vmem_limit_bytes=VMEM_LIMIT_BYTES)


def _resident(shape):
    zeros = (0,) * len(shape)
    return pl.BlockSpec(shape, lambda *_: zeros, pipeline_mode=pl.Buffered(1))


def _head_cols(h):
    return slice(h * V_DIM, (h + 1) * V_DIM)


def _head_rows(h, n):
    return pl.ds(h, n, stride=N_HEADS)


def _in_proj_kernel(x_ref, g_ref, w_ref, q_ref, k_ref, v_ref, kb_ref, vb_ref, u_ref):
    tm = x_ref.shape[0]
    h = _rms_norm_rows(x_ref[...], g_ref[...]).astype(BF16)

    def cols(lo, hi):
        return jnp.dot(h, w_ref[:, lo:hi], preferred_element_type=F32)

    a_cols = 2 * QK_WIDTH + WIDTH_A
    u_ref[...] = cols(a_cols, a_cols + WIDTH_B) * _sigmoid(cols(a_cols + WIDTH_B, a_cols + 2 * WIDTH_B))
    k, v = cols(QK_WIDTH, 2 * QK_WIDTH), cols(2 * QK_WIDTH, a_cols)
    for full, out_ref, bf_ref in ((k, k_ref, kb_ref), (v, v_ref, vb_ref)):
        bf_ref[...] = full.astype(BF16)
        for hd in range(N_HEADS):
            out_ref[_head_rows(hd, tm), :] = full[:, _head_cols(hd)]
    q_ref[...] = (cols(0, QK_WIDTH) * HEAD_DIM ** -0.5).astype(BF16)


def _in_proj(x, ln_g, w_in_bf, tm):
    n = x.shape[0]
    rows = lambda width: pl.BlockSpec((tm, width), lambda i: (i, 0))
    by_head = pl.BlockSpec((tm * N_HEADS, V_DIM), lambda i: (i, 0))
    wide = lambda dt: jax.ShapeDtypeStruct((n, QK_WIDTH), dt)
    tall = jax.ShapeDtypeStruct((n * N_HEADS, V_DIM), F32)
    return pl.pallas_call(
        _in_proj_kernel,
        grid=(n // tm,),
        in_specs=[rows(D_MODEL), _resident((1, D_MODEL)), _resident(w_in_bf.shape)],
        out_specs=[rows(QK_WIDTH), by_head, by_head, rows(QK_WIDTH), rows(WIDTH_A), rows(WIDTH_B)],
        out_shape=[wide(BF16), tall, tall, wide(BF16), wide(BF16), wide(F32)],
        compiler_params=_params("parallel"),
        name="in_proj",
    )(x, ln_g, w_in_bf)


def _lambda(lq1_ref, lk1_ref, lq2_ref, lk2_ref):
    d1 = jnp.sum(lq1_ref[...] * lk1_ref[...], axis=-1, keepdims=True)
    d2 = jnp.sum(lq2_ref[...] * lk2_ref[...], axis=-1, keepdims=True)
    return jnp.exp(d1) - jnp.exp(d2) + LAM_INIT


def _stack_maps(qh):
    lane = lax.broadcasted_iota(jnp.int32, qh.shape, 1)
    zero = jnp.zeros_like(qh)
    return jnp.concatenate([jnp.where(lane < HEAD_DIM, qh, zero), jnp.where(lane >= HEAD_DIM, qh, zero)], axis=0)


def _with_ones(v):
    return jnp.concatenate([v, jnp.ones_like(v)], axis=1)


def _finish_head(acc, lam, g, tq):
    o1 = acc[:tq, :V_DIM] / acc[:tq, V_DIM:]
    o2 = acc[tq:, :V_DIM] / acc[tq:, V_DIM:]
    o = o1 - lam * o2
    return _rms_norm_rows(o, g) * (1.0 - LAM_INIT)


def _prompt_attn_kernel(q_ref, k_ref, v_ref, bt_ref, lq1_ref, lk1_ref, lq2_ref, lk2_ref, g_ref,
                        o_ref, q2_sc, m_sc, acc_sc, *, tq):
    tiles_per_step = q_ref.shape[0] // tq

    def key_tile(slot, start, tk, bias_of_head):
        for h in range(N_HEADS):
            hs = _head_cols(h)
            s = lax.dot_general(q2_sc[slot, h], k_ref[start:start + tk, hs], _NT,
                                preferred_element_type=F32)
            if bias_of_head is not None:
                bias = bias_of_head(h)
                s = s + jnp.concatenate([bias, bias], axis=0)
            m_new = jnp.broadcast_to(jnp.max(s, axis=1, keepdims=True), (2 * tq, LANES))
            if start > 0:
                m_prev = m_sc[slot, h]
                m_new = jnp.maximum(m_prev, m_new)
                alpha = jnp.exp(m_prev - m_new)
            p = jnp.exp(s - jnp.concatenate([m_new] * (tk // LANES), axis=1))
            pv = jnp.dot(p.astype(BF16), _with_ones(v_ref[start:start + tk, hs]), preferred_element_type=F32)
            if start > 0:
                pv = jnp.concatenate([alpha, alpha], axis=1) * acc_sc[slot, h] + pv
            acc_sc[slot, h] = pv
            m_sc[slot, h] = m_new

    def query_tile(i):
        slot = i % 2
        q_rows = slice((i % tiles_per_step) * tq, (i % tiles_per_step + 1) * tq)
        for h in range(N_HEADS):
            q2_sc[slot, h] = _stack_maps(q_ref[q_rows, _head_cols(h)])
        n_far = max(i - 1, 0)
        for j in range(n_far // 2):
            key_tile(slot, j * 2 * tq, 2 * tq, None)
        if n_far % 2:
            key_tile(slot, (n_far - 1) * tq, tq, None)
        if i >= 1:
            key_tile(slot, (i - 1) * tq, 2 * tq, lambda h: bt_ref[h])
        else:
            key_tile(slot, 0, tq, lambda h: bt_ref[h, :, tq:])
        lam = _lambda(lq1_ref, lk1_ref, lq2_ref, lk2_ref)
        for h in range(N_HEADS):
            o = _finish_head(acc_sc[slot, h], lam, g_ref[...], tq)
            o_ref[q_rows, _head_cols(h)] = o.astype(o_ref.dtype)

    def step(first_tile):
        for i in range(first_tile, first_tile + tiles_per_step):
            query_tile(i)

    for c in range(k_ref.shape[0] // q_ref.shape[0]):
        pl.when(pl.program_id(1) == c)(functools.partial(step, c * tiles_per_step))


def _prompt_attn(q, kb, vb, bias_tail, lam_params, subln_g, batch, seq, tq, tiles_per_step):
    steps = seq // (tq * tiles_per_step)
    rows = tq * tiles_per_step
    small = [_resident(p.shape) for p in lam_params]
    return pl.pallas_call(
        functools.partial(_prompt_attn_kernel, tq=tq),
        grid=(batch, steps),
        in_specs=[pl.BlockSpec((rows, QK_WIDTH), lambda b, i: (b * steps + i, 0)),
                  pl.BlockSpec((seq, QK_WIDTH), lambda b, i: (b, 0)),
                  pl.BlockSpec((seq, WIDTH_A), lambda b, i: (b, 0)),
                  _resident(bias_tail.shape), *small, _resident(subln_g.shape)],
        out_specs=pl.BlockSpec((rows, WIDTH_A), lambda b, i: (b * steps + i, 0)),
        out_shape=jax.ShapeDtypeStruct((batch * seq, WIDTH_A), BF16),
        scratch_shapes=[pltpu.VMEM((2, N_HEADS, 2 * tq, V_DIM), BF16), pltpu.VMEM((2, N_HEADS, 2 * tq, LANES), F32),
                        pltpu.VMEM((2, N_HEADS, 2 * tq, 2 * V_DIM), F32)],
        compiler_params=_params("parallel", "arbitrary"),
        name="prompt_attn",
    )(q, kb, vb, bias_tail, *lam_params, subln_g)


def _sample_attn_kernel(q_ref, ck_ref, cv_ref, kn_ref, vn_ref, bp_ref, bn_ref, lq1_ref, lk1_ref, lq2_ref, lk2_ref,
                        g_ref, o_ref, *, past_len):
    n_streams = ck_ref.shape[0]
    t = q_ref.shape[0] // n_streams
    lam = _lambda(lq1_ref, lk1_ref, lq2_ref, lk2_ref)
    for s in range(n_streams):
        rows = slice(s * t, (s + 1) * t)
        for h in range(N_HEADS):
            hs = _head_cols(h)
            q2 = _stack_maps(q_ref[rows, hs])
            bias_past, bias_new = bp_ref[h], bn_ref[h]
            k_past = ck_ref[s, _head_rows(h, past_len), :].astype(BF16)
            s_past = lax.dot_general(q2, k_past, _NT, preferred_element_type=F32)
            s_past = s_past + jnp.concatenate([bias_past, bias_past], axis=0)
            s_new = lax.dot_general(q2, kn_ref[rows, hs], _NT, preferred_element_type=F32)
            s_new = s_new + jnp.concatenate([bias_new, bias_new], axis=0)
            m = jnp.maximum(jnp.max(s_past, axis=1, keepdims=True), jnp.max(s_new, axis=1, keepdims=True))
            p_past = jnp.exp(s_past - m).astype(BF16)
            p_new = jnp.exp(s_new - m).astype(BF16)
            v_past = cv_ref[s, _head_rows(h, past_len), :].astype(BF16)
            acc = (jnp.dot(p_past, _with_ones(v_past), preferred_element_type=F32)
                   + jnp.dot(p_new, _with_ones(vn_ref[rows, hs]), preferred_element_type=F32))
            o_ref[rows, hs] = _finish_head(acc, lam, g_ref[...], t).astype(o_ref.dtype)


def _sample_attn(q, cache_k, cache_v, kb, vb, bias_past, bias_new, lam_params, subln_g, past_len, streams):
    batch = cache_k.shape[0]
    t = q.shape[0] // batch
    small = [_resident(p.shape) for p in lam_params]
    new_rows = pl.BlockSpec((streams * t, QK_WIDTH), lambda b: (b, 0))
    cached = pl.BlockSpec((streams, past_len * N_HEADS, V_DIM), lambda b: (b, 0, 0))
    return pl.pallas_call(
        functools.partial(_sample_attn_kernel, past_len=past_len),
        grid=(batch // streams,),
        in_specs=[new_rows, cached, cached, new_rows, new_rows,
                  _resident(bias_past.shape), _resident(bias_new.shape), *small, _resident(subln_g.shape)],
        out_specs=new_rows,
        out_shape=jax.ShapeDtypeStruct(q.shape, BF16),
        compiler_params=_params("parallel"),
        name="sample_attn",
    )(q, cache_k, cache_v, kb, vb, bias_past, bias_new, *lam_params, subln_g)


def _conv_kernel(pre_ref, halo_ref, u_ref, wb_ref, b_ref, g_ref, beta_ref, c_ref, ubuf, ybuf, *, tt, rc):
    n_groups = WIDTH_B // LANES
    n_streams = u_ref.shape[0]
    n_chunks = n_streams * (tt // rc)
    for s in range(n_streams):
        history = jnp.where(pl.program_id(1) == 0, pre_ref[s], halo_ref[s])
        for k in range(tt // rc):
            for lg in range(n_groups):
                ls = slice(lg * LANES, (lg + 1) * LANES)
                slab = (s * (tt // rc) + k) * n_groups + lg
                ubuf[slab, 0:HALO, :] = history[:, ls] if k == 0 else u_ref[s, k * rc - HALO:k * rc, ls]
                ubuf[slab, HALO:, :] = u_ref[s, k * rc:(k + 1) * rc, ls]

    def conv_slab(slab, carry):
        lg = slab % n_groups
        acc = None
        for j in range(CONV_WIDTH):
            first = HALO - CONV_PAD + j
            window = ubuf[slab, first:first + rc, :].reshape(rc // SUBLANES, SUBLANES, LANES)
            term = window * wb_ref[lg, j]
            acc = term if acc is None else acc + term
        ybuf[slab] = acc.reshape(rc, LANES)
        return carry

    lax.fori_loop(0, n_chunks * n_groups, conv_slab, 0, unroll=n_groups)

    for c in range(n_chunks):
        s, k = divmod(c, tt // rc)
        y = jnp.concatenate([ybuf[c * n_groups + lg] for lg in range(n_groups)], axis=1) + b_ref[...]
        mu = jnp.mean(y, axis=-1, keepdims=True)
        var = jnp.mean(jnp.square(y - mu), axis=-1, keepdims=True)
        yn = (y - mu) * lax.rsqrt(var + EPS) * g_ref[...] + beta_ref[...]
        c_ref[s, k * rc:(k + 1) * rc, :] = (yn * _sigmoid(yn)).astype(c_ref.dtype)


def _conv(prefix, u, w_rows, b_dw, ln_g, ln_b, streams, tt, rc):
    batch, seq, _ = u.shape
    halo_blocks = tt // HALO
    slabs = streams * (tt // rc) * (WIDTH_B // LANES)
    small = [_resident(p.shape) for p in (w_rows, b_dw, ln_g, ln_b)]
    return pl.pallas_call(
        functools.partial(_conv_kernel, tt=tt, rc=rc),
        grid=(batch // streams, seq // tt),
        in_specs=[pl.BlockSpec((streams, HALO, WIDTH_B), lambda b, t: (b, 0, 0)),
                  pl.BlockSpec((streams, HALO, WIDTH_B), lambda b, t: (b, jnp.maximum(t * halo_blocks - 1, 0), 0)),
                  pl.BlockSpec((streams, tt, WIDTH_B), lambda b, t: (b, t, 0)), *small],
        out_specs=pl.BlockSpec((streams, tt, WIDTH_B), lambda b, t: (b, t, 0)),
        out_shape=jax.ShapeDtypeStruct(u.shape, BF16),
        scratch_shapes=[pltpu.VMEM((slabs, HALO + rc, LANES), F32), pltpu.VMEM((slabs, rc, LANES), F32)],
        compiler_params=_params("parallel", "parallel"),
        name="conv_ln_swish",
    )(prefix, u, u, w_rows, b_dw, ln_g, ln_b)


def _out_ffn_kernel(x_ref, a_ref, c_ref, wo_ref, g2_ref, wu_ref, wd_ref, gf_ref, y_ref, *, ff_chunk):
    x = (x_ref[...]
         + jnp.dot(a_ref[...], wo_ref[:WIDTH_A, :], preferred_element_type=F32)
         + jnp.dot(c_ref[...], wo_ref[WIDTH_A:, :], preferred_element_type=F32))
    h2 = _rms_norm_rows(x, g2_ref[...]).astype(BF16)
    ffn = None
    for lo in range(0, D_FF, ff_chunk):
        hid = jnp.dot(h2, wu_ref[:, lo:lo + ff_chunk], preferred_element_type=F32)
        hid = jnp.square(jnp.maximum(hid, 0.0)).astype(BF16)
        down = jnp.dot(hid, wd_ref[lo:lo + ff_chunk, :], preferred_element_type=F32)
        ffn = down if ffn is None else ffn + down
    y_ref[...] = _rms_norm_rows(x + ffn, gf_ref[...])


def _out_ffn(x, a, c, w_out_bf, ln2_g, w_up_bf, w_down_bf, ln_f_g, tm, ff_chunk):
    n = x.shape[0]
    rows = lambda width: pl.BlockSpec((tm, width), lambda i: (i, 0))
    return pl.pallas_call(
        functools.partial(_out_ffn_kernel, ff_chunk=ff_chunk),
        grid=(n // tm,),
        in_specs=[rows(D_MODEL), rows(WIDTH_A), rows(WIDTH_B), _resident(w_out_bf.shape), _resident(ln2_g.shape),
                  _resident(w_up_bf.shape), _resident(w_down_bf.shape), _resident(ln_f_g.shape)],
        out_specs=rows(D_MODEL),
        out_shape=jax.ShapeDtypeStruct(x.shape, F32),
        compiler_params=_params("parallel"),
        name="out_ffn",
    )(x, a, c, w_out_bf, ln2_g, w_up_bf, w_down_bf, ln_f_g)


def _bucket_of(rel):
    half = N_BUCKETS // 2
    max_exact = half // 2
    n = -rel
    ret = jnp.where(n < 0, half, 0)
    n = jnp.abs(n)
    nf = jnp.maximum(n, 1).astype(F32)
    large = max_exact + (jnp.log(nf / max_exact) / math.log(MAX_DISTANCE / max_exact)
                         * (half - max_exact)).astype(jnp.int32)
    large = jnp.minimum(large, half - 1)
    return ret + jnp.where(n < max_exact, n, large)


def _bias_table(rel_bias, q0, nq, k0, nk, far_rel):
    span = nq + nk - 1
    shifted = (rel_bias - rel_bias[_bucket_of(far_rel)]).T
    hit = _bucket_of(k0 - (q0 + nq - 1) + jnp.arange(span))[:, None] == jnp.arange(N_BUCKETS)
    by_rel = jnp.sum(jnp.where(hit[None], shifted[:, None, :], 0.0), axis=-1)
    rows = jnp.pad(by_rel, ((0, 0), (0, 1)))
    rows = jnp.broadcast_to(rows[:, None, :], (N_HEADS, nq, span + 1)).reshape(N_HEADS, nq * (span + 1))
    table = rows[:, :nq * span].reshape(N_HEADS, nq, span)[:, :, nq - 1:]
    q_pos, k_pos = q0 + jnp.arange(nq), k0 + jnp.arange(nk)
    visible = (k_pos[None, :] // CHUNK) <= (q_pos[:, None] // CHUNK)
    return jnp.where(visible[None], table, NEG_INF).astype(F32)


class _Tiles(NamedTuple):
    prompt_tokens: int
    sample_tokens: int
    ff_chunk: int
    tq: int
    attn_tiles_per_step: int
    conv_rows: int
    conv_chunk: int
    sample_conv_streams: int
    sample_attn_streams: int


def _tiles(batch, seq, dec_batch, dec_seq, past_len):
    t = _Tiles(prompt_tokens=1024, sample_tokens=512, ff_chunk=1024, tq=256, attn_tiles_per_step=2,
               conv_rows=1024, conv_chunk=128, sample_conv_streams=8, sample_attn_streams=2)
    assert (batch * seq) % t.prompt_tokens == 0 and (dec_batch * dec_seq) % t.sample_tokens == 0
    assert D_FF % t.ff_chunk == 0 and seq % (t.tq * t.attn_tiles_per_step) == 0 and t.tq % LANES == 0
    assert seq % t.conv_rows == 0 and t.conv_rows % t.conv_chunk == 0 and t.conv_chunk % HALO == 0
    assert dec_batch % t.sample_conv_streams == 0 and dec_batch % t.sample_attn_streams == 0
    assert dec_seq % HALO == 0 and seq >= CONV_PAD and dec_seq >= CONV_PAD and past_len % CHUNK == 0
    return t


def kernel(x_prompt, x_sample, cache_k, cache_v, state_conv, rel_bias, ln1_g, w_in, lambda_q1, lambda_k1, lambda_q2,
           lambda_k2, subln_g, w_dw, b_dw, conv_ln_g, conv_ln_b, w_out, ln2_g, w_up, w_down, ln_f_g):
    batch, seq, _ = x_prompt.shape
    dec_batch, dec_seq, _ = x_sample.shape
    past_len = cache_k.shape[2]
    assert w_in.shape[0] == 1, "one layer"
    tiles = _tiles(batch, seq, dec_batch, dec_seq, past_len)
    tq = tiles.tq

    w_in_bf, w_out_bf = w_in[0].astype(BF16), w_out[0].astype(BF16)
    w_up_bf, w_down_bf = w_up[0].astype(BF16), w_down[0].astype(BF16)
    w_taps = w_dw[0].reshape(CONV_WIDTH, WIDTH_B // LANES, 1, LANES).transpose(1, 0, 2, 3)
    w_rows = jnp.broadcast_to(w_taps, (WIDTH_B // LANES, CONV_WIDTH, SUBLANES, LANES))
    lam_params = (lambda_q1, lambda_k1, lambda_q2, lambda_k2)

    far_rel = jnp.int32(-(2 * tq))
    bias_tail = _bias_table(rel_bias, tq, tq, 0, 2 * tq, far_rel)
    bias_past = _bias_table(rel_bias, past_len, dec_seq, 0, past_len, far_rel)
    bias_new = _bias_table(rel_bias, past_len, dec_seq, past_len, dec_seq, far_rel)

    xp = x_prompt.reshape(batch * seq, D_MODEL)
    q_p, k_p, v_p, kb_p, vb_p, u_p = _in_proj(xp, ln1_g, w_in_bf, tm=tiles.prompt_tokens)
    a_p = _prompt_attn(q_p, kb_p, vb_p, bias_tail, lam_params, subln_g, batch, seq, tq, tiles.attn_tiles_per_step)
    u_p3 = u_p.reshape(batch, seq, WIDTH_B)
    c_p = _conv(jnp.zeros((batch, HALO, WIDTH_B), F32), u_p3, w_rows, b_dw, conv_ln_g, conv_ln_b,
                streams=1, tt=tiles.conv_rows, rc=tiles.conv_chunk)
    y_p = _out_ffn(xp, a_p, c_p.reshape(batch * seq, WIDTH_B), w_out_bf, ln2_g, w_up_bf, w_down_bf,
                   ln_f_g[None, :], tm=tiles.prompt_tokens, ff_chunk=tiles.ff_chunk)

    xs = x_sample.reshape(dec_batch * dec_seq, D_MODEL)
    q_s, k_s, v_s, kb_s, vb_s, u_s = _in_proj(xs, ln1_g, w_in_bf, tm=tiles.sample_tokens)
    by_head = lambda cache: cache[0].reshape(dec_batch, past_len * N_HEADS, V_DIM)
    a_s = _sample_attn(q_s, by_head(cache_k), by_head(cache_v), kb_s, vb_s, bias_past, bias_new,
                       lam_params, subln_g, past_len, streams=tiles.sample_attn_streams)
    u_s3 = u_s.reshape(dec_batch, dec_seq, WIDTH_B)
    prefix_s = jnp.pad(state_conv[0], ((0, 0), (HALO - CONV_PAD, 0), (0, 0)))
    c_s = _conv(prefix_s, u_s3, w_rows, b_dw, conv_ln_g, conv_ln_b,
                streams=tiles.sample_conv_streams, tt=dec_seq, rc=dec_seq)
    y_s = _out_ffn(xs, a_s, c_s.reshape(dec_batch * dec_seq, WIDTH_B), w_out_bf, ln2_g, w_up_bf, w_down_bf,
                   ln_f_g[None, :], tm=tiles.sample_tokens, ff_chunk=tiles.ff_chunk)

    heads = lambda t, b, s: t.reshape(1, b, s, N_HEADS, V_DIM)
    return (y_p.reshape(batch, seq, D_MODEL), y_s.reshape(dec_batch, dec_seq, D_MODEL),
            heads(k_p, batch, seq), heads(v_p, batch, seq), u_p3[None, :, seq - CONV_PAD:],
            heads(k_s, dec_batch, dec_seq), heads(v_s, dec_batch, dec_seq), u_s3[None, :, dec_seq - CONV_PAD:])
```

```python
import functools
import math
from typing import NamedTuple

import jax
import jax.numpy as jnp
from jax import lax
from jax.experimental import pallas as pl
from jax.experimental.pallas import tpu as pltpu

D_MODEL = 1024
CHUNK = 64
N_HEADS = 4
HEAD_DIM = 64
V_DIM = 2 * HEAD_DIM
QK_WIDTH = N_HEADS * 2 * HEAD_DIM
WIDTH_A = N_HEADS * V_DIM
WIDTH_B = D_MODEL - WIDTH_A
CONV_WIDTH = 31
CONV_PAD = CONV_WIDTH - 1
D_FF = 4 * D_MODEL
N_BUCKETS = 32
MAX_DISTANCE = 128
EPS = 1e-6
NEG_INF = -1e30
LAM_INIT = 0.8 - 0.6 * math.exp(-0.3 * 0)

LANES = 128
SUBLANES = 8
HALO = 32
VMEM_LIMIT_BYTES = 56 * 1024 * 1024

F32 = jnp.float32
BF16 = jnp.bfloat16
_NT = (((1,), (1,)), ((), ()))


def _sigmoid(x):
    return 1.0 / (1.0 + jnp.exp(-x))


def _rms_norm_rows(x, g):
    return x * lax.rsqrt(jnp.mean(x * x, axis=-1, keepdims=True) + EPS) * g


def _params(*semantics):
    return pltpu.CompilerParams(dimension_semantics=semantics, vmem_limit_bytes=VMEM_LIMIT_BYTES)


def _resident(shape):
    zeros = (0,) * len(shape)
    return pl.BlockSpec(shape, lambda *_: zeros, pipeline_mode=pl.Buffered(1))


def _head_cols(h):
    return slice(h * V_DIM, (h + 1) * V_DIM)


def _head_rows(h, n):
    return pl.ds(h, n, stride=N_HEADS)


def _in_proj_kernel(x_ref, g_ref, w_ref, q_ref, k_ref, v_ref, kb_ref, vb_ref, u_ref):
    tm = x_ref.shape[0]
    h = _rms_norm_rows(x_ref[...], g_ref[...]).astype(BF16)

    def cols(lo, hi):
        return jnp.dot(h, w_ref[:, lo:hi], preferred_element_type=F32)

    a_cols = 2 * QK_WIDTH + WIDTH_A
    u_ref[...] = cols(a_cols, a_cols + WIDTH_B) * _sigmoid(cols(a_cols + WIDTH_B, a_cols + 2 * WIDTH_B))
    k, v = cols(QK_WIDTH, 2 * QK_WIDTH), cols(2 * QK_WIDTH, a_cols)
    for full, out_ref, bf_ref in ((k, k_ref, kb_ref), (v, v_ref, vb_ref)):
        bf_ref[...] = full.astype(BF16)
        for hd in range(N_HEADS):
            out_ref[_head_rows(hd, tm), :] = full[:, _head_cols(hd)]
    q_ref[...] = (cols(0, QK_WIDTH) * HEAD_DIM ** -0.5).astype(BF16)


def _in_proj(x, ln_g, w_in_bf, tm):
    n = x.shape[0]
    rows = lambda width: pl.BlockSpec((tm, width), lambda i: (i, 0))
    by_head = pl.BlockSpec((tm * N_HEADS, V_DIM), lambda i: (i, 0))
    wide = lambda dt: jax.ShapeDtypeStruct((n, QK_WIDTH), dt)
    tall = jax.ShapeDtypeStruct((n * N_HEADS, V_DIM), F32)
    return pl.pallas_call(
        _in_proj_kernel,
        grid=(n // tm,),
        in_specs=[rows(D_MODEL), _resident((1, D_MODEL)), _resident(w_in_bf.shape)],
        out_specs=[rows(QK_WIDTH), by_head, by_head, rows(QK_WIDTH), rows(WIDTH_A), rows(WIDTH_B)],
        out_shape=[wide(BF16), tall, tall, wide(BF16), wide(BF16), wide(F32)],
        compiler_params=_params("parallel"),
        name="in_proj",
    )(x, ln_g, w_in_bf)


def _lambda(lq1_ref, lk1_ref, lq2_ref, lk2_ref):
    d1 = jnp.sum(lq1_ref[...] * lk1_ref[...], axis=-1, keepdims=True)
    d2 = jnp.sum(lq2_ref[...] * lk2_ref[...], axis=-1, keepdims=True)
    return jnp.exp(d1) - jnp.exp(d2) + LAM_INIT


def _stack_maps(qh):
    lane = lax.broadcasted_iota(jnp.int32, qh.shape, 1)
    zero = jnp.zeros_like(qh)
    return jnp.concatenate([jnp.where(lane < HEAD_DIM, qh, zero), jnp.where(lane >= HEAD_DIM, qh, zero)], axis=0)


def _with_ones(v):
    return jnp.concatenate([v, jnp.ones_like(v)], axis=1)


def _finish_head(acc, lam, g, tq):
    o1 = acc[:tq, :V_DIM] / acc[:tq, V_DIM:]
    o2 = acc[tq:, :V_DIM] / acc[tq:, V_DIM:]
    o = o1 - lam * o2
    return _rms_norm_rows(o, g) * (1.0 - LAM_INIT)


def _prompt_attn_kernel(q_ref, k_ref, v_ref, bt_ref, lq1_ref, lk1_ref, lq2_ref, lk2_ref, g_ref,
                        o_ref, q2_sc, m_sc, acc_sc, *, tq):
    tiles_per_step = q_ref.shape[0] // tq

    def key_tile(slot, start, tk, bias_of_head):
        for h in range(N_HEADS):
            hs = _head_cols(h)
            s = lax.dot_general(q2_sc[slot, h], k_ref[start:start + tk, hs], _NT,
                                preferred_element_type=F32)
            if bias_of_head is not None:
                bias = bias_of_head(h)
                s = s + jnp.concatenate([bias, bias], axis=0)
            m_new = jnp.broadcast_to(jnp.max(s, axis=1, keepdims=True), (2 * tq, LANES))
            if start > 0:
                m_prev = m_sc[slot, h]
                m_new = jnp.maximum(m_prev, m_new)
                alpha = jnp.exp(m_prev - m_new)
            p = jnp.exp(s - jnp.concatenate([m_new] * (tk // LANES), axis=1))
            pv = jnp.dot(p.astype(BF16), _with_ones(v_ref[start:start + tk, hs]), preferred_element_type=F32)
            if start > 0:
                pv = jnp.concatenate([alpha, alpha], axis=1) * acc_sc[slot, h] + pv
            acc_sc[slot, h] = pv
            m_sc[slot, h] = m_new

    def query_tile(i):
        slot = i % 2
        q_rows = slice((i % tiles_per_step) * tq, (i % tiles_per_step + 1) * tq)
        for h in range(N_HEADS):
            q2_sc[slot, h] = _stack_maps(q_ref[q_rows, _head_cols(h)])
        n_far = max(i - 1, 0)
        for j in range(n_far // 2):
            key_tile(slot, j * 2 * tq, 2 * tq, None)
        if n_far % 2:
            key_tile(slot, (n_far - 1) * tq, tq, None)
        if i >= 1:
            key_tile(slot, (i - 1) * tq, 2 * tq, lambda h: bt_ref[h])
        else:
            key_tile(slot, 0, tq, lambda h: bt_ref[h, :, tq:])
        lam = _lambda(lq1_ref, lk1_ref, lq2_ref, lk2_ref)
        for h in range(N_HEADS):
            o = _finish_head(acc_sc[slot, h], lam, g_ref[...], tq)
            o_ref[q_rows, _head_cols(h)] = o.astype(o_ref.dtype)

    def step(first_tile):
        for i in range(first_tile, first_tile + tiles_per_step):
            query_tile(i)

    for c in range(k_ref.shape[0] // q_ref.shape[0]):
        pl.when(pl.program_id(1) == c)(functools.partial(step, c * tiles_per_step))


def _prompt_attn(q, kb, vb, bias_tail, lam_params, subln_g, batch, seq, tq, tiles_per_step):
    steps = seq // (tq * tiles_per_step)
    rows = tq * tiles_per_step
    small = [_resident(p.shape) for p in lam_params]
    return pl.pallas_call(
        functools.partial(_prompt_attn_kernel, tq=tq),
        grid=(batch, steps),
        in_specs=[pl.BlockSpec((rows, QK_WIDTH), lambda b, i: (b * steps + i, 0)),
                  pl.BlockSpec((seq, QK_WIDTH), lambda b, i: (b, 0)),
                  pl.BlockSpec((seq, WIDTH_A), lambda b, i: (b, 0)),
                  _resident(bias_tail.shape), *small, _resident(subln_g.shape)],
        out_specs=pl.BlockSpec((rows, WIDTH_A), lambda b, i: (b * steps + i, 0)),
        out_shape=jax.ShapeDtypeStruct((batch * seq, WIDTH_A), BF16),
        scratch_shapes=[pltpu.VMEM((2, N_HEADS, 2 * tq, V_DIM), BF16), pltpu.VMEM((2, N_HEADS, 2 * tq, LANES), F32),
                        pltpu.VMEM((2, N_HEADS, 2 * tq, 2 * V_DIM), F32)],
        compiler_params=_params("parallel", "arbitrary"),
        name="prompt_attn",
    )(q, kb, vb, bias_tail, *lam_params, subln_g)


def _sample_attn_kernel(q_ref, ck_ref, cv_ref, kn_ref, vn_ref, bp_ref, bn_ref, lq1_ref, lk1_ref, lq2_ref, lk2_ref,
                        g_ref, o_ref, *, past_len):
    n_streams = ck_ref.shape[0]
    t = q_ref.shape[0] // n_streams
    lam = _lambda(lq1_ref, lk1_ref, lq2_ref, lk2_ref)
    for s in range(n_streams):
        rows = slice(s * t, (s + 1) * t)
        for h in range(N_HEADS):
            hs = _head_cols(h)
            q2 = _stack_maps(q_ref[rows, hs])
            bias_past, bias_new = bp_ref[h], bn_ref[h]
            k_past = ck_ref[s, _head_rows(h, past_len), :].astype(BF16)
            s_past = lax.dot_general(q2, k_past, _NT, preferred_element_type=F32)
            s_past = s_past + jnp.concatenate([bias_past, bias_past], axis=0)
            s_new = lax.dot_general(q2, kn_ref[rows, hs], _NT, preferred_element_type=F32)
            s_new = s_new + jnp.concatenate([bias_new, bias_new], axis=0)
            m = jnp.maximum(jnp.max(s_past, axis=1, keepdims=True), jnp.max(s_new, axis=1, keepdims=True))
            p_past = jnp.exp(s_past - m).astype(BF16)
            p_new = jnp.exp(s_new - m).astype(BF16)
            v_past = cv_ref[s, _head_rows(h, past_len), :].astype(BF16)
            acc = (jnp.dot(p_past, _with_ones(v_past), preferred_element_type=F32)
                   + jnp.dot(p_new, _with_ones(vn_ref[rows, hs]), preferred_element_type=F32))
            o_ref[rows, hs] = _finish_head(acc, lam, g_ref[...], t).astype(o_ref.dtype)


def _sample_attn(q, cache_k, cache_v, kb, vb, bias_past, bias_new, lam_params, subln_g, past_len, streams):
    batch = cache_k.shape[0]
    t = q.shape[0] // batch
    small = [_resident(p.shape) for p in lam_params]
    new_rows = pl.BlockSpec((streams * t, QK_WIDTH), lambda b: (b, 0))
    cached = pl.BlockSpec((streams, past_len * N_HEADS, V_DIM), lambda b: (b, 0, 0))
    return pl.pallas_call(
        functools.partial(_sample_attn_kernel, past_len=past_len),
        grid=(batch // streams,),
        in_specs=[new_rows, cached, cached, new_rows, new_rows,
                  _resident(bias_past.shape), _resident(bias_new.shape), *small, _resident(subln_g.shape)],
        out_specs=new_rows,
        out_shape=jax.ShapeDtypeStruct(q.shape, BF16),
        compiler_params=_params("parallel"),
        name="sample_attn",
    )(q, cache_k, cache_v, kb, vb, bias_past, bias_new, *lam_params, subln_g)


def _conv_kernel(pre_ref, halo_ref, u_ref, wb_ref, b_ref, g_ref, beta_ref, c_ref, ubuf, ybuf, *, tt, rc):
    n_groups = WIDTH_B // LANES
    n_streams = u_ref.shape[0]
    n_chunks = n_streams * (tt // rc)
    for s in range(n_streams):
        history = jnp.where(pl.program_id(1) == 0, pre_ref[s], halo_ref[s])
        for k in range(tt // rc):
            for lg in range(n_groups):
                ls = slice(lg * LANES, (lg + 1) * LANES)
                slab = (s * (tt // rc) + k) * n_groups + lg
                ubuf[slab, 0:HALO, :] = history[:, ls] if k == 0 else u_ref[s, k * rc - HALO:k * rc, ls]
                ubuf[slab, HALO:, :] = u_ref[s, k * rc:(k + 1) * rc, ls]

    def conv_slab(slab, carry):
        lg = slab % n_groups
        acc = None
        for j in range(CONV_WIDTH):
            first = HALO - CONV_PAD + j
            window = ubuf[slab, first:first + rc, :].reshape(rc // SUBLANES, SUBLANES, LANES)
            term = window * wb_ref[lg, j]
            acc = term if acc is None else acc + term
        ybuf[slab] = acc.reshape(rc, LANES)
        return carry

    lax.fori_loop(0, n_chunks * n_groups, conv_slab, 0, unroll=n_groups)

    for c in range(n_chunks):
        s, k = divmod(c, tt // rc)
        y = jnp.concatenate([ybuf[c * n_groups + lg] for lg in range(n_groups)], axis=1) + b_ref[...]
        mu = jnp.mean(y, axis=-1, keepdims=True)
        var = jnp.mean(jnp.square(y - mu), axis=-1, keepdims=True)
        yn = (y - mu) * lax.rsqrt(var + EPS) * g_ref[...] + beta_ref[...]
        c_ref[s, k * rc:(k + 1) * rc, :] = (yn * _sigmoid(yn)).astype(c_ref.dtype)


def _conv(prefix, u, w_rows, b_dw, ln_g, ln_b, streams, tt, rc):
    batch, seq, _ = u.shape
    halo_blocks = tt // HALO
    slabs = streams * (tt // rc) * (WIDTH_B // LANES)
    small = [_resident(p.shape) for p in (w_rows, b_dw, ln_g, ln_b)]
    return pl.pallas_call(
        functools.partial(_conv_kernel, tt=tt, rc=rc),
        grid=(batch // streams, seq // tt),
        in_specs=[pl.BlockSpec((streams, HALO, WIDTH_B), lambda b, t: (b, 0, 0)),
                  pl.BlockSpec((streams, HALO, WIDTH_B), lambda b, t: (b, jnp.maximum(t * halo_blocks - 1, 0), 0)),
                  pl.BlockSpec((streams, tt, WIDTH_B), lambda b, t: (b, t, 0)), *small],
        out_specs=pl.BlockSpec((streams, tt, WIDTH_B), lambda b, t: (b, t, 0)),
        out_shape=jax.ShapeDtypeStruct(u.shape, BF16),
        scratch_shapes=[pltpu.VMEM((slabs, HALO + rc, LANES), F32), pltpu.VMEM((slabs, rc, LANES), F32)],
        compiler_params=_params("parallel", "parallel"),
        name="conv_ln_swish",
    )(prefix, u, u, w_rows, b_dw, ln_g, ln_b)


def _out_ffn_kernel(x_ref, a_ref, c_ref, wo_ref, g2_ref, wu_ref, wd_ref, gf_ref, y_ref, *, ff_chunk):
    x = (x_ref[...]
         + jnp.dot(a_ref[...], wo_ref[:WIDTH_A, :], preferred_element_type=F32)
         + jnp.dot(c_ref[...], wo_ref[WIDTH_A:, :], preferred_element_type=F32))
    h2 = _rms_norm_rows(x, g2_ref[...]).astype(BF16)
    ffn = None
    for lo in range(0, D_FF, ff_chunk):
        hid = jnp.dot(h2, wu_ref[:, lo:lo + ff_chunk], preferred_element_type=F32)
        hid = jnp.square(jnp.maximum(hid, 0.0)).astype(BF16)
        down = jnp.dot(hid, wd_ref[lo:lo + ff_chunk, :], preferred_element_type=F32)
        ffn = down if ffn is None else ffn + down
    y_ref[...] = _rms_norm_rows(x + ffn, gf_ref[...])


def _out_ffn(x, a, c, w_out_bf, ln2_g, w_up_bf, w_down_bf, ln_f_g, tm, ff_chunk):
    n = x.shape[0]
    rows = lambda width: pl.BlockSpec((tm, width), lambda i: (i, 0))
    return pl.pallas_call(
        functools.partial(_out_ffn_kernel, ff_chunk=ff_chunk),
        grid=(n // tm,),
        in_specs=[rows(D_MODEL), rows(WIDTH_A), rows(WIDTH_B), _resident(w_out_bf.shape), _resident(ln2_g.shape),
                  _resident(w_up_bf.shape), _resident(w_down_bf.shape), _resident(ln_f_g.shape)],
        out_specs=rows(D_MODEL),
        out_shape=jax.ShapeDtypeStruct(x.shape, F32),
        compiler_params=_params("parallel"),
        name="out_ffn",
    )(x, a, c, w_out_bf, ln2_g, w_up_bf, w_down_bf, ln_f_g)


def _bucket_of(rel):
    half = N_BUCKETS // 2
    max_exact = half // 2
    n = -rel
    ret = jnp.where(n < 0, half, 0)
    n = jnp.abs(n)
    nf = jnp.maximum(n, 1).astype(F32)
    large = max_exact + (jnp.log(nf / max_exact) / math.log(MAX_DISTANCE / max_exact)
                         * (half - max_exact)).astype(jnp.int32)
    large = jnp.minimum(large, half - 1)
    return ret + jnp.where(n < max_exact, n, large)


def _bias_table(rel_bias, q0, nq, k0, nk, far_rel):
    span = nq + nk - 1
    shifted = (rel_bias - rel_bias[_bucket_of(far_rel)]).T
    hit = _bucket_of(k0 - (q0 + nq - 1) + jnp.arange(span))[:, None] == jnp.arange(N_BUCKETS)
    by_rel = jnp.sum(jnp.where(hit[None], shifted[:, None, :], 0.0), axis=-1)
    rows = jnp.pad(by_rel, ((0, 0), (0, 1)))
    rows = jnp.broadcast_to(rows[:, None, :], (N_HEADS, nq, span + 1)).reshape(N_HEADS, nq * (span + 1))
    table = rows[:, :nq * span].reshape(N_HEADS, nq, span)[:, :, nq - 1:]
    q_pos, k_pos = q0 + jnp.arange(nq), k0 + jnp.arange(nk)
    visible = (k_pos[None, :] // CHUNK) <= (q_pos[:, None] // CHUNK)
    return jnp.where(visible[None], table, NEG_INF).astype(F32)


class _Tiles(NamedTuple):
    prompt_tokens: int
    sample_tokens: int
    ff_chunk: int
    tq: int
    attn_tiles_per_step: int
    conv_rows: int
    conv_chunk: int
    sample_conv_streams: int
    sample_attn_streams: int


def _tiles(batch, seq, dec_batch, dec_seq, past_len):
    t = _Tiles(prompt_tokens=1024, sample_tokens=512, ff_chunk=1024, tq=256, attn_tiles_per_step=4,
               conv_rows=1024, conv_chunk=128, sample_conv_streams=8, sample_attn_streams=2)
    assert (batch * seq) % t.prompt_tokens == 0 and (dec_batch * dec_seq) % t.sample_tokens == 0
    assert D_FF % t.ff_chunk == 0 and seq % (t.tq * t.attn_tiles_per_step) == 0 and t.tq % LANES == 0
    assert seq % t.conv_rows == 0 and t.conv_rows % t.conv_chunk == 0 and t.conv_chunk % HALO == 0
    assert dec_batch % t.sample_conv_streams == 0 and dec_batch % t.sample_attn_streams == 0
    assert dec_seq % HALO == 0 and seq >= CONV_PAD and dec_seq >= CONV_PAD and past_len % CHUNK == 0
    return t


def kernel(x_prompt, x_sample, cache_k, cache_v, state_conv, rel_bias, ln1_g, w_in, lambda_q1, lambda_k1, lambda_q2,
           lambda_k2, subln_g, w_dw, b_dw, conv_ln_g, conv_ln_b, w_out, ln2_g, w_up, w_down, ln_f_g):
    batch, seq, _ = x_prompt.shape
    dec_batch, dec_seq, _ = x_sample.shape
    past_len = cache_k.shape[2]
    assert w_in.shape[0] == 1, "one layer"
    tiles = _tiles(batch, seq, dec_batch, dec_seq, past_len)
    tq = tiles.tq

    w_in_bf, w_out_bf = w_in[0].astype(BF16), w_out[0].astype(BF16)
    w_up_bf, w_down_bf = w_up[0].astype(BF16), w_down[0].astype(BF16)
    w_taps = w_dw[0].reshape(CONV_WIDTH, WIDTH_B // LANES, 1, LANES).transpose(1, 0, 2, 3)
    w_rows = jnp.broadcast_to(w_taps, (WIDTH_B // LANES, CONV_WIDTH, SUBLANES, LANES))
    lam_params = (lambda_q1, lambda_k1, lambda_q2, lambda_k2)

    far_rel = jnp.int32(-(2 * tq))
    bias_tail = _bias_table(rel_bias, tq, tq, 0, 2 * tq, far_rel)
    bias_past = _bias_table(rel_bias, past_len, dec_seq, 0, past_len, far_rel)
    bias_new = _bias_table(rel_bias, past_len, dec_seq, past_len, dec_seq, far_rel)

    xp = x_prompt.reshape(batch * seq, D_MODEL)
    q_p, k_p, v_p, kb_p, vb_p, u_p = _in_proj(xp, ln1_g, w_in_bf, tm=tiles.prompt_tokens)
    a_p = _prompt_attn(q_p, kb_p, vb_p, bias_tail, lam_params, subln_g, batch, seq, tq, tiles.attn_tiles_per_step)
    u_p3 = u_p.reshape(batch, seq, WIDTH_B)
    c_p = _conv(jnp.zeros((batch, HALO, WIDTH_B), F32), u_p3, w_rows, b_dw, conv_ln_g, conv_ln_b,
                streams=1, tt=tiles.conv_rows, rc=tiles.conv_chunk)
    y_p = _out_ffn(xp, a_p, c_p.reshape(batch * seq, WIDTH_B), w_out_bf, ln2_g, w_up_bf, w_down_bf,
                   ln_f_g[None, :], tm=tiles.prompt_tokens, ff_chunk=tiles.ff_chunk)

    xs = x_sample.reshape(dec_batch * dec_seq, D_MODEL)
    q_s, k_s, v_s, kb_s, vb_s, u_s = _in_proj(xs, ln1_g, w_in_bf, tm=tiles.sample_tokens)
    by_head = lambda cache: cache[0].reshape(dec_batch, past_len * N_HEADS, V_DIM)
    a_s = _sample_attn(q_s, by_head(cache_k), by_head(cache_v), kb_s, vb_s, bias_past, bias_new,
                       lam_params, subln_g, past_len, streams=tiles.sample_attn_streams)
    u_s3 = u_s.reshape(dec_batch, dec_seq, WIDTH_B)
    prefix_s = jnp.pad(state_conv[0], ((0, 0), (HALO - CONV_PAD, 0), (0, 0)))
    c_s = _conv(prefix_s, u_s3, w_rows, b_dw, conv_ln_g, conv_ln_b,
                streams=tiles.sample_conv_streams, tt=dec_seq, rc=dec_seq)
    y_s = _out_ffn(xs, a_s, c_s.reshape(dec_batch * dec_seq, WIDTH_B), w_out_bf, ln2_g, w_up_bf, w_down_bf,
                   ln_f_g[None, :], tm=tiles.sample_tokens, ff_chunk=tiles.ff_chunk)

    heads = lambda t, b, s: t.reshape(1, b, s, N_HEADS, V_DIM)
    return (y_p.reshape(batch, seq, D_MODEL), y_s.reshape(dec_batch, dec_seq, D_MODEL),
            heads(k_p, batch, seq), heads(v_p, batch, seq), u_p3[None, :, seq - CONV_PAD:],
            heads(k_s, dec_batch, dec_seq), heads(v_s, dec_batch, dec_seq), u_s3[None, :, dec_seq - CONV_PAD:])
```

```python
import functools
import math
from typing import NamedTuple

import jax
import jax.numpy as jnp
from jax import lax
from jax.experimental import pallas as pl
from jax.experimental.pallas import tpu as pltpu

D_MODEL = 1024
CHUNK = 64
N_HEADS = 4
HEAD_DIM = 64
V_DIM = 2 * HEAD_DIM
QK_WIDTH = N_HEADS * 2 * HEAD_DIM
WIDTH_A = N_HEADS * V_DIM
WIDTH_B = D_MODEL - WIDTH_A
CONV_WIDTH = 31
CONV_PAD = CONV_WIDTH - 1
D_FF = 4 * D_MODEL
N_BUCKETS = 32
MAX_DISTANCE = 128
EPS = 1e-6
NEG_INF = -1e30
LAM_INIT = 0.8 - 0.6 * math.exp(-0.3 * 0)

LANES = 128
SUBLANES = 8
VREGS = 64
HALO = 32
VMEM_LIMIT_BYTES = 56 * 1024 * 1024

F32 = jnp.float32
BF16 = jnp.bfloat16
_NT = (((1,), (1,)), ((), ()))


def _sigmoid(x):
    return 1.0 / (1.0 + jnp.exp(-x))


def _rms_norm_rows(x, g):
    return x * lax.rsqrt(jnp.mean(x * x, axis=-1, keepdims=True) + EPS) * g


def _params(*semantics):
    return pltpu.CompilerParams(dimension_semantics=semantics, vmem_limit_bytes=VMEM_LIMIT_BYTES)


def _resident(shape):
    zeros = (0,) * len(shape)
    return pl.BlockSpec(shape, lambda *_: zeros, pipeline_mode=pl.Buffered(1))


def _head_cols(h):
    return slice(h * V_DIM, (h + 1) * V_DIM)


def _head_rows(h, n):
    return pl.ds(h, n, stride=N_HEADS)


def _in_proj_kernel(x_ref, g_ref, w_ref, q_ref, k_ref, v_ref, kb_ref, vb_ref, u_ref):
    tm = x_ref.shape[0]
    h = _rms_norm_rows(x_ref[...], g_ref[...]).astype(BF16)

    def cols(lo, hi):
        return jnp.dot(h, w_ref[:, lo:hi], preferred_element_type=F32)

    a_cols = 2 * QK_WIDTH + WIDTH_A
    u_ref[...] = cols(a_cols, a_cols + WIDTH_B) * _sigmoid(cols(a_cols + WIDTH_B, a_cols + 2 * WIDTH_B))
    k, v = cols(QK_WIDTH, 2 * QK_WIDTH), cols(2 * QK_WIDTH, a_cols)
    for full, out_ref, bf_ref in ((k, k_ref, kb_ref), (v, v_ref, vb_ref)):
        bf_ref[...] = full.astype(BF16)
        for hd in range(N_HEADS):
            out_ref[_head_rows(hd, tm), :] = full[:, _head_cols(hd)]
    q_ref[...] = (cols(0, QK_WIDTH) * HEAD_DIM ** -0.5).astype(BF16)


def _in_proj(x, ln_g, w_in_bf, tm):
    n = x.shape[0]
    rows = lambda width: pl.BlockSpec((tm, width), lambda i: (i, 0))
    by_head = pl.BlockSpec((tm * N_HEADS, V_DIM), lambda i: (i, 0))
    wide = lambda dt: jax.ShapeDtypeStruct((n, QK_WIDTH), dt)
    tall = jax.ShapeDtypeStruct((n * N_HEADS, V_DIM), F32)
    return pl.pallas_call(
        _in_proj_kernel,
        grid=(n // tm,),
        in_specs=[rows(D_MODEL), _resident((1, D_MODEL)), _resident(w_in_bf.shape)],
        out_specs=[rows(QK_WIDTH), by_head, by_head, rows(QK_WIDTH), rows(WIDTH_A), rows(WIDTH_B)],
        out_shape=[wide(BF16), tall, tall, wide(BF16), wide(BF16), wide(F32)],
        compiler_params=_params("parallel"),
        name="in_proj",
    )(x, ln_g, w_in_bf)


def _lambda(lq1_ref, lk1_ref, lq2_ref, lk2_ref):
    d1 = jnp.sum(lq1_ref[...] * lk1_ref[...], axis=-1, keepdims=True)
    d2 = jnp.sum(lq2_ref[...] * lk2_ref[...], axis=-1, keepdims=True)
    return jnp.exp(d1) - jnp.exp(d2) + LAM_INIT


def _stack_maps(qh):
    lane = lax.broadcasted_iota(jnp.int32, qh.shape, 1)
    zero = jnp.zeros_like(qh)
    return jnp.concatenate([jnp.where(lane < HEAD_DIM, qh, zero), jnp.where(lane >= HEAD_DIM, qh, zero)], axis=0)


def _with_ones(v):
    return jnp.concatenate([v, jnp.ones_like(v)], axis=1)


def _finish_head(acc, lam, g, tq):
    o1 = acc[:tq, :V_DIM] / acc[:tq, V_DIM:]
    o2 = acc[tq:, :V_DIM] / acc[tq:, V_DIM:]
    o = o1 - lam * o2
    return _rms_norm_rows(o, g) * (1.0 - LAM_INIT)


def _prompt_attn_kernel(q_ref, k_ref, v_ref, bt_ref, lq1_ref, lk1_ref, lq2_ref, lk2_ref, g_ref,
                        o_ref, q2_sc, m_sc, acc_sc, *, tq):
    tiles_per_step = q_ref.shape[0] // tq

    def key_tile(slot, start, tk, bias_of_head):
        for h in range(N_HEADS):
            hs = _head_cols(h)
            s = lax.dot_general(q2_sc[slot, h], k_ref[start:start + tk, hs], _NT,
                                preferred_element_type=F32)
            if bias_of_head is not None:
                bias = bias_of_head(h)
                s = s + jnp.concatenate([bias, bias], axis=0)
            m_new = jnp.broadcast_to(jnp.max(s, axis=1, keepdims=True), (2 * tq, LANES))
            if start > 0:
                m_prev = m_sc[slot, h]
                m_new = jnp.maximum(m_prev, m_new)
                alpha = jnp.exp(m_prev - m_new)
            p = jnp.exp(s - jnp.concatenate([m_new] * (tk // LANES), axis=1))
            pv = jnp.dot(p.astype(BF16), _with_ones(v_ref[start:start + tk, hs]), preferred_element_type=F32)
            if start > 0:
                pv = jnp.concatenate([alpha, alpha], axis=1) * acc_sc[slot, h] + pv
            acc_sc[slot, h] = pv
            m_sc[slot, h] = m_new

    def query_tile(i):
        slot = i % 2
        q_rows = slice((i % tiles_per_step) * tq, (i % tiles_per_step + 1) * tq)
        for h in range(N_HEADS):
            q2_sc[slot, h] = _stack_maps(q_ref[q_rows, _head_cols(h)])
        n_far = max(i - 1, 0)
        for j in range(n_far // 2):
            key_tile(slot, j * 2 * tq, 2 * tq, None)
        if n_far % 2:
            key_tile(slot, (n_far - 1) * tq, tq, None)
        if i >= 1:
            key_tile(slot, (i - 1) * tq, 2 * tq, lambda h: bt_ref[h])
        else:
            key_tile(slot, 0, tq, lambda h: bt_ref[h, :, tq:])
        lam = _lambda(lq1_ref, lk1_ref, lq2_ref, lk2_ref)
        for h in range(N_HEADS):
            o = _finish_head(acc_sc[slot, h], lam, g_ref[...], tq)
            o_ref[q_rows, _head_cols(h)] = o.astype(o_ref.dtype)

    def step(first_tile):
        for i in range(first_tile, first_tile + tiles_per_step):
            query_tile(i)

    for c in range(k_ref.shape[0] // q_ref.shape[0]):
        pl.when(pl.program_id(1) == c)(functools.partial(step, c * tiles_per_step))


def _prompt_attn(q, kb, vb, bias_tail, lam_params, subln_g, batch, seq, tq, tiles_per_step):
    steps = seq // (tq * tiles_per_step)
    rows = tq * tiles_per_step
    small = [_resident(p.shape) for p in lam_params]
    return pl.pallas_call(
        functools.partial(_prompt_attn_kernel, tq=tq),
        grid=(batch, steps),
        in_specs=[pl.BlockSpec((rows, QK_WIDTH), lambda b, i: (b * steps + i, 0)),
                  pl.BlockSpec((seq, QK_WIDTH), lambda b, i: (b, 0)),
                  pl.BlockSpec((seq, WIDTH_A), lambda b, i: (b, 0)),
                  _resident(bias_tail.shape), *small, _resident(subln_g.shape)],
        out_specs=pl.BlockSpec((rows, WIDTH_A), lambda b, i: (b * steps + i, 0)),
        out_shape=jax.ShapeDtypeStruct((batch * seq, WIDTH_A), BF16),
        scratch_shapes=[pltpu.VMEM((2, N_HEADS, 2 * tq, V_DIM), BF16), pltpu.VMEM((2, N_HEADS, 2 * tq, LANES), F32),
                        pltpu.VMEM((2, N_HEADS, 2 * tq, 2 * V_DIM), F32)],
        compiler_params=_params("parallel", "arbitrary"),
        name="prompt_attn",
    )(q, kb, vb, bias_tail, *lam_params, subln_g)


def _sample_attn_kernel(q_ref, ck_ref, cv_ref, kn_ref, vn_ref, bp_ref, bn_ref, lq1_ref, lk1_ref, lq2_ref, lk2_ref,
                        g_ref, o_ref, *, past_len):
    n_streams = ck_ref.shape[0]
    t = q_ref.shape[0] // n_streams
    lam = _lambda(lq1_ref, lk1_ref, lq2_ref, lk2_ref)
    for s in range(n_streams):
        rows = slice(s * t, (s + 1) * t)
        for h in range(N_HEADS):
            hs = _head_cols(h)
            q2 = _stack_maps(q_ref[rows, hs])
            bias_past, bias_new = bp_ref[h], bn_ref[h]
            k_past = ck_ref[s, _head_rows(h, past_len), :].astype(BF16)
            s_past = lax.dot_general(q2, k_past, _NT, preferred_element_type=F32)
            far = past_len - bias_past.shape[1]
            s_recent = s_past[:, far:] + jnp.concatenate([bias_past, bias_past], axis=0)
            s_past = jnp.concatenate([s_past[:, :far], s_recent], axis=1)
            s_new = lax.dot_general(q2, kn_ref[rows, hs], _NT, preferred_element_type=F32)
            s_new = s_new + jnp.concatenate([bias_new, bias_new], axis=0)
            m = jnp.maximum(jnp.max(s_past, axis=1, keepdims=True), jnp.max(s_new, axis=1, keepdims=True))
            p_past = jnp.exp(s_past - m).astype(BF16)
            p_new = jnp.exp(s_new - m).astype(BF16)
            v_past = cv_ref[s, _head_rows(h, past_len), :].astype(BF16)
            acc = (jnp.dot(p_past, _with_ones(v_past), preferred_element_type=F32)
                   + jnp.dot(p_new, _with_ones(vn_ref[rows, hs]), preferred_element_type=F32))
            o_ref[rows, hs] = _finish_head(acc, lam, g_ref[...], t).astype(o_ref.dtype)


def _sample_attn(q, cache_k, cache_v, kb, vb, bias_past, bias_new, lam_params, subln_g, past_len, streams):
    batch = cache_k.shape[0]
    t = q.shape[0] // batch
    small = [_resident(p.shape) for p in lam_params]
    new_rows = pl.BlockSpec((streams * t, QK_WIDTH), lambda b: (b, 0))
    cached = pl.BlockSpec((streams, past_len * N_HEADS, V_DIM), lambda b: (b, 0, 0))
    return pl.pallas_call(
        functools.partial(_sample_attn_kernel, past_len=past_len),
        grid=(batch // streams,),
        in_specs=[new_rows, cached, cached, new_rows, new_rows,
                  _resident(bias_past.shape), _resident(bias_new.shape), *small, _resident(subln_g.shape)],
        out_specs=new_rows,
        out_shape=jax.ShapeDtypeStruct(q.shape, BF16),
        compiler_params=_params("parallel"),
        name="sample_attn",
    )(q, cache_k, cache_v, kb, vb, bias_past, bias_new, *lam_params, subln_g)


def _conv_kernel(pre_ref, halo_ref, u_ref, wb_ref, b_ref, g_ref, beta_ref, c_ref, ubuf, ybuf, *, tt, rc):
    n_groups = WIDTH_B // LANES
    n_streams = u_ref.shape[0]
    n_chunks = n_streams * (tt // rc)
    for s in range(n_streams):
        history = jnp.where(pl.program_id(1) == 0, pre_ref[s], halo_ref[s])
        for k in range(tt // rc):
            for lg in range(n_groups):
                ls = slice(lg * LANES, (lg + 1) * LANES)
                slab = (s * (tt // rc) + k) * n_groups + lg
                ubuf[slab, 0:HALO, :] = history[:, ls] if k == 0 else u_ref[s, k * rc - HALO:k * rc, ls]
                ubuf[slab, HALO:, :] = u_ref[s, k * rc:(k + 1) * rc, ls]

    def conv_slab(slab, carry):
        lg = slab % n_groups
        acc = None
        for j in range(CONV_WIDTH):
            first = HALO - CONV_PAD + j
            window = ubuf[slab, first:first + rc, :].reshape(rc // SUBLANES, SUBLANES, LANES)
            term = window * wb_ref[lg, j]
            acc = term if acc is None else acc + term
        ybuf[slab] = acc.reshape(rc, LANES)
        return carry

    unroll = min(n_groups, max(1, VREGS // (rc // SUBLANES)))
    lax.fori_loop(0, n_chunks * n_groups, conv_slab, 0, unroll=unroll)

    for c in range(n_chunks):
        s, k = divmod(c, tt // rc)
        y = jnp.concatenate([ybuf[c * n_groups + lg] for lg in range(n_groups)], axis=1) + b_ref[...]
        mu = jnp.mean(y, axis=-1, keepdims=True)
        var = jnp.mean(jnp.square(y - mu), axis=-1, keepdims=True)
        yn = (y - mu) * lax.rsqrt(var + EPS) * g_ref[...] + beta_ref[...]
        c_ref[s, k * rc:(k + 1) * rc, :] = (yn * _sigmoid(yn)).astype(c_ref.dtype)


def _conv(prefix, u, w_rows, b_dw, ln_g, ln_b, streams, tt, rc):
    batch, seq, _ = u.shape
    halo_blocks = tt // HALO
    slabs = streams * (tt // rc) * (WIDTH_B // LANES)
    small = [_resident(p.shape) for p in (w_rows, b_dw, ln_g, ln_b)]
    return pl.pallas_call(
        functools.partial(_conv_kernel, tt=tt, rc=rc),
        grid=(batch // streams, seq // tt),
        in_specs=[pl.BlockSpec((streams, HALO, WIDTH_B), lambda b, t: (b, 0, 0)),
                  pl.BlockSpec((streams, HALO, WIDTH_B), lambda b, t: (b, jnp.maximum(t * halo_blocks - 1, 0), 0)),
                  pl.BlockSpec((streams, tt, WIDTH_B), lambda b, t: (b, t, 0)), *small],
        out_specs=pl.BlockSpec((streams, tt, WIDTH_B), lambda b, t: (b, t, 0)),
        out_shape=jax.ShapeDtypeStruct(u.shape, BF16),
        scratch_shapes=[pltpu.VMEM((slabs, HALO + rc, LANES), F32), pltpu.VMEM((slabs, rc, LANES), F32)],
        compiler_params=_params("parallel", "parallel"),
        name="conv_ln_swish",
    )(prefix, u, u, w_rows, b_dw, ln_g, ln_b)


def _out_ffn_kernel(x_ref, a_ref, c_ref, wo_ref, g2_ref, wu_ref, wd_ref, gf_ref, y_ref, *, ff_chunk):
    x = (x_ref[...]
         + jnp.dot(a_ref[...], wo_ref[:WIDTH_A, :], preferred_element_type=F32)
         + jnp.dot(c_ref[...], wo_ref[WIDTH_A:, :], preferred_element_type=F32))
    h2 = _rms_norm_rows(x, g2_ref[...]).astype(BF16)
    ffn = None
    for lo in range(0, D_FF, ff_chunk):
        hid = jnp.dot(h2, wu_ref[:, lo:lo + ff_chunk], preferred_element_type=F32)
        hid = jnp.square(jnp.maximum(hid, 0.0)).astype(BF16)
        down = jnp.dot(hid, wd_ref[lo:lo + ff_chunk, :], preferred_element_type=F32)
        ffn = down if ffn is None else ffn + down
    y_ref[...] = _rms_norm_rows(x + ffn, gf_ref[...])


def _out_ffn(x, a, c, w_out_bf, ln2_g, w_up_bf, w_down_bf, ln_f_g, tm, ff_chunk):
    n = x.shape[0]
    rows = lambda width: pl.BlockSpec((tm, width), lambda i: (i, 0))
    return pl.pallas_call(
        functools.partial(_out_ffn_kernel, ff_chunk=ff_chunk),
        grid=(n // tm,),
        in_specs=[rows(D_MODEL), rows(WIDTH_A), rows(WIDTH_B), _resident(w_out_bf.shape), _resident(ln2_g.shape),
                  _resident(w_up_bf.shape), _resident(w_down_bf.shape), _resident(ln_f_g.shape)],
        out_specs=rows(D_MODEL),
        out_shape=jax.ShapeDtypeStruct(x.shape, F32),
        compiler_params=_params("parallel"),
        name="out_ffn",
    )(x, a, c, w_out_bf, ln2_g, w_up_bf, w_down_bf, ln_f_g)


def _bucket_of(rel):
    half = N_BUCKETS // 2
    max_exact = half // 2
    n = -rel
    ret = jnp.where(n < 0, half, 0)
    n = jnp.abs(n)
    nf = jnp.maximum(n, 1).astype(F32)
    large = max_exact + (jnp.log(nf / max_exact) / math.log(MAX_DISTANCE / max_exact)
                         * (half - max_exact)).astype(jnp.int32)
    large = jnp.minimum(large, half - 1)
    return ret + jnp.where(n < max_exact, n, large)


def _bias_table(rel_bias, q0, nq, k0, nk, far_rel):
    span = nq + nk - 1
    shifted = (rel_bias - rel_bias[_bucket_of(far_rel)]).T
    hit = _bucket_of(k0 - (q0 + nq - 1) + jnp.arange(span))[:, None] == jnp.arange(N_BUCKETS)
    by_rel = jnp.sum(jnp.where(hit[None], shifted[:, None, :], 0.0), axis=-1)
    rows = jnp.pad(by_rel, ((0, 0), (0, 1)))
    rows = jnp.broadcast_to(rows[:, None, :], (N_HEADS, nq, span + 1)).reshape(N_HEADS, nq * (span + 1))
    table = rows[:, :nq * span].reshape(N_HEADS, nq, span)[:, :, nq - 1:]
    q_pos, k_pos = q0 + jnp.arange(nq), k0 + jnp.arange(nk)
    visible = (k_pos[None, :] // CHUNK) <= (q_pos[:, None] // CHUNK)
    return jnp.where(visible[None], table, NEG_INF).astype(F32)


class _Tiles(NamedTuple):
    prompt_tokens: int
    sample_tokens: int
    ff_chunk: int
    tq: int
    attn_tiles_per_step: int
    conv_rows: int
    conv_chunk: int
    sample_conv_streams: int
    sample_attn_streams: int


def _tiles(batch, seq, dec_batch, dec_seq, past_len):
    t = _Tiles(prompt_tokens=1024, sample_tokens=1024, ff_chunk=1024, tq=256, attn_tiles_per_step=2,
               conv_rows=1024, conv_chunk=256, sample_conv_streams=8, sample_attn_streams=2)
    assert past_len >= MAX_DISTANCE and MAX_DISTANCE % LANES == 0
    assert (batch * seq) % t.prompt_tokens == 0 and (dec_batch * dec_seq) % t.sample_tokens == 0
    assert D_FF % t.ff_chunk == 0 and seq % (t.tq * t.attn_tiles_per_step) == 0 and t.tq % LANES == 0
    assert seq % t.conv_rows == 0 and t.conv_rows % t.conv_chunk == 0 and t.conv_chunk % HALO == 0
    assert dec_batch % t.sample_conv_streams == 0 and dec_batch % t.sample_attn_streams == 0
    assert dec_seq % HALO == 0 and seq >= CONV_PAD and dec_seq >= CONV_PAD and past_len % CHUNK == 0
    return t


def kernel(x_prompt, x_sample, cache_k, cache_v, state_conv, rel_bias, ln1_g, w_in, lambda_q1, lambda_k1, lambda_q2,
           lambda_k2, subln_g, w_dw, b_dw, conv_ln_g, conv_ln_b, w_out, ln2_g, w_up, w_down, ln_f_g):
    batch, seq, _ = x_prompt.shape
    dec_batch, dec_seq, _ = x_sample.shape
    past_len = cache_k.shape[2]
    assert w_in.shape[0] == 1, "one layer"
    tiles = _tiles(batch, seq, dec_batch, dec_seq, past_len)
    tq = tiles.tq

    w_in_bf, w_out_bf = w_in[0].astype(BF16), w_out[0].astype(BF16)
    w_up_bf, w_down_bf = w_up[0].astype(BF16), w_down[0].astype(BF16)
    w_taps = w_dw[0].reshape(CONV_WIDTH, WIDTH_B // LANES, 1, LANES).transpose(1, 0, 2, 3)
    w_rows = jnp.broadcast_to(w_taps, (WIDTH_B // LANES, CONV_WIDTH, SUBLANES, LANES))
    lam_params = (lambda_q1, lambda_k1, lambda_q2, lambda_k2)

    far_rel = jnp.int32(-(2 * tq))
    bias_tail = _bias_table(rel_bias, tq, tq, 0, 2 * tq, far_rel)
    bias_past = _bias_table(rel_bias, past_len, dec_seq, past_len - MAX_DISTANCE, MAX_DISTANCE, far_rel)
    bias_new = _bias_table(rel_bias, past_len, dec_seq, past_len, dec_seq, far_rel)

    xp = x_prompt.reshape(batch * seq, D_MODEL)
    q_p, k_p, v_p, kb_p, vb_p, u_p = _in_proj(xp, ln1_g, w_in_bf, tm=tiles.prompt_tokens)
    a_p = _prompt_attn(q_p, kb_p, vb_p, bias_tail, lam_params, subln_g, batch, seq, tq, tiles.attn_tiles_per_step)
    u_p3 = u_p.reshape(batch, seq, WIDTH_B)
    c_p = _conv(jnp.zeros((batch, HALO, WIDTH_B), F32), u_p3, w_rows, b_dw, conv_ln_g, conv_ln_b,
                streams=1, tt=tiles.conv_rows, rc=tiles.conv_chunk)
    y_p = _out_ffn(xp, a_p, c_p.reshape(batch * seq, WIDTH_B), w_out_bf, ln2_g, w_up_bf, w_down_bf,
                   ln_f_g[None, :], tm=tiles.prompt_tokens, ff_chunk=tiles.ff_chunk)

    xs = x_sample.reshape(dec_batch * dec_seq, D_MODEL)
    q_s, k_s, v_s, kb_s, vb_s, u_s = _in_proj(xs, ln1_g, w_in_bf, tm=tiles.sample_tokens)
    by_head = lambda cache: cache[0].reshape(dec_batch, past_len * N_HEADS, V_DIM)
    a_s = _sample_attn(q_s, by_head(cache_k), by_head(cache_v), kb_s, vb_s, bias_past, bias_new,
                       lam_params, subln_g, past_len, streams=tiles.sample_attn_streams)
    u_s3 = u_s.reshape(dec_batch, dec_seq, WIDTH_B)
    prefix_s = jnp.pad(state_conv[0], ((0, 0), (HALO - CONV_PAD, 0), (0, 0)))
    c_s = _conv(prefix_s, u_s3, w_rows, b_dw, conv_ln_g, conv_ln_b,
                streams=tiles.sample_conv_streams, tt=dec_seq, rc=dec_seq)
    y_s = _out_ffn(xs, a_s, c_s.reshape(dec_batch * dec_seq, WIDTH_B), w_out_bf, ln2_g, w_up_bf, w_down_bf,
                   ln_f_g[None, :], tm=tiles.sample_tokens, ff_chunk=tiles.ff_chunk)

    heads = lambda t, b, s: t.reshape(1, b, s, N_HEADS, V_DIM)
    return (y_p.reshape(batch, seq, D_MODEL), y_s.reshape(dec_batch, dec_seq, D_MODEL),
            heads(k_p, batch, seq), heads(v_p, batch, seq), u_p3[None, :, seq - CONV_PAD:],
            heads(k_s, dec_batch, dec_seq), heads(v_s, dec_batch, dec_seq), u_s3[None, :, dec_seq - CONV_PAD:])
```

```python
import functools
import math
from typing import NamedTuple

import jax
import jax.numpy as jnp
from jax import lax
from jax.experimental import pallas as pl
from jax.experimental.pallas import tpu as pltpu

D_MODEL = 1024
CHUNK = 64
N_HEADS = 4
HEAD_DIM = 64
V_DIM = 2 * HEAD_DIM
QK_WIDTH = N_HEADS * 2 * HEAD_DIM
WIDTH_A = N_HEADS * V_DIM
WIDTH_B = D_MODEL - WIDTH_A
CONV_WIDTH = 31
CONV_PAD = CONV_WIDTH - 1
D_FF = 4 * D_MODEL
N_BUCKETS = 32
MAX_DISTANCE = 128
EPS = 1e-6
NEG_INF = -1e30
LAM_INIT = 0.8 - 0.6 * math.exp(-0.3 * 0)

LANES = 128
SUBLANES = 8
VREGS = 64
HALO = 32
VMEM_LIMIT_BYTES = 56 * 1024 * 1024

F32 = jnp.float32
BF16 = jnp.bfloat16
_NT = (((1,), (1,)), ((), ()))


def _sigmoid(x):
    return 1.0 / (1.0 + jnp.exp(-x))


def _rms_norm_rows(x, g):
    return x * lax.rsqrt(jnp.mean(x * x, axis=-1, keepdims=True) + EPS) * g


def _params(*semantics):
    return pltpu.CompilerParams(dimension_semantics=semantics, vmem_limit_bytes=VMEM_LIMIT_BYTES)


def _resident(shape):
    zeros = (0,) * len(shape)
    return pl.BlockSpec(shape, lambda *_: zeros, pipeline_mode=pl.Buffered(1))


def _head_cols(h):
    return slice(h * V_DIM, (h + 1) * V_DIM)


def _head_rows(h, n):
    return pl.ds(h, n, stride=N_HEADS)


def _in_proj_kernel(x_ref, g_ref, w_ref, q_ref, k_ref, v_ref, kb_ref, vb_ref, u_ref):
    tm = x_ref.shape[0]
    h = _rms_norm_rows(x_ref[...], g_ref[...]).astype(BF16)

    def cols(lo, hi):
        return jnp.dot(h, w_ref[:, lo:hi], preferred_element_type=F32)

    a_cols = 2 * QK_WIDTH + WIDTH_A
    u_ref[...] = cols(a_cols, a_cols + WIDTH_B) * _sigmoid(cols(a_cols + WIDTH_B, a_cols + 2 * WIDTH_B))
    k, v = cols(QK_WIDTH, 2 * QK_WIDTH), cols(2 * QK_WIDTH, a_cols)
    for full, out_ref, bf_ref in ((k, k_ref, kb_ref), (v, v_ref, vb_ref)):
        bf_ref[...] = full.astype(BF16)
        for hd in range(N_HEADS):
            out_ref[_head_rows(hd, tm), :] = full[:, _head_cols(hd)]
    q_ref[...] = (cols(0, QK_WIDTH) * HEAD_DIM ** -0.5).astype(BF16)


def _in_proj(x, ln_g, w_in_bf, tm):
    n = x.shape[0]
    rows = lambda width: pl.BlockSpec((tm, width), lambda i: (i, 0))
    by_head = pl.BlockSpec((tm * N_HEADS, V_DIM), lambda i: (i, 0))
    wide = lambda dt: jax.ShapeDtypeStruct((n, QK_WIDTH), dt)
    tall = jax.ShapeDtypeStruct((n * N_HEADS, V_DIM), F32)
    return pl.pallas_call(
        _in_proj_kernel,
        grid=(n // tm,),
        in_specs=[rows(D_MODEL), _resident((1, D_MODEL)), _resident(w_in_bf.shape)],
        out_specs=[rows(QK_WIDTH), by_head, by_head, rows(QK_WIDTH), rows(WIDTH_A), rows(WIDTH_B)],
        out_shape=[wide(BF16), tall, tall, wide(BF16), wide(BF16), wide(F32)],
        compiler_params=_params("parallel"),
        name="in_proj",
    )(x, ln_g, w_in_bf)


def _lambda(lq1_ref, lk1_ref, lq2_ref, lk2_ref):
    d1 = jnp.sum(lq1_ref[...] * lk1_ref[...], axis=-1, keepdims=True)
    d2 = jnp.sum(lq2_ref[...] * lk2_ref[...], axis=-1, keepdims=True)
    return jnp.exp(d1) - jnp.exp(d2) + LAM_INIT


def _stack_maps(qh):
    lane = lax.broadcasted_iota(jnp.int32, qh.shape, 1)
    zero = jnp.zeros_like(qh)
    return jnp.concatenate([jnp.where(lane < HEAD_DIM, qh, zero), jnp.where(lane >= HEAD_DIM, qh, zero)], axis=0)


def _with_ones(v):
    return jnp.concatenate([v, jnp.ones_like(v)], axis=1)


def _finish_head(acc, lam, g, tq):
    o1 = acc[:tq, :V_DIM] / acc[:tq, V_DIM:]
    o2 = acc[tq:, :V_DIM] / acc[tq:, V_DIM:]
    o = o1 - lam * o2
    return _rms_norm_rows(o, g) * (1.0 - LAM_INIT)


def _prompt_attn_kernel(q_ref, k_ref, v_ref, bt_ref, lq1_ref, lk1_ref, lq2_ref, lk2_ref, g_ref,
                        o_ref, q2_sc, m_sc, acc_sc, *, tq):
    tiles_per_step = q_ref.shape[0] // tq

    def key_tile(slot, start, tk, bias_of_head):
        for h in range(N_HEADS):
            hs = _head_cols(h)
            s = lax.dot_general(q2_sc[slot, h], k_ref[start:start + tk, hs], _NT,
                                preferred_element_type=F32)
            if bias_of_head is not None:
                bias = bias_of_head(h)
                s = s + jnp.concatenate([bias, bias], axis=0)
            m_new = jnp.broadcast_to(jnp.max(s, axis=1, keepdims=True), (2 * tq, LANES))
            if start > 0:
                m_prev = m_sc[slot, h]
                m_new = jnp.maximum(m_prev, m_new)
                alpha = jnp.exp(m_prev - m_new)
            p = jnp.exp(s - jnp.concatenate([m_new] * (tk // LANES), axis=1))
            pv = jnp.dot(p.astype(BF16), _with_ones(v_ref[start:start + tk, hs]), preferred_element_type=F32)
            if start > 0:
                pv = jnp.concatenate([alpha, alpha], axis=1) * acc_sc[slot, h] + pv
            acc_sc[slot, h] = pv
            m_sc[slot, h] = m_new

    def query_tile(i):
        slot = i % 2
        q_rows = slice((i % tiles_per_step) * tq, (i % tiles_per_step + 1) * tq)
        for h in range(N_HEADS):
            q2_sc[slot, h] = _stack_maps(q_ref[q_rows, _head_cols(h)])
        n_far = max(i - 1, 0)
        for j in range(n_far // 2):
            key_tile(slot, j * 2 * tq, 2 * tq, None)
        if n_far % 2:
            key_tile(slot, (n_far - 1) * tq, tq, None)
        if i >= 1:
            key_tile(slot, (i - 1) * tq, 2 * tq, lambda h: bt_ref[h])
        else:
            key_tile(slot, 0, tq, lambda h: bt_ref[h, :, tq:])
        lam = _lambda(lq1_ref, lk1_ref, lq2_ref, lk2_ref)
        for h in range(N_HEADS):
            o = _finish_head(acc_sc[slot, h], lam, g_ref[...], tq)
            o_ref[q_rows, _head_cols(h)] = o.astype(o_ref.dtype)

    def step(first_tile):
        for i in range(first_tile, first_tile + tiles_per_step):
            query_tile(i)

    for c in range(k_ref.shape[0] // q_ref.shape[0]):
        pl.when(pl.program_id(1) == c)(functools.partial(step, c * tiles_per_step))


def _prompt_attn(q, kb, vb, bias_tail, lam_params, subln_g, batch, seq, tq, tiles_per_step):
    steps = seq // (tq * tiles_per_step)
    rows = tq * tiles_per_step
    small = [_resident(p.shape) for p in lam_params]
    return pl.pallas_call(
        functools.partial(_prompt_attn_kernel, tq=tq),
        grid=(batch, steps),
        in_specs=[pl.BlockSpec((rows, QK_WIDTH), lambda b, i: (b * steps + i, 0)),
                  pl.BlockSpec((seq, QK_WIDTH), lambda b, i: (b, 0)),
                  pl.BlockSpec((seq, WIDTH_A), lambda b, i: (b, 0)),
                  _resident(bias_tail.shape), *small, _resident(subln_g.shape)],
        out_specs=pl.BlockSpec((rows, WIDTH_A), lambda b, i: (b * steps + i, 0)),
        out_shape=jax.ShapeDtypeStruct((batch * seq, WIDTH_A), BF16),
        scratch_shapes=[pltpu.VMEM((2, N_HEADS, 2 * tq, V_DIM), BF16), pltpu.VMEM((2, N_HEADS, 2 * tq, LANES), F32),
                        pltpu.VMEM((2, N_HEADS, 2 * tq, 2 * V_DIM), F32)],
        compiler_params=_params("parallel", "arbitrary"),
        name="prompt_attn",
    )(q, kb, vb, bias_tail, *lam_params, subln_g)


def _sample_attn_kernel(q_ref, ck_ref, cv_ref, kn_ref, vn_ref, bp_ref, bn_ref, lq1_ref, lk1_ref, lq2_ref, lk2_ref,
                        g_ref, o_ref, *, past_len):
    n_streams = ck_ref.shape[0]
    t = q_ref.shape[0] // n_streams
    lam = _lambda(lq1_ref, lk1_ref, lq2_ref, lk2_ref)
    for s in range(n_streams):
        rows = slice(s * t, (s + 1) * t)
        for h in range(N_HEADS):
            hs = _head_cols(h)
            q2 = _stack_maps(q_ref[rows, hs])
            bias_past, bias_new = bp_ref[h], bn_ref[h]
            k_past = ck_ref[s, _head_rows(h, past_len), :].astype(BF16)
            s_past = lax.dot_general(q2, k_past, _NT, preferred_element_type=F32)
            s_past = s_past + jnp.concatenate([bias_past, bias_past], axis=0)
            s_new = lax.dot_general(q2, kn_ref[rows, hs], _NT, preferred_element_type=F32)
            s_new = s_new + jnp.concatenate([bias_new, bias_new], axis=0)
            m = jnp.maximum(jnp.max(s_past, axis=1, keepdims=True), jnp.max(s_new, axis=1, keepdims=True))
            p_past = jnp.exp(s_past - m).astype(BF16)
            p_new = jnp.exp(s_new - m).astype(BF16)
            v_past = cv_ref[s, _head_rows(h, past_len), :].astype(BF16)
            acc = (jnp.dot(p_past, _with_ones(v_past), preferred_element_type=F32)
                   + jnp.dot(p_new, _with_ones(vn_ref[rows, hs]), preferred_element_type=F32))
            o_ref[rows, hs] = _finish_head(acc, lam, g_ref[...], t).astype(o_ref.dtype)


def _sample_attn(q, cache_k, cache_v, kb, vb, bias_past, bias_new, lam_params, subln_g, past_len, streams):
    batch = cache_k.shape[0]
    t = q.shape[0] // batch
    small = [_resident(p.shape) for p in lam_params]
    new_rows = pl.BlockSpec((streams * t, QK_WIDTH), lambda b: (b, 0))
    cached = pl.BlockSpec((streams, past_len * N_HEADS, V_DIM), lambda b: (b, 0, 0))
    return pl.pallas_call(
        functools.partial(_sample_attn_kernel, past_len=past_len),
        grid=(batch // streams,),
        in_specs=[new_rows, cached, cached, new_rows, new_rows,
                  _resident(bias_past.shape), _resident(bias_new.shape), *small, _resident(subln_g.shape)],
        out_specs=new_rows,
        out_shape=jax.ShapeDtypeStruct(q.shape, BF16),
        compiler_params=_params("parallel"),
        name="sample_attn",
    )(q, cache_k, cache_v, kb, vb, bias_past, bias_new, *lam_params, subln_g)


def _conv_kernel(pre_ref, halo_ref, u_ref, wb_ref, b_ref, g_ref, beta_ref, c_ref, ubuf, ybuf, *, tt, rc):
    n_groups = WIDTH_B // LANES
    n_streams = u_ref.shape[0]
    n_chunks = n_streams * (tt // rc)
    for s in range(n_streams):
        history = jnp.where(pl.program_id(1) == 0, pre_ref[s], halo_ref[s])
        for k in range(tt // rc):
            for lg in range(n_groups):
                ls = slice(lg * LANES, (lg + 1) * LANES)
                slab = (s * (tt // rc) + k) * n_groups + lg
                ubuf[slab, 0:HALO, :] = history[:, ls] if k == 0 else u_ref[s, k * rc - HALO:k * rc, ls]
                ubuf[slab, HALO:, :] = u_ref[s, k * rc:(k + 1) * rc, ls]

    def conv_slab(slab, carry):
        lg = slab % n_groups
        acc = None
        for j in range(CONV_WIDTH):
            first = HALO - CONV_PAD + j
            window = ubuf[slab, first:first + rc, :].reshape(rc // SUBLANES, SUBLANES, LANES)
            term = window * wb_ref[lg, j]
            acc = term if acc is None else acc + term
        ybuf[slab] = acc.reshape(rc, LANES)
        return carry

    unroll = min(n_groups, max(1, VREGS // (rc // SUBLANES)))
    lax.fori_loop(0, n_chunks * n_groups, conv_slab, 0, unroll=unroll)

    for c in range(n_chunks):
        s, k = divmod(c, tt // rc)
        y = jnp.concatenate([ybuf[c * n_groups + lg] for lg in range(n_groups)], axis=1) + b_ref[...]
        mu = jnp.mean(y, axis=-1, keepdims=True)
        var = jnp.mean(jnp.square(y - mu), axis=-1, keepdims=True)
        yn = (y - mu) * lax.rsqrt(var + EPS) * g_ref[...] + beta_ref[...]
        c_ref[s, k * rc:(k + 1) * rc, :] = (yn * _sigmoid(yn)).astype(c_ref.dtype)


def _conv(prefix, u, w_rows, b_dw, ln_g, ln_b, streams, tt, rc):
    batch, seq, _ = u.shape
    halo_blocks = tt // HALO
    slabs = streams * (tt // rc) * (WIDTH_B // LANES)
    small = [_resident(p.shape) for p in (w_rows, b_dw, ln_g, ln_b)]
    return pl.pallas_call(
        functools.partial(_conv_kernel, tt=tt, rc=rc),
        grid=(batch // streams, seq // tt),
        in_specs=[pl.BlockSpec((streams, HALO, WIDTH_B), lambda b, t: (b, 0, 0)),
                  pl.BlockSpec((streams, HALO, WIDTH_B), lambda b, t: (b, jnp.maximum(t * halo_blocks - 1, 0), 0)),
                  pl.BlockSpec((streams, tt, WIDTH_B), lambda b, t: (b, t, 0)), *small],
        out_specs=pl.BlockSpec((streams, tt, WIDTH_B), lambda b, t: (b, t, 0)),
        out_shape=jax.ShapeDtypeStruct(u.shape, BF16),
        scratch_shapes=[pltpu.VMEM((slabs, HALO + rc, LANES), F32), pltpu.VMEM((slabs, rc, LANES), F32)],
        compiler_params=_params("parallel", "parallel"),
        name="conv_ln_swish",
    )(prefix, u, u, w_rows, b_dw, ln_g, ln_b)


def _out_ffn_kernel(x_ref, a_ref, c_ref, wo_ref, g2_ref, wu_ref, wd_ref, gf_ref, y_ref, *, ff_chunk):
    x = (x_ref[...]
         + jnp.dot(a_ref[...], wo_ref[:WIDTH_A, :], preferred_element_type=F32)
         + jnp.dot(c_ref[...], wo_ref[WIDTH_A:, :], preferred_element_type=F32))
    h2 = _rms_norm_rows(x, g2_ref[...]).astype(BF16)
    ffn = None
    for lo in range(0, D_FF, ff_chunk):
        hid = jnp.dot(h2, wu_ref[:, lo:lo + ff_chunk], preferred_element_type=F32)
        hid = jnp.square(jnp.maximum(hid, 0.0)).astype(BF16)
        down = jnp.dot(hid, wd_ref[lo:lo + ff_chunk, :], preferred_element_type=F32)
        ffn = down if ffn is None else ffn + down
    y_ref[...] = _rms_norm_rows(x + ffn, gf_ref[...])


def _out_ffn(x, a, c, w_out_bf, ln2_g, w_up_bf, w_down_bf, ln_f_g, tm, ff_chunk):
    n = x.shape[0]
    rows = lambda width: pl.BlockSpec((tm, width), lambda i: (i, 0))
    return pl.pallas_call(
        functools.partial(_out_ffn_kernel, ff_chunk=ff_chunk),
        grid=(n // tm,),
        in_specs=[rows(D_MODEL), rows(WIDTH_A), rows(WIDTH_B), _resident(w_out_bf.shape), _resident(ln2_g.shape),
                  _resident(w_up_bf.shape), _resident(w_down_bf.shape), _resident(ln_f_g.shape)],
        out_specs=rows(D_MODEL),
        out_shape=jax.ShapeDtypeStruct(x.shape, F32),
        compiler_params=_params("parallel"),
        name="out_ffn",
    )(x, a, c, w_out_bf, ln2_g, w_up_bf, w_down_bf, ln_f_g)


def _bucket_of(rel):
    half = N_BUCKETS // 2
    max_exact = half // 2
    n = -rel
    ret = jnp.where(n < 0, half, 0)
    n = jnp.abs(n)
    nf = jnp.maximum(n, 1).astype(F32)
    large = max_exact + (jnp.log(nf / max_exact) / math.log(MAX_DISTANCE / max_exact)
                         * (half - max_exact)).astype(jnp.int32)
    large = jnp.minimum(large, half - 1)
    return ret + jnp.where(n < max_exact, n, large)


def _bias_table(rel_bias, q0, nq, k0, nk, far_rel):
    span = nq + nk - 1
    shifted = (rel_bias - rel_bias[_bucket_of(far_rel)]).T
    hit = _bucket_of(k0 - (q0 + nq - 1) + jnp.arange(span))[:, None] == jnp.arange(N_BUCKETS)
    by_rel = jnp.sum(jnp.where(hit[None], shifted[:, None, :], 0.0), axis=-1)
    rows = jnp.pad(by_rel, ((0, 0), (0, 1)))
    rows = jnp.broadcast_to(rows[:, None, :], (N_HEADS, nq, span + 1)).reshape(N_HEADS, nq * (span + 1))
    table = rows[:, :nq * span].reshape(N_HEADS, nq, span)[:, :, nq - 1:]
    q_pos, k_pos = q0 + jnp.arange(nq), k0 + jnp.arange(nk)
    visible = (k_pos[None, :] // CHUNK) <= (q_pos[:, None] // CHUNK)
    return jnp.where(visible[None], table, NEG_INF).astype(F32)


class _Tiles(NamedTuple):
    prompt_tokens: int
    sample_tokens: int
    ff_chunk: int
    tq: int
    attn_tiles_per_step: int
    conv_rows: int
    conv_chunk: int
    sample_conv_streams: int
    sample_attn_streams: int


def _tiles(batch, seq, dec_batch, dec_seq, past_len):
    t = _Tiles(prompt_tokens=1024, sample_tokens=512, ff_chunk=1024, tq=256, attn_tiles_per_step=2,
               conv_rows=1024, conv_chunk=256, sample_conv_streams=8, sample_attn_streams=2)
    assert (batch * seq) % t.prompt_tokens == 0 and (dec_batch * dec_seq) % t.sample_tokens == 0
    assert D_FF % t.ff_chunk == 0 and seq % (t.tq * t.attn_tiles_per_step) == 0 and t.tq % LANES == 0
    assert seq % t.conv_rows == 0 and t.conv_rows % t.conv_chunk == 0 and t.conv_chunk % HALO == 0
    assert dec_batch % t.sample_conv_streams == 0 and dec_batch % t.sample_attn_streams == 0
    assert dec_seq % HALO == 0 and seq >= CONV_PAD and dec_seq >= CONV_PAD and past_len % CHUNK == 0
    return t


def kernel(x_prompt, x_sample, cache_k, cache_v, state_conv, rel_bias, ln1_g, w_in, lambda_q1, lambda_k1, lambda_q2,
           lambda_k2, subln_g, w_dw, b_dw, conv_ln_g, conv_ln_b, w_out, ln2_g, w_up, w_down, ln_f_g):
    batch, seq, _ = x_prompt.shape
    dec_batch, dec_seq, _ = x_sample.shape
    past_len = cache_k.shape[2]
    assert w_in.shape[0] == 1, "one layer"
    tiles = _tiles(batch, seq, dec_batch, dec_seq, past_len)
    tq = tiles.tq

    w_in_bf, w_out_bf = w_in[0].astype(BF16), w_out[0].astype(BF16)
    w_up_bf, w_down_bf = w_up[0].astype(BF16), w_down[0].astype(BF16)
    w_taps = w_dw[0].reshape(CONV_WIDTH, WIDTH_B // LANES, 1, LANES).transpose(1, 0, 2, 3)
    w_rows = jnp.broadcast_to(w_taps, (WIDTH_B // LANES, CONV_WIDTH, SUBLANES, LANES))
    lam_params = (lambda_q1, lambda_k1, lambda_q2, lambda_k2)

    far_rel = jnp.int32(-(2 * tq))
    bias_tail = _bias_table(rel_bias, tq, tq, 0, 2 * tq, far_rel)
    bias_past = _bias_table(rel_bias, past_len, dec_seq, 0, past_len, far_rel)
    bias_new = _bias_table(rel_bias, past_len, dec_seq, past_len, dec_seq, far_rel)

    xp = x_prompt.reshape(batch * seq, D_MODEL)
    q_p, k_p, v_p, kb_p, vb_p, u_p = _in_proj(xp, ln1_g, w_in_bf, tm=tiles.prompt_tokens)
    a_p = _prompt_attn(q_p, kb_p, vb_p, bias_tail, lam_params, subln_g, batch, seq, tq, tiles.attn_tiles_per_step)
    u_p3 = u_p.reshape(batch, seq, WIDTH_B)
    c_p = _conv(jnp.zeros((batch, HALO, WIDTH_B), F32), u_p3, w_rows, b_dw, conv_ln_g, conv_ln_b,
                streams=1, tt=tiles.conv_rows, rc=tiles.conv_chunk)
    y_p = _out_ffn(xp, a_p, c_p.reshape(batch * seq, WIDTH_B), w_out_bf, ln2_g, w_up_bf, w_down_bf,
                   ln_f_g[None, :], tm=tiles.prompt_tokens, ff_chunk=tiles.ff_chunk)

    xs = x_sample.reshape(dec_batch * dec_seq, D_MODEL)
    q_s, k_s, v_s, kb_s, vb_s, u_s = _in_proj(xs, ln1_g, w_in_bf, tm=tiles.sample_tokens)
    by_head = lambda cache: cache[0].reshape(dec_batch, past_len * N_HEADS, V_DIM)
    a_s = _sample_attn(q_s, by_head(cache_k), by_head(cache_v), kb_s, vb_s, bias_past, bias_new,
                       lam_params, subln_g, past_len, streams=tiles.sample_attn_streams)
    u_s3 = u_s.reshape(dec_batch, dec_seq, WIDTH_B)
    prefix_s = jnp.pad(state_conv[0], ((0, 0), (HALO - CONV_PAD, 0), (0, 0)))
    c_s = _conv(prefix_s, u_s3, w_rows, b_dw, conv_ln_g, conv_ln_b,
                streams=tiles.sample_conv_streams, tt=dec_seq, rc=dec_seq)
    y_s = _out_ffn(xs, a_s, c_s.reshape(dec_batch * dec_seq, WIDTH_B), w_out_bf, ln2_g, w_up_bf, w_down_bf,
                   ln_f_g[None, :], tm=tiles.sample_tokens, ff_chunk=tiles.ff_chunk)

    heads = lambda t, b, s: t.reshape(1, b, s, N_HEADS, V_DIM)
    return (y_p.reshape(batch, seq, D_MODEL), y_s.reshape(dec_batch, dec_seq, D_MODEL),
            heads(k_p, batch, seq), heads(v_p, batch, seq), u_p3[None, :, seq - CONV_PAD:],
            heads(k_s, dec_batch, dec_seq), heads(v_s, dec_batch, dec_seq), u_s3[None, :, dec_seq - CONV_PAD:])
```

```python
import functools
import math
from typing import NamedTuple

import jax
import jax.numpy as jnp
from jax import lax
from jax.experimental import pallas as pl
from jax.experimental.pallas import tpu as pltpu

D_MODEL = 1024
CHUNK = 64
N_HEADS = 4
HEAD_DIM = 64
V_DIM = 2 * HEAD_DIM
QK_WIDTH = N_HEADS * 2 * HEAD_DIM
WIDTH_A = N_HEADS * V_DIM
WIDTH_B = D_MODEL - WIDTH_A
CONV_WIDTH = 31
CONV_PAD = CONV_WIDTH - 1
D_FF = 4 * D_MODEL
N_BUCKETS = 32
MAX_DISTANCE = 128
EPS = 1e-6
NEG_INF = -1e30
LAM_INIT = 0.8 - 0.6 * math.exp(-0.3 * 0)

LANES = 128
SUBLANES = 8
VREGS = 64
HALO = 32
VMEM_LIMIT_BYTES = 56 * 1024 * 1024

F32 = jnp.float32
BF16 = jnp.bfloat16
_NT = (((1,), (1,)), ((), ()))


def _sigmoid(x):
    return 1.0 / (1.0 + jnp.exp(-x))


def _rms_norm_rows(x, g):
    return x * lax.rsqrt(jnp.mean(x * x, axis=-1, keepdims=True) + EPS) * g


def _params(*semantics):
    return pltpu.CompilerParams(dimension_semantics=semantics, vmem_limit_bytes=VMEM_LIMIT_BYTES)


def _resident(shape):
    zeros = (0,) * len(shape)
    return pl.BlockSpec(shape, lambda *_: zeros, pipeline_mode=pl.Buffered(1))


def _head_cols(h):
    return slice(h * V_DIM, (h + 1) * V_DIM)


def _head_rows(h, n):
    return pl.ds(h, n, stride=N_HEADS)


def _in_proj_kernel(x_ref, g_ref, w_ref, q_ref, k_ref, v_ref, kb_ref, vb_ref, u_ref):
    tm = x_ref.shape[0]
    h = _rms_norm_rows(x_ref[...], g_ref[...]).astype(BF16)

    def cols(lo, hi):
        return jnp.dot(h, w_ref[:, lo:hi], preferred_element_type=F32)

    a_cols = 2 * QK_WIDTH + WIDTH_A
    u_ref[...] = cols(a_cols, a_cols + WIDTH_B) * _sigmoid(cols(a_cols + WIDTH_B, a_cols + 2 * WIDTH_B))
    k, v = cols(QK_WIDTH, 2 * QK_WIDTH), cols(2 * QK_WIDTH, a_cols)
    for full, out_ref, bf_ref in ((k, k_ref, kb_ref), (v, v_ref, vb_ref)):
        bf_ref[...] = full.astype(BF16)
        for hd in range(N_HEADS):
            out_ref[_head_rows(hd, tm), :] = full[:, _head_cols(hd)]
    q_ref[...] = (cols(0, QK_WIDTH) * HEAD_DIM ** -0.5).astype(BF16)


def _in_proj(x, ln_g, w_in_bf, tm):
    n = x.shape[0]
    rows = lambda width: pl.BlockSpec((tm, width), lambda i: (i, 0))
    by_head = pl.BlockSpec((tm * N_HEADS, V_DIM), lambda i: (i, 0))
    wide = lambda dt: jax.ShapeDtypeStruct((n, QK_WIDTH), dt)
    tall = jax.ShapeDtypeStruct((n * N_HEADS, V_DIM), F32)
    return pl.pallas_call(
        _in_proj_kernel,
        grid=(n // tm,),
        in_specs=[rows(D_MODEL), _resident((1, D_MODEL)), _resident(w_in_bf.shape)],
        out_specs=[rows(QK_WIDTH), by_head, by_head, rows(QK_WIDTH), rows(WIDTH_A), rows(WIDTH_B)],
        out_shape=[wide(BF16), tall, tall, wide(BF16), wide(BF16), wide(F32)],
        compiler_params=_params("parallel"),
        name="in_proj",
    )(x, ln_g, w_in_bf)


def _lambda(lq1_ref, lk1_ref, lq2_ref, lk2_ref):
    d1 = jnp.sum(lq1_ref[...] * lk1_ref[...], axis=-1, keepdims=True)
    d2 = jnp.sum(lq2_ref[...] * lk2_ref[...], axis=-1, keepdims=True)
    return jnp.exp(d1) - jnp.exp(d2) + LAM_INIT


def _stack_maps(qh):
    lane = lax.broadcasted_iota(jnp.int32, qh.shape, 1)
    zero = jnp.zeros_like(qh)
    return jnp.concatenate([jnp.where(lane < HEAD_DIM, qh, zero), jnp.where(lane >= HEAD_DIM, qh, zero)], axis=0)


def _with_ones(v):
    return jnp.concatenate([v, jnp.ones_like(v)], axis=1)


def _finish_head(acc, lam, g, tq):
    o1 = acc[:tq, :V_DIM] / acc[:tq, V_DIM:]
    o2 = acc[tq:, :V_DIM] / acc[tq:, V_DIM:]
    o = o1 - lam * o2
    return _rms_norm_rows(o, g) * (1.0 - LAM_INIT)


def _prompt_attn_kernel(q_ref, k_ref, v_ref, bt_ref, lq1_ref, lk1_ref, lq2_ref, lk2_ref, g_ref,
                        o_ref, q2_sc, m_sc, acc_sc, *, tq):
    tiles_per_step = q_ref.shape[0] // tq

    def key_tile(slot, start, tk, bias_of_head):
        for h in range(N_HEADS):
            hs = _head_cols(h)
            s = lax.dot_general(q2_sc[slot, h], k_ref[start:start + tk, hs], _NT,
                                preferred_element_type=F32)
            if bias_of_head is not None:
                bias = bias_of_head(h)
                s = s + jnp.concatenate([bias, bias], axis=0)
            m_new = jnp.broadcast_to(jnp.max(s, axis=1, keepdims=True), (2 * tq, LANES))
            if start > 0:
                m_prev = m_sc[slot, h]
                m_new = jnp.maximum(m_prev, m_new)
                alpha = jnp.exp(m_prev - m_new)
            p = jnp.exp(s - jnp.concatenate([m_new] * (tk // LANES), axis=1))
            pv = jnp.dot(p.astype(BF16), _with_ones(v_ref[start:start + tk, hs]), preferred_element_type=F32)
            if start > 0:
                pv = jnp.concatenate([alpha, alpha], axis=1) * acc_sc[slot, h] + pv
            acc_sc[slot, h] = pv
            m_sc[slot, h] = m_new

    def query_tile(i):
        slot = i % 2
        q_rows = slice((i % tiles_per_step) * tq, (i % tiles_per_step + 1) * tq)
        for h in range(N_HEADS):
            q2_sc[slot, h] = _stack_maps(q_ref[q_rows, _head_cols(h)])
        n_far = max(i - 1, 0)
        for j in range(n_far // 2):
            key_tile(slot, j * 2 * tq, 2 * tq, None)
        if n_far % 2:
            key_tile(slot, (n_far - 1) * tq, tq, None)
        if i >= 1:
            key_tile(slot, (i - 1) * tq, 2 * tq, lambda h: bt_ref[h])
        else:
            key_tile(slot, 0, tq, lambda h: bt_ref[h, :, tq:])
        lam = _lambda(lq1_ref, lk1_ref, lq2_ref, lk2_ref)
        for h in range(N_HEADS):
            o = _finish_head(acc_sc[slot, h], lam, g_ref[...], tq)
            o_ref[q_rows, _head_cols(h)] = o.astype(o_ref.dtype)

    def step(first_tile):
        for i in range(first_tile, first_tile + tiles_per_step):
            query_tile(i)

    for c in range(k_ref.shape[0] // q_ref.shape[0]):
        pl.when(pl.program_id(1) == c)(functools.partial(step, c * tiles_per_step))


def _prompt_attn(q, kb, vb, bias_tail, lam_params, subln_g, batch, seq, tq, tiles_per_step):
    steps = seq // (tq * tiles_per_step)
    rows = tq * tiles_per_step
    small = [_resident(p.shape) for p in lam_params]
    return pl.pallas_call(
        functools.partial(_prompt_attn_kernel, tq=tq),
        grid=(batch, steps),
        in_specs=[pl.BlockSpec((rows, QK_WIDTH), lambda b, i: (b * steps + i, 0)),
                  pl.BlockSpec((seq, QK_WIDTH), lambda b, i: (b, 0)),
                  pl.BlockSpec((seq, WIDTH_A), lambda b, i: (b, 0)),
                  _resident(bias_tail.shape), *small, _resident(subln_g.shape)],
        out_specs=pl.BlockSpec((rows, WIDTH_A), lambda b, i: (b * steps + i, 0)),
        out_shape=jax.ShapeDtypeStruct((batch * seq, WIDTH_A), BF16),
        scratch_shapes=[pltpu.VMEM((2, N_HEADS, 2 * tq, V_DIM), BF16), pltpu.VMEM((2, N_HEADS, 2 * tq, LANES), F32),
                        pltpu.VMEM((2, N_HEADS, 2 * tq, 2 * V_DIM), F32)],
        compiler_params=_params("parallel", "arbitrary"),
        name="prompt_attn",
    )(q, kb, vb, bias_tail, *lam_params, subln_g)


def _sample_attn_kernel(q_ref, ck_ref, cv_ref, kn_ref, vn_ref, bp_ref, bn_ref, lq1_ref, lk1_ref, lq2_ref, lk2_ref,
                        g_ref, o_ref, *, past_len):
    n_streams = ck_ref.shape[0]
    t = q_ref.shape[0] // n_streams
    lam = _lambda(lq1_ref, lk1_ref, lq2_ref, lk2_ref)
    for s in range(n_streams):
        rows = slice(s * t, (s + 1) * t)
        for h in range(N_HEADS):
            hs = _head_cols(h)
            q2 = _stack_maps(q_ref[rows, hs])
            bias_past, bias_new = bp_ref[h], bn_ref[h]
            k_past = ck_ref[s, _head_rows(h, past_len), :].astype(BF16)
            s_past = lax.dot_general(q2, k_past, _NT, preferred_element_type=F32)
            far = past_len - bias_past.shape[1]
            s_recent = s_past[:, far:] + jnp.concatenate([bias_past, bias_past], axis=0)
            s_past = jnp.concatenate([s_past[:, :far], s_recent], axis=1)
            s_new = lax.dot_general(q2, kn_ref[rows, hs], _NT, preferred_element_type=F32)
            s_new = s_new + jnp.concatenate([bias_new, bias_new], axis=0)
            m = jnp.maximum(jnp.max(s_past, axis=1, keepdims=True), jnp.max(s_new, axis=1, keepdims=True))
            p_past = jnp.exp(s_past - m).astype(BF16)
            p_new = jnp.exp(s_new - m).astype(BF16)
            v_past = cv_ref[s, _head_rows(h, past_len), :].astype(BF16)
            acc = (jnp.dot(p_past, _with_ones(v_past), preferred_element_type=F32)
                   + jnp.dot(p_new, _with_ones(vn_ref[rows, hs]), preferred_element_type=F32))
            o_ref[rows, hs] = _finish_head(acc, lam, g_ref[...], t).astype(o_ref.dtype)


def _sample_attn(q, cache_k, cache_v, kb, vb, bias_past, bias_new, lam_params, subln_g, past_len, streams):
    batch = cache_k.shape[0]
    t = q.shape[0] // batch
    small = [_resident(p.shape) for p in lam_params]
    new_rows = pl.BlockSpec((streams * t, QK_WIDTH), lambda b: (b, 0))
    cached = pl.BlockSpec((streams, past_len * N_HEADS, V_DIM), lambda b: (b, 0, 0))
    return pl.pallas_call(
        functools.partial(_sample_attn_kernel, past_len=past_len),
        grid=(batch // streams,),
        in_specs=[new_rows, cached, cached, new_rows, new_rows,
                  _resident(bias_past.shape), _resident(bias_new.shape), *small, _resident(subln_g.shape)],
        out_specs=new_rows,
        out_shape=jax.ShapeDtypeStruct(q.shape, BF16),
        compiler_params=_params("parallel"),
        name="sample_attn",
    )(q, cache_k, cache_v, kb, vb, bias_past, bias_new, *lam_params, subln_g)


def _conv_kernel(pre_ref, halo_ref, u_ref, wb_ref, b_ref, g_ref, beta_ref, c_ref, ubuf, ybuf, *, tt, rc):
    n_groups = WIDTH_B // LANES
    n_streams = u_ref.shape[0]
    n_chunks = n_streams * (tt // rc)
    for s in range(n_streams):
        history = jnp.where(pl.program_id(1) == 0, pre_ref[s], halo_ref[s])
        for k in range(tt // rc):
            for lg in range(n_groups):
                ls = slice(lg * LANES, (lg + 1) * LANES)
                slab = (s * (tt // rc) + k) * n_groups + lg
                ubuf[slab, 0:HALO, :] = history[:, ls] if k == 0 else u_ref[s, k * rc - HALO:k * rc, ls]
                ubuf[slab, HALO:, :] = u_ref[s, k * rc:(k + 1) * rc, ls]

    def conv_slab(slab, carry):
        lg = slab % n_groups
        acc = None
        for j in range(CONV_WIDTH):
            first = HALO - CONV_PAD + j
            window = ubuf[slab, first:first + rc, :].reshape(rc // SUBLANES, SUBLANES, LANES)
            term = window * wb_ref[lg, j]
            acc = term if acc is None else acc + term
        ybuf[slab] = acc.reshape(rc, LANES)
        return carry

    unroll = min(n_groups, max(1, VREGS // (rc // SUBLANES)))
    lax.fori_loop(0, n_chunks * n_groups, conv_slab, 0, unroll=unroll)

    for c in range(n_chunks):
        s, k = divmod(c, tt // rc)
        y = jnp.concatenate([ybuf[c * n_groups + lg] for lg in range(n_groups)], axis=1) + b_ref[...]
        mu = jnp.mean(y, axis=-1, keepdims=True)
        var = jnp.mean(jnp.square(y - mu), axis=-1, keepdims=True)
        yn = (y - mu) * lax.rsqrt(var + EPS) * g_ref[...] + beta_ref[...]
        c_ref[s, k * rc:(k + 1) * rc, :] = (yn * _sigmoid(yn)).astype(c_ref.dtype)


def _conv(prefix, u, w_rows, b_dw, ln_g, ln_b, streams, tt, rc):
    batch, seq, _ = u.shape
    halo_blocks = tt // HALO
    slabs = streams * (tt // rc) * (WIDTH_B // LANES)
    small = [_resident(p.shape) for p in (w_rows, b_dw, ln_g, ln_b)]
    return pl.pallas_call(
        functools.partial(_conv_kernel, tt=tt, rc=rc),
        grid=(batch // streams, seq // tt),
        in_specs=[pl.BlockSpec((streams, HALO, WIDTH_B), lambda b, t: (b, 0, 0)),
                  pl.BlockSpec((streams, HALO, WIDTH_B), lambda b, t: (b, jnp.maximum(t * halo_blocks - 1, 0), 0)),
                  pl.BlockSpec((streams, tt, WIDTH_B), lambda b, t: (b, t, 0)), *small],
        out_specs=pl.BlockSpec((streams, tt, WIDTH_B), lambda b, t: (b, t, 0)),
        out_shape=jax.ShapeDtypeStruct(u.shape, BF16),
        scratch_shapes=[pltpu.VMEM((slabs, HALO + rc, LANES), F32), pltpu.VMEM((slabs, rc, LANES), F32)],
        compiler_params=_params("parallel", "parallel"),
        name="conv_ln_swish",
    )(prefix, u, u, w_rows, b_dw, ln_g, ln_b)


def _out_ffn_kernel(x_ref, a_ref, c_ref, wo_ref, g2_ref, wu_ref, wd_ref, gf_ref, y_ref, *, ff_chunk):
    x = (x_ref[...]
         + jnp.dot(a_ref[...], wo_ref[:WIDTH_A, :], preferred_element_type=F32)
         + jnp.dot(c_ref[...], wo_ref[WIDTH_A:, :], preferred_element_type=F32))
    h2 = _rms_norm_rows(x, g2_ref[...]).astype(BF16)
    ffn = None
    for lo in range(0, D_FF, ff_chunk):
        hid = jnp.dot(h2, wu_ref[:, lo:lo + ff_chunk], preferred_element_type=F32)
        hid = jnp.square(jnp.maximum(hid, 0.0)).astype(BF16)
        down = jnp.dot(hid, wd_ref[lo:lo + ff_chunk, :], preferred_element_type=F32)
        ffn = down if ffn is None else ffn + down
    y_ref[...] = _rms_norm_rows(x + ffn, gf_ref[...])


def _out_ffn(x, a, c, w_out_bf, ln2_g, w_up_bf, w_down_bf, ln_f_g, tm, ff_chunk):
    n = x.shape[0]
    rows = lambda width: pl.BlockSpec((tm, width), lambda i: (i, 0))
    return pl.pallas_call(
        functools.partial(_out_ffn_kernel, ff_chunk=ff_chunk),
        grid=(n // tm,),
        in_specs=[rows(D_MODEL), rows(WIDTH_A), rows(WIDTH_B), _resident(w_out_bf.shape), _resident(ln2_g.shape),
                  _resident(w_up_bf.shape), _resident(w_down_bf.shape), _resident(ln_f_g.shape)],
        out_specs=rows(D_MODEL),
        out_shape=jax.ShapeDtypeStruct(x.shape, F32),
        compiler_params=_params("parallel"),
        name="out_ffn",
    )(x, a, c, w_out_bf, ln2_g, w_up_bf, w_down_bf, ln_f_g)


def _bucket_of(rel):
    half = N_BUCKETS // 2
    max_exact = half // 2
    n = -rel
    ret = jnp.where(n < 0, half, 0)
    n = jnp.abs(n)
    nf = jnp.maximum(n, 1).astype(F32)
    large = max_exact + (jnp.log(nf / max_exact) / math.log(MAX_DISTANCE / max_exact)
                         * (half - max_exact)).astype(jnp.int32)
    large = jnp.minimum(large, half - 1)
    return ret + jnp.where(n < max_exact, n, large)


def _bias_table(rel_bias, q0, nq, k0, nk, far_rel):
    span = nq + nk - 1
    shifted = (rel_bias - rel_bias[_bucket_of(far_rel)]).T
    hit = _bucket_of(k0 - (q0 + nq - 1) + jnp.arange(span))[:, None] == jnp.arange(N_BUCKETS)
    by_rel = jnp.sum(jnp.where(hit[None], shifted[:, None, :], 0.0), axis=-1)
    rows = jnp.pad(by_rel, ((0, 0), (0, 1)))
    rows = jnp.broadcast_to(rows[:, None, :], (N_HEADS, nq, span + 1)).reshape(N_HEADS, nq * (span + 1))
    table = rows[:, :nq * span].reshape(N_HEADS, nq, span)[:, :, nq - 1:]
    q_pos, k_pos = q0 + jnp.arange(nq), k0 + jnp.arange(nk)
    visible = (k_pos[None, :] // CHUNK) <= (q_pos[:, None] // CHUNK)
    return jnp.where(visible[None], table, NEG_INF).astype(F32)


class _Tiles(NamedTuple):
    prompt_tokens: int
    sample_tokens: int
    ff_chunk: int
    tq: int
    attn_tiles_per_step: int
    conv_rows: int
    conv_chunk: int
    sample_conv_streams: int
    sample_attn_streams: int


def _tiles(batch, seq, dec_batch, dec_seq, past_len):
    t = _Tiles(prompt_tokens=1024, sample_tokens=512, ff_chunk=1024, tq=256, attn_tiles_per_step=2,
               conv_rows=1024, conv_chunk=256, sample_conv_streams=8, sample_attn_streams=2)
    assert past_len >= MAX_DISTANCE and MAX_DISTANCE % LANES == 0
    assert (batch * seq) % t.prompt_tokens == 0 and (dec_batch * dec_seq) % t.sample_tokens == 0
    assert D_FF % t.ff_chunk == 0 and seq % (t.tq * t.attn_tiles_per_step) == 0 and t.tq % LANES == 0
    assert seq % t.conv_rows == 0 and t.conv_rows % t.conv_chunk == 0 and t.conv_chunk % HALO == 0
    assert dec_batch % t.sample_conv_streams == 0 and dec_batch % t.sample_attn_streams == 0
    assert dec_seq % HALO == 0 and seq >= CONV_PAD and dec_seq >= CONV_PAD and past_len % CHUNK == 0
    return t


def kernel(x_prompt, x_sample, cache_k, cache_v, state_conv, rel_bias, ln1_g, w_in, lambda_q1, lambda_k1, lambda_q2,
           lambda_k2, subln_g, w_dw, b_dw, conv_ln_g, conv_ln_b, w_out, ln2_g, w_up, w_down, ln_f_g):
    batch, seq, _ = x_prompt.shape
    dec_batch, dec_seq, _ = x_sample.shape
    past_len = cache_k.shape[2]
    assert w_in.shape[0] == 1, "one layer"
    tiles = _tiles(batch, seq, dec_batch, dec_seq, past_len)
    tq = tiles.tq

    w_in_bf, w_out_bf = w_in[0].astype(BF16), w_out[0].astype(BF16)
    w_up_bf, w_down_bf = w_up[0].astype(BF16), w_down[0].astype(BF16)
    w_taps = w_dw[0].reshape(CONV_WIDTH, WIDTH_B // LANES, 1, LANES).transpose(1, 0, 2, 3)
    w_rows = jnp.broadcast_to(w_taps, (WIDTH_B // LANES, CONV_WIDTH, SUBLANES, LANES))
    lam_params = (lambda_q1, lambda_k1, lambda_q2, lambda_k2)

    far_rel = jnp.int32(-(2 * tq))
    bias_tail = _bias_table(rel_bias, tq, tq, 0, 2 * tq, far_rel)
    bias_past = _bias_table(rel_bias, past_len, dec_seq, past_len - MAX_DISTANCE, MAX_DISTANCE, far_rel)
    bias_new = _bias_table(rel_bias, past_len, dec_seq, past_len, dec_seq, far_rel)

    xp = x_prompt.reshape(batch * seq, D_MODEL)
    q_p, k_p, v_p, kb_p, vb_p, u_p = _in_proj(xp, ln1_g, w_in_bf, tm=tiles.prompt_tokens)
    a_p = _prompt_attn(q_p, kb_p, vb_p, bias_tail, lam_params, subln_g, batch, seq, tq, tiles.attn_tiles_per_step)
    u_p3 = u_p.reshape(batch, seq, WIDTH_B)
    c_p = _conv(jnp.zeros((batch, HALO, WIDTH_B), F32), u_p3, w_rows, b_dw, conv_ln_g, conv_ln_b,
                streams=1, tt=tiles.conv_rows, rc=tiles.conv_chunk)
    y_p = _out_ffn(xp, a_p, c_p.reshape(batch * seq, WIDTH_B), w_out_bf, ln2_g, w_up_bf, w_down_bf,
                   ln_f_g[None, :], tm=tiles.prompt_tokens, ff_chunk=tiles.ff_chunk)

    xs = x_sample.reshape(dec_batch * dec_seq, D_MODEL)
    q_s, k_s, v_s, kb_s, vb_s, u_s = _in_proj(xs, ln1_g, w_in_bf, tm=tiles.sample_tokens)
    by_head = lambda cache: cache[0].reshape(dec_batch, past_len * N_HEADS, V_DIM)
    a_s = _sample_attn(q_s, by_head(cache_k), by_head(cache_v), kb_s, vb_s, bias_past, bias_new,
                       lam_params, subln_g, past_len, streams=tiles.sample_attn_streams)
    u_s3 = u_s.reshape(dec_batch, dec_seq, WIDTH_B)
    prefix_s = jnp.pad(state_conv[0], ((0, 0), (HALO - CONV_PAD, 0), (0, 0)))
    c_s = _conv(prefix_s, u_s3, w_rows, b_dw, conv_ln_g, conv_ln_b,
                streams=tiles.sample_conv_streams, tt=dec_seq, rc=dec_seq)
    y_s = _out_ffn(xs, a_s, c_s.reshape(dec_batch * dec_seq, WIDTH_B), w_out_bf, ln2_g, w_up_bf, w_down_bf,
                   ln_f_g[None, :], tm=tiles.sample_tokens, ff_chunk=tiles.ff_chunk)

    heads = lambda t, b, s: t.reshape(1, b, s, N_HEADS, V_DIM)
    return (y_p.reshape(batch, seq, D_MODEL), y_s.reshape(dec_batch, dec_seq, D_MODEL),
            heads(k_p, batch, seq), heads(v_p, batch, seq), u_p3[None, :, seq - CONV_PAD:],
            heads(k_s, dec_batch, dec_seq), heads(v_s, dec_batch, dec_seq), u_s3[None, :, dec_seq - CONV_PAD:])
```

```python
import functools
import math
from typing import NamedTuple

import jax
import jax.numpy as jnp
from jax import lax
from jax.experimental import pallas as pl
from jax.experimental.pallas import tpu as pltpu

D_MODEL = 1024
CHUNK = 64
N_HEADS = 4
HEAD_DIM = 64
V_DIM = 2 * HEAD_DIM
QK_WIDTH = N_HEADS * 2 * HEAD_DIM
WIDTH_A = N_HEADS * V_DIM
WIDTH_B = D_MODEL - WIDTH_A
CONV_WIDTH = 31
CONV_PAD = CONV_WIDTH - 1
D_FF = 4 * D_MODEL
N_BUCKETS = 32
MAX_DISTANCE = 128
EPS = 1e-6
NEG_INF = -1e30
LAM_INIT = 0.8 - 0.6 * math.exp(-0.3 * 0)

LANES = 128
SUBLANES = 8
VREGS = 64
HALO = 32
VMEM_LIMIT_BYTES = 56 * 1024 * 1024

F32 = jnp.float32
BF16 = jnp.bfloat16
_NT = (((1,), (1,)), ((), ()))


def _sigmoid(x):
    return 1.0 / (1.0 + jnp.exp(-x))


def _rms_norm_rows(x, g):
    return x * lax.rsqrt(jnp.mean(x * x, axis=-1, keepdims=True) + EPS) * g


def _params(*semantics):
    return pltpu.CompilerParams(dimension_semantics=semantics, vmem_limit_bytes=VMEM_LIMIT_BYTES)


def _resident(shape):
    zeros = (0,) * len(shape)
    return pl.BlockSpec(shape, lambda *_: zeros, pipeline_mode=pl.Buffered(1))


def _head_cols(h):
    return slice(h * V_DIM, (h + 1) * V_DIM)


def _head_rows(h, n):
    return pl.ds(h, n, stride=N_HEADS)


def _in_proj_kernel(x_ref, g_ref, w_ref, q_ref, k_ref, v_ref, kb_ref, vb_ref, u_ref):
    tm = x_ref.shape[0]
    h = _rms_norm_rows(x_ref[...], g_ref[...]).astype(BF16)

    def cols(lo, hi):
        return jnp.dot(h, w_ref[:, lo:hi], preferred_element_type=F32)

    a_cols = 2 * QK_WIDTH + WIDTH_A
    u_ref[...] = cols(a_cols, a_cols + WIDTH_B) * _sigmoid(cols(a_cols + WIDTH_B, a_cols + 2 * WIDTH_B))
    k, v = cols(QK_WIDTH, 2 * QK_WIDTH), cols(2 * QK_WIDTH, a_cols)
    for full, out_ref, bf_ref in ((k, k_ref, kb_ref), (v, v_ref, vb_ref)):
        bf_ref[...] = full.astype(BF16)
        for hd in range(N_HEADS):
            out_ref[_head_rows(hd, tm), :] = full[:, _head_cols(hd)]
    q_ref[...] = (cols(0, QK_WIDTH) * HEAD_DIM ** -0.5).astype(BF16)


def _in_proj(x, ln_g, w_in_bf, tm):
    n = x.shape[0]
    rows = lambda width: pl.BlockSpec((tm, width), lambda i: (i, 0))
    by_head = pl.BlockSpec((tm * N_HEADS, V_DIM), lambda i: (i, 0))
    wide = lambda dt: jax.ShapeDtypeStruct((n, QK_WIDTH), dt)
    tall = jax.ShapeDtypeStruct((n * N_HEADS, V_DIM), F32)
    return pl.pallas_call(
        _in_proj_kernel,
        grid=(n // tm,),
        in_specs=[rows(D_MODEL), _resident((1, D_MODEL)), _resident(w_in_bf.shape)],
        out_specs=[rows(QK_WIDTH), by_head, by_head, rows(QK_WIDTH), rows(WIDTH_A), rows(WIDTH_B)],
        out_shape=[wide(BF16), tall, tall, wide(BF16), wide(BF16), wide(F32)],
        compiler_params=_params("parallel"),
        name="in_proj",
    )(x, ln_g, w_in_bf)


def _lambda(lq1_ref, lk1_ref, lq2_ref, lk2_ref):
    d1 = jnp.sum(lq1_ref[...] * lk1_ref[...], axis=-1, keepdims=True)
    d2 = jnp.sum(lq2_ref[...] * lk2_ref[...], axis=-1, keepdims=True)
    return jnp.exp(d1) - jnp.exp(d2) + LAM_INIT


def _stack_maps(qh):
    lane = lax.broadcasted_iota(jnp.int32, qh.shape, 1)
    zero = jnp.zeros_like(qh)
    return jnp.concatenate([jnp.where(lane < HEAD_DIM, qh, zero), jnp.where(lane >= HEAD_DIM, qh, zero)], axis=0)


def _with_ones(v):
    return jnp.concatenate([v, jnp.ones_like(v)], axis=1)


def _finish_head(acc, lam, g, tq):
    o1 = acc[:tq, :V_DIM] / acc[:tq, V_DIM:]
    o2 = acc[tq:, :V_DIM] / acc[tq:, V_DIM:]
    o = o1 - lam * o2
    return _rms_norm_rows(o, g) * (1.0 - LAM_INIT)


def _prompt_attn_kernel(q_ref, k_ref, v_ref, bt_ref, lq1_ref, lk1_ref, lq2_ref, lk2_ref, g_ref,
                        o_ref, q2_sc, m_sc, acc_sc, *, tq):
    tiles_per_step = q_ref.shape[0] // tq

    def key_tile(slot, start, tk, bias_of_head):
        for h in range(N_HEADS):
            hs = _head_cols(h)
            s = lax.dot_general(q2_sc[slot, h], k_ref[start:start + tk, hs], _NT,
                                preferred_element_type=F32)
            if bias_of_head is not None:
                bias = bias_of_head(h)
                s = s + jnp.concatenate([bias, bias], axis=0)
            m_new = jnp.broadcast_to(jnp.max(s, axis=1, keepdims=True), (2 * tq, LANES))
            if start > 0:
                m_prev = m_sc[slot, h]
                m_new = jnp.maximum(m_prev, m_new)
                alpha = jnp.exp(m_prev - m_new)
            p = jnp.exp(s - jnp.concatenate([m_new] * (tk // LANES), axis=1))
            pv = jnp.dot(p.astype(BF16), _with_ones(v_ref[start:start + tk, hs]), preferred_element_type=F32)
            if start > 0:
                pv = jnp.concatenate([alpha, alpha], axis=1) * acc_sc[slot, h] + pv
            acc_sc[slot, h] = pv
            m_sc[slot, h] = m_new

    def query_tile(i):
        slot = i % 2
        q_rows = slice((i % tiles_per_step) * tq, (i % tiles_per_step + 1) * tq)
        for h in range(N_HEADS):
            q2_sc[slot, h] = _stack_maps(q_ref[q_rows, _head_cols(h)])
        n_far = max(i - 1, 0)
        for j in range(n_far // 2):
            key_tile(slot, j * 2 * tq, 2 * tq, None)
        if n_far % 2:
            key_tile(slot, (n_far - 1) * tq, tq, None)
        if i >= 1:
            key_tile(slot, (i - 1) * tq, 2 * tq, lambda h: bt_ref[h])
        else:
            key_tile(slot, 0, tq, lambda h: bt_ref[h, :, tq:])
        lam = _lambda(lq1_ref, lk1_ref, lq2_ref, lk2_ref)
        for h in range(N_HEADS):
            o = _finish_head(acc_sc[slot, h], lam, g_ref[...], tq)
            o_ref[q_rows, _head_cols(h)] = o.astype(o_ref.dtype)

    def step(first_tile):
        for i in range(first_tile, first_tile + tiles_per_step):
            query_tile(i)

    for c in range(k_ref.shape[0] // q_ref.shape[0]):
        pl.when(pl.program_id(1) == c)(functools.partial(step, c * tiles_per_step))


def _prompt_attn(q, kb, vb, bias_tail, lam_params, subln_g, batch, seq, tq, tiles_per_step):
    steps = seq // (tq * tiles_per_step)
    rows = tq * tiles_per_step
    small = [_resident(p.shape) for p in lam_params]
    return pl.pallas_call(
        functools.partial(_prompt_attn_kernel, tq=tq),
        grid=(batch, steps),
        in_specs=[pl.BlockSpec((rows, QK_WIDTH), lambda b, i: (b * steps + i, 0)),
                  pl.BlockSpec((seq, QK_WIDTH), lambda b, i: (b, 0)),
                  pl.BlockSpec((seq, WIDTH_A), lambda b, i: (b, 0)),
                  _resident(bias_tail.shape), *small, _resident(subln_g.shape)],
        out_specs=pl.BlockSpec((rows, WIDTH_A), lambda b, i: (b * steps + i, 0)),
        out_shape=jax.ShapeDtypeStruct((batch * seq, WIDTH_A), BF16),
        scratch_shapes=[pltpu.VMEM((2, N_HEADS, 2 * tq, V_DIM), BF16), pltpu.VMEM((2, N_HEADS, 2 * tq, LANES), F32),
                        pltpu.VMEM((2, N_HEADS, 2 * tq, 2 * V_DIM), F32)],
        compiler_params=_params("parallel", "arbitrary"),
        name="prompt_attn",
    )(q, kb, vb, bias_tail, *lam_params, subln_g)


CACHE_SLOTS = 3


def _sample_attn_kernel(q_ref, ck_hbm, cv_hbm, kn_ref, vn_ref, bp_ref, bn_ref, lq1_ref, lk1_ref, lq2_ref, lk2_ref,
                        g_ref, o_ref, kbuf, vbuf, sems, *, past_len):
    n_streams = ck_hbm.shape[0]
    t = q_ref.shape[0] // n_streams
    lam = _lambda(lq1_ref, lk1_ref, lq2_ref, lk2_ref)

    def fetch(stream, slot):
        return (pltpu.make_async_copy(ck_hbm.at[stream], kbuf.at[slot], sems.at[0, slot]),
                pltpu.make_async_copy(cv_hbm.at[stream], vbuf.at[slot], sems.at[1, slot]))

    for ahead in range(CACHE_SLOTS - 1):
        for copy in fetch(ahead, ahead):
            copy.start()

    def one_stream(s, carry):
        slot = s % CACHE_SLOTS
        for copy in fetch(s, slot):
            copy.wait()

        @pl.when(s + CACHE_SLOTS - 1 < n_streams)
        def _():
            for copy in fetch(s + CACHE_SLOTS - 1, (s + CACHE_SLOTS - 1) % CACHE_SLOTS):
                copy.start()

        rows = pl.ds(pl.multiple_of(s * t, t), t)
        for h in range(N_HEADS):
            hs = _head_cols(h)
            q2 = _stack_maps(q_ref[rows, hs])
            bias_past, bias_new = bp_ref[h], bn_ref[h]
            k_past = kbuf[slot, _head_rows(h, past_len), :].astype(BF16)
            s_past = lax.dot_general(q2, k_past, _NT, preferred_element_type=F32)
            far = past_len - bias_past.shape[1]
            s_recent = s_past[:, far:] + jnp.concatenate([bias_past, bias_past], axis=0)
            s_past = jnp.concatenate([s_past[:, :far], s_recent], axis=1)
            s_new = lax.dot_general(q2, kn_ref[rows, hs], _NT, preferred_element_type=F32)
            s_new = s_new + jnp.concatenate([bias_new, bias_new], axis=0)
            m = jnp.maximum(jnp.max(s_past, axis=1, keepdims=True), jnp.max(s_new, axis=1, keepdims=True))
            p_past = jnp.exp(s_past - m).astype(BF16)
            p_new = jnp.exp(s_new - m).astype(BF16)
            v_past = vbuf[slot, _head_rows(h, past_len), :].astype(BF16)
            acc = (jnp.dot(p_past, _with_ones(v_past), preferred_element_type=F32)
                   + jnp.dot(p_new, _with_ones(vn_ref[rows, hs]), preferred_element_type=F32))
            o_ref[rows, hs] = _finish_head(acc, lam, g_ref[...], t).astype(o_ref.dtype)
        return carry

    lax.fori_loop(0, n_streams, one_stream, 0)


def _sample_attn(q, cache_k, cache_v, kb, vb, bias_past, bias_new, lam_params, subln_g, past_len):
    block = (past_len * N_HEADS, V_DIM)
    whole = lambda a: pl.BlockSpec(a.shape, lambda: (0,) * a.ndim)
    hbm = pl.BlockSpec(memory_space=pl.ANY)
    operands = (q, cache_k, cache_v, kb, vb, bias_past, bias_new, *lam_params, subln_g)
    return pl.pallas_call(
        functools.partial(_sample_attn_kernel, past_len=past_len),
        in_specs=[hbm if a is cache_k or a is cache_v else whole(a) for a in operands],
        out_specs=whole(q),
        out_shape=jax.ShapeDtypeStruct(q.shape, BF16),
        scratch_shapes=[pltpu.VMEM((CACHE_SLOTS,) + block, F32), pltpu.VMEM((CACHE_SLOTS,) + block, F32),
                        pltpu.SemaphoreType.DMA((2, CACHE_SLOTS))],
        compiler_params=pltpu.CompilerParams(vmem_limit_bytes=VMEM_LIMIT_BYTES),
        name="sample_attn",
    )(*operands)


def _conv_kernel(pre_ref, halo_ref, u_ref, wb_ref, b_ref, g_ref, beta_ref, c_ref, ubuf, ybuf, *, tt, rc):
    n_groups = WIDTH_B // LANES
    n_streams = u_ref.shape[0]
    n_chunks = n_streams * (tt // rc)
    for s in range(n_streams):
        history = jnp.where(pl.program_id(1) == 0, pre_ref[s], halo_ref[s])
        for k in range(tt // rc):
            for lg in range(n_groups):
                ls = slice(lg * LANES, (lg + 1) * LANES)
                slab = (s * (tt // rc) + k) * n_groups + lg
                ubuf[slab, 0:HALO, :] = history[:, ls] if k == 0 else u_ref[s, k * rc - HALO:k * rc, ls]
                ubuf[slab, HALO:, :] = u_ref[s, k * rc:(k + 1) * rc, ls]

    def conv_slab(slab, carry):
        lg = slab % n_groups
        acc = None
        for j in range(CONV_WIDTH):
            first = HALO - CONV_PAD + j
            window = ubuf[slab, first:first + rc, :].reshape(rc // SUBLANES, SUBLANES, LANES)
            term = window * wb_ref[lg, j]
            acc = term if acc is None else acc + term
        ybuf[slab] = acc.reshape(rc, LANES)
        return carry

    unroll = min(n_groups, max(1, VREGS // (rc // SUBLANES)))
    lax.fori_loop(0, n_chunks * n_groups, conv_slab, 0, unroll=unroll)

    for c in range(n_chunks):
        s, k = divmod(c, tt // rc)
        y = jnp.concatenate([ybuf[c * n_groups + lg] for lg in range(n_groups)], axis=1) + b_ref[...]
        mu = jnp.mean(y, axis=-1, keepdims=True)
        var = jnp.mean(jnp.square(y - mu), axis=-1, keepdims=True)
        yn = (y - mu) * lax.rsqrt(var + EPS) * g_ref[...] + beta_ref[...]
        c_ref[s, k * rc:(k + 1) * rc, :] = (yn * _sigmoid(yn)).astype(c_ref.dtype)


def _conv(prefix, u, w_rows, b_dw, ln_g, ln_b, streams, tt, rc):
    batch, seq, _ = u.shape
    halo_blocks = tt // HALO
    slabs = streams * (tt // rc) * (WIDTH_B // LANES)
    small = [_resident(p.shape) for p in (w_rows, b_dw, ln_g, ln_b)]
    return pl.pallas_call(
        functools.partial(_conv_kernel, tt=tt, rc=rc),
        grid=(batch // streams, seq // tt),
        in_specs=[pl.BlockSpec((streams, HALO, WIDTH_B), lambda b, t: (b, 0, 0)),
                  pl.BlockSpec((streams, HALO, WIDTH_B), lambda b, t: (b, jnp.maximum(t * halo_blocks - 1, 0), 0)),
                  pl.BlockSpec((streams, tt, WIDTH_B), lambda b, t: (b, t, 0)), *small],
        out_specs=pl.BlockSpec((streams, tt, WIDTH_B), lambda b, t: (b, t, 0)),
        out_shape=jax.ShapeDtypeStruct(u.shape, BF16),
        scratch_shapes=[pltpu.VMEM((slabs, HALO + rc, LANES), F32), pltpu.VMEM((slabs, rc, LANES), F32)],
        compiler_params=_params("parallel", "parallel"),
        name="conv_ln_swish",
    )(prefix, u, u, w_rows, b_dw, ln_g, ln_b)


def _out_ffn_kernel(x_ref, a_ref, c_ref, wo_ref, g2_ref, wu_ref, wd_ref, gf_ref, y_ref, *, ff_chunk):
    x = (x_ref[...]
         + jnp.dot(a_ref[...], wo_ref[:WIDTH_A, :], preferred_element_type=F32)
         + jnp.dot(c_ref[...], wo_ref[WIDTH_A:, :], preferred_element_type=F32))
    h2 = _rms_norm_rows(x, g2_ref[...]).astype(BF16)
    ffn = None
    for lo in range(0, D_FF, ff_chunk):
        hid = jnp.dot(h2, wu_ref[:, lo:lo + ff_chunk], preferred_element_type=F32)
        hid = jnp.square(jnp.maximum(hid, 0.0)).astype(BF16)
        down = jnp.dot(hid, wd_ref[lo:lo + ff_chunk, :], preferred_element_type=F32)
        ffn = down if ffn is None else ffn + down
    y_ref[...] = _rms_norm_rows(x + ffn, gf_ref[...])


def _out_ffn(x, a, c, w_out_bf, ln2_g, w_up_bf, w_down_bf, ln_f_g, tm, ff_chunk):
    n = x.shape[0]
    rows = lambda width: pl.BlockSpec((tm, width), lambda i: (i, 0))
    return pl.pallas_call(
        functools.partial(_out_ffn_kernel, ff_chunk=ff_chunk),
        grid=(n // tm,),
        in_specs=[rows(D_MODEL), rows(WIDTH_A), rows(WIDTH_B), _resident(w_out_bf.shape), _resident(ln2_g.shape),
                  _resident(w_up_bf.shape), _resident(w_down_bf.shape), _resident(ln_f_g.shape)],
        out_specs=rows(D_MODEL),
        out_shape=jax.ShapeDtypeStruct(x.shape, F32),
        compiler_params=_params("parallel"),
        name="out_ffn",
    )(x, a, c, w_out_bf, ln2_g, w_up_bf, w_down_bf, ln_f_g)


def _bucket_of(rel):
    half = N_BUCKETS // 2
    max_exact = half // 2
    n = -rel
    ret = jnp.where(n < 0, half, 0)
    n = jnp.abs(n)
    nf = jnp.maximum(n, 1).astype(F32)
    large = max_exact + (jnp.log(nf / max_exact) / math.log(MAX_DISTANCE / max_exact)
                         * (half - max_exact)).astype(jnp.int32)
    large = jnp.minimum(large, half - 1)
    return ret + jnp.where(n < max_exact, n, large)


def _bias_table(rel_bias, q0, nq, k0, nk, far_rel):
    span = nq + nk - 1
    shifted = (rel_bias - rel_bias[_bucket_of(far_rel)]).T
    hit = _bucket_of(k0 - (q0 + nq - 1) + jnp.arange(span))[:, None] == jnp.arange(N_BUCKETS)
    by_rel = jnp.sum(jnp.where(hit[None], shifted[:, None, :], 0.0), axis=-1)
    rows = jnp.pad(by_rel, ((0, 0), (0, 1)))
    rows = jnp.broadcast_to(rows[:, None, :], (N_HEADS, nq, span + 1)).reshape(N_HEADS, nq * (span + 1))
    table = rows[:, :nq * span].reshape(N_HEADS, nq, span)[:, :, nq - 1:]
    q_pos, k_pos = q0 + jnp.arange(nq), k0 + jnp.arange(nk)
    visible = (k_pos[None, :] // CHUNK) <= (q_pos[:, None] // CHUNK)
    return jnp.where(visible[None], table, NEG_INF).astype(F32)


class _Tiles(NamedTuple):
    prompt_tokens: int
    sample_tokens: int
    ff_chunk: int
    tq: int
    attn_tiles_per_step: int
    conv_rows: int
    conv_chunk: int
    sample_conv_streams: int
    sample_attn_streams: int


def _tiles(batch, seq, dec_batch, dec_seq, past_len):
    t = _Tiles(prompt_tokens=1024, sample_tokens=1024, ff_chunk=1024, tq=256, attn_tiles_per_step=2,
               conv_rows=1024, conv_chunk=256, sample_conv_streams=8, sample_attn_streams=1)
    assert past_len >= MAX_DISTANCE and MAX_DISTANCE % LANES == 0
    assert (batch * seq) % t.prompt_tokens == 0 and (dec_batch * dec_seq) % t.sample_tokens == 0
    assert D_FF % t.ff_chunk == 0 and seq % (t.tq * t.attn_tiles_per_step) == 0 and t.tq % LANES == 0
    assert seq % t.conv_rows == 0 and t.conv_rows % t.conv_chunk == 0 and t.conv_chunk % HALO == 0
    assert dec_batch % t.sample_conv_streams == 0 and dec_batch % t.sample_attn_streams == 0
    assert dec_seq % HALO == 0 and seq >= CONV_PAD and dec_seq >= CONV_PAD and past_len % CHUNK == 0
    return t


def kernel(x_prompt, x_sample, cache_k, cache_v, state_conv, rel_bias, ln1_g, w_in, lambda_q1, lambda_k1, lambda_q2,
           lambda_k2, subln_g, w_dw, b_dw, conv_ln_g, conv_ln_b, w_out, ln2_g, w_up, w_down, ln_f_g):
    batch, seq, _ = x_prompt.shape
    dec_batch, dec_seq, _ = x_sample.shape
    past_len = cache_k.shape[2]
    assert w_in.shape[0] == 1, "one layer"
    tiles = _tiles(batch, seq, dec_batch, dec_seq, past_len)
    tq = tiles.tq

    w_in_bf, w_out_bf = w_in[0].astype(BF16), w_out[0].astype(BF16)
    w_up_bf, w_down_bf = w_up[0].astype(BF16), w_down[0].astype(BF16)
    w_taps = w_dw[0].reshape(CONV_WIDTH, WIDTH_B // LANES, 1, LANES).transpose(1, 0, 2, 3)
    w_rows = jnp.broadcast_to(w_taps, (WIDTH_B // LANES, CONV_WIDTH, SUBLANES, LANES))
    lam_params = (lambda_q1, lambda_k1, lambda_q2, lambda_k2)

    far_rel = jnp.int32(-(2 * tq))
    bias_tail = _bias_table(rel_bias, tq, tq, 0, 2 * tq, far_rel)
    bias_past = _bias_table(rel_bias, past_len, dec_seq, past_len - MAX_DISTANCE, MAX_DISTANCE, far_rel)
    bias_new = _bias_table(rel_bias, past_len, dec_seq, past_len, dec_seq, far_rel)

    xp = x_prompt.reshape(batch * seq, D_MODEL)
    q_p, k_p, v_p, kb_p, vb_p, u_p = _in_proj(xp, ln1_g, w_in_bf, tm=tiles.prompt_tokens)
    a_p = _prompt_attn(q_p, kb_p, vb_p, bias_tail, lam_params, subln_g, batch, seq, tq, tiles.attn_tiles_per_step)
    u_p3 = u_p.reshape(batch, seq, WIDTH_B)
    c_p = _conv(jnp.zeros((batch, HALO, WIDTH_B), F32), u_p3, w_rows, b_dw, conv_ln_g, conv_ln_b,
                streams=1, tt=tiles.conv_rows, rc=tiles.conv_chunk)
    y_p = _out_ffn(xp, a_p, c_p.reshape(batch * seq, WIDTH_B), w_out_bf, ln2_g, w_up_bf, w_down_bf,
                   ln_f_g[None, :], tm=tiles.prompt_tokens, ff_chunk=tiles.ff_chunk)

    xs = x_sample.reshape(dec_batch * dec_seq, D_MODEL)
    q_s, k_s, v_s, kb_s, vb_s, u_s = _in_proj(xs, ln1_g, w_in_bf, tm=tiles.sample_tokens)
    by_head = lambda cache: cache[0].reshape(dec_batch, past_len * N_HEADS, V_DIM)
    a_s = _sample_attn(q_s, by_head(cache_k), by_head(cache_v), kb_s, vb_s, bias_past, bias_new,
                       lam_params, subln_g, past_len)
    u_s3 = u_s.reshape(dec_batch, dec_seq, WIDTH_B)
    prefix_s = jnp.pad(state_conv[0], ((0, 0), (HALO - CONV_PAD, 0), (0, 0)))
    c_s = _conv(prefix_s, u_s3, w_rows, b_dw, conv_ln_g, conv_ln_b,
                streams=tiles.sample_conv_streams, tt=dec_seq, rc=dec_seq)
    y_s = _out_ffn(xs, a_s, c_s.reshape(dec_batch * dec_seq, WIDTH_B), w_out_bf, ln2_g, w_up_bf, w_down_bf,
                   ln_f_g[None, :], tm=tiles.sample_tokens, ff_chunk=tiles.ff_chunk)

    heads = lambda t, b, s: t.reshape(1, b, s, N_HEADS, V_DIM)
    return (y_p.reshape(batch, seq, D_MODEL), y_s.reshape(dec_batch, dec_seq, D_MODEL),
            heads(k_p, batch, seq), heads(v_p, batch, seq), u_p3[None, :, seq - CONV_PAD:],
            heads(k_s, dec_batch, dec_seq), heads(v_s, dec_batch, dec_seq), u_s3[None, :, dec_seq - CONV_PAD:])
```
